```python
import math
import jax, jax.numpy as jnp
from jax import lax
import numpy as np

D_MODEL = 2048
BATCH = 8
SEQ = 2048
DEPTH = 1

N_ATT_HEADS = D_MODEL // 256
QK_NOPE_DIM = 128
QK_ROPE_DIM = 64
QK_HEAD_DIM = QK_NOPE_DIM + QK_ROPE_DIM
V_HEAD_DIM = 128
Q_LORA_RANK = D_MODEL // 4
KV_LORA_RANK = D_MODEL // 8
ATT_WIDTH = N_ATT_HEADS * V_HEAD_DIM
ATT_Q_BLOCK = 128
ROPE_THETA = 10000.0
N_REC_HEADS = D_MODEL // 256
REC_KEY_DIM = 128
REC_VAL_DIM = 128
REC_QK = N_REC_HEADS * REC_KEY_DIM
REC_WIDTH = N_REC_HEADS * REC_VAL_DIM
REC_CHUNK = 64
MIX_WIDTH = ATT_WIDTH + REC_WIDTH
IN_COLS = Q_LORA_RANK + KV_LORA_RANK + QK_ROPE_DIM + 3 * REC_QK + 2 * REC_WIDTH
IN_SPLITS = (
    Q_LORA_RANK,
    Q_LORA_RANK + KV_LORA_RANK,
    Q_LORA_RANK + KV_LORA_RANK + QK_ROPE_DIM,
    Q_LORA_RANK + KV_LORA_RANK + QK_ROPE_DIM + REC_QK,
    Q_LORA_RANK + KV_LORA_RANK + QK_ROPE_DIM + 2 * REC_QK,
    Q_LORA_RANK + KV_LORA_RANK + QK_ROPE_DIM + 3 * REC_QK,
    Q_LORA_RANK + KV_LORA_RANK + QK_ROPE_DIM + 3 * REC_QK + REC_WIDTH,
)
N_EXPERTS = 64
TOP_K = 8
N_GROUPS = 8
TOPK_GROUPS = 4
EXPERT_DIM = 512
SHARED_DIM = 512
ROUTED_SCALE = 2.5
MOE_BLOCK = 256
EPS = 1e-6

kernel_name = "hymba_mla_hgrn2_moe_adaln_encoder"


def rms_norm(x, w):
    xf = x.astype(jnp.float32)
    y = xf * lax.rsqrt(jnp.mean(xf * xf, axis=-1, keepdims=True) + EPS)
    return (y * w.astype(jnp.float32)).astype(x.dtype)


def swiglu(x, w_gate, w_up, w_down):
    return (jax.nn.silu(x @ w_gate) * (x @ w_up)) @ w_down


def rope_tables(positions, dtype):
    inv_freq = ROPE_THETA ** (-jnp.arange(0, QK_ROPE_DIM, 2, dtype=jnp.float32) / QK_ROPE_DIM)
    ang = positions.astype(jnp.float32)[..., None] * inv_freq
    return jnp.cos(ang)[:, :, None, :].astype(dtype), jnp.sin(ang)[:, :, None, :].astype(dtype)


def apply_rope(t, cos, sin):
    t1, t2 = jnp.split(t, 2, axis=-1)
    return jnp.concatenate([t1 * cos - t2 * sin, t1 * sin + t2 * cos], axis=-1)


def mla_attention(q_lat, kv_lat, k_rope, cos, sin, q_lat_norm_w, w_uq, kv_lat_norm_w, w_ukv,
                  q_norm_w, k_norm_w):
    B, S, _ = q_lat.shape
    H = N_ATT_HEADS
    q = (rms_norm(q_lat, q_lat_norm_w) @ w_uq).reshape(B, S, H, QK_HEAD_DIM)
    kv = (rms_norm(kv_lat, kv_lat_norm_w) @ w_ukv).reshape(B, S, H, QK_NOPE_DIM + V_HEAD_DIM)
    k_nope, v = kv[..., :QK_NOPE_DIM], kv[..., QK_NOPE_DIM:]
    k = jnp.concatenate([k_nope, jnp.broadcast_to(k_rope[:, :, None, :], (B, S, H, QK_ROPE_DIM))], -1)
    q = rms_norm(q, q_norm_w)
    k = rms_norm(k, k_norm_w)

    def rope_part(t):
        return jnp.concatenate([t[..., :QK_NOPE_DIM], apply_rope(t[..., QK_NOPE_DIM:], cos, sin)], -1)

    q = rope_part(q).transpose(0, 2, 1, 3)
    k = rope_part(k).transpose(0, 2, 1, 3)
    v = v.transpose(0, 2, 1, 3)
    nq = S // ATT_Q_BLOCK
    q_blocks = q.reshape(B, H, nq, ATT_Q_BLOCK, QK_HEAD_DIM).transpose(2, 0, 1, 3, 4)
    scale = QK_HEAD_DIM ** -0.5

    def attend(qb):
        s = jnp.einsum('bhqd,bhkd->bhqk', qb, k).astype(jnp.float32) * scale
        p = jax.nn.softmax(s, axis=-1)
        return jnp.einsum('bhqk,bhkd->bhqd', p.astype(v.dtype), v)

    o = lax.map(attend, q_blocks)
    return o.transpose(1, 0, 3, 2, 4).reshape(B, S, H * V_HEAD_DIM)


def hgrn2_chunk_scan(q, k, v, log_f):
    B, H, S, K = q.shape
    V = v.shape[-1]
    C = REC_CHUNK
    n = S // C

    def to_chunks(a):
        return a.reshape(B, H, n, C, a.shape[-1]).transpose(2, 0, 1, 3, 4)

    qc, kc, vc, gc = to_chunks(q), to_chunks(k), to_chunks(v), to_chunks(log_f)
    mask = jnp.tril(jnp.ones((C, C), dtype=bool))

    def step(state, inp):
        q_, k_, v_, g_ = inp
        q_ = q_.astype(jnp.float32)
        k_ = k_.astype(jnp.float32)
        v_ = v_.astype(jnp.float32)
        b = jnp.cumsum(g_.astype(jnp.float32), axis=2)
        diff = b[:, :, :, None, :] - b[:, :, None, :, :]
        decay = jnp.exp(jnp.where(mask[:, :, None], diff, -jnp.inf))
        a = jnp.einsum('bhtsk,bhsk->bhts', q_[:, :, :, None, :] * decay, k_)
        o_intra = jnp.einsum('bhts,bhsv->bhtv', a, v_)
        o_inter = jnp.einsum('bhtk,bhkv->bhtv', q_ * jnp.exp(b), state)
        b_last = b[:, :, -1:, :]
        k_dec = k_ * jnp.exp(b_last - b)
        new_state = state * jnp.exp(b_last)[:, :, 0, :, None] + jnp.einsum('bhsk,bhsv->bhkv', k_dec, v_)
        return new_state, o_intra + o_inter

    state0 = jnp.zeros((B, H, K, V), jnp.float32)
    _, o = lax.scan(step, state0, (qc, kc, vc, gc))
    return o.transpose(1, 2, 0, 3, 4).reshape(B, H, S, V)


def hgrn2_mixer(r_q, r_ff, r_fb, r_i, r_g, lb_f, lb_b, rec_norm_w):
    B, S, _ = r_q.shape
    H = N_REC_HEADS

    def heads(a, d):
        return a.reshape(B, S, H, d).transpose(0, 2, 1, 3)

    q = heads(r_q, REC_KEY_DIM)
    v = heads(r_i, REC_VAL_DIM)

    def gate(z, lb):
        f = lb + (1.0 - lb) * jax.nn.sigmoid(z.astype(jnp.float32))
        return heads(jnp.log(f), REC_KEY_DIM), heads(1.0 - f, REC_KEY_DIM)

    logf_f, k_f = gate(r_ff, lb_f)
    logf_b, k_b = gate(r_fb, lb_b)
    o_f = hgrn2_chunk_scan(q, k_f, v, logf_f)

    def flip(a):
        return jnp.flip(a, axis=2)

    o_b = flip(hgrn2_chunk_scan(flip(q), flip(k_b), flip(v), flip(logf_b)))
    o = (o_f + o_b).transpose(0, 2, 1, 3)
    o = rms_norm(o, rec_norm_w).astype(r_i.dtype) * jax.nn.silu(r_g.reshape(B, S, H, REC_VAL_DIM))
    return o.reshape(B, S, H * REC_VAL_DIM)


def moe_ffn(h, w_router, router_bias, w_gate, w_up, w_down, ws_gate, ws_up, ws_down):
    B, S, D = h.shape
    T = B * S
    E = N_EXPERTS
    xt = h.reshape(T, D)
    scores = jax.nn.sigmoid((xt @ w_router).astype(jnp.float32))
    sel = scores + router_bias.astype(jnp.float32)
    grp_score = lax.top_k(sel.reshape(T, N_GROUPS, E // N_GROUPS), 2)[0].sum(-1)
    _, grp_idx = lax.top_k(grp_score, TOPK_GROUPS)
    grp_mask = jnp.any(grp_idx[:, :, None] == jnp.arange(N_GROUPS)[None, None, :], axis=1)
    sel = jnp.where(jnp.repeat(grp_mask, E // N_GROUPS, axis=-1), sel, -jnp.inf)
    _, top_idx = lax.top_k(sel, TOP_K)
    top_w = jnp.take_along_axis(scores, top_idx, axis=-1)
    top_w = top_w / jnp.sum(top_w, axis=-1, keepdims=True) * ROUTED_SCALE

    A = T * TOP_K
    flat_e = top_idx.reshape(A).astype(jnp.int32)
    flat_tok = jnp.repeat(jnp.arange(T, dtype=jnp.int32), TOP_K)
    flat_w = top_w.reshape(A)
    order = jnp.argsort(flat_e)
    se, st, sw = flat_e[order], flat_tok[order], flat_w[order]
    counts = jnp.bincount(flat_e, length=E).astype(jnp.int32)
    padded = (counts + MOE_BLOCK - 1) // MOE_BLOCK * MOE_BLOCK
    pad_end = jnp.cumsum(padded)
    pad_start = pad_end - padded
    grp_start = jnp.cumsum(counts) - counts
    dest = pad_start[se] + jnp.arange(A, dtype=jnp.int32) - grp_start[se]
    n_blocks = -(-A // MOE_BLOCK) + E
    P = n_blocks * MOE_BLOCK
    buf_tok = jnp.full((P,), T, jnp.int32).at[dest].set(st)
    buf_w = jnp.zeros((P,), jnp.float32).at[dest].set(sw)
    blk_expert = jnp.minimum(
        jnp.searchsorted(pad_end, jnp.arange(n_blocks, dtype=jnp.int32) * MOE_BLOCK, side='right'),
        E - 1).astype(jnp.int32)
    xt_pad = jnp.concatenate([xt, jnp.zeros((1, D), xt.dtype)], axis=0)

    def body(acc, inp):
        tok, wgt, e = inp
        xb = xt_pad[tok]
        yb = swiglu(xb, w_gate[e], w_up[e], w_down[e]).astype(jnp.float32) * wgt[:, None]
        return acc.at[tok].add(yb), None

    acc0 = jnp.zeros((T + 1, D), jnp.float32)
    routed, _ = lax.scan(body, acc0, (buf_tok.reshape(n_blocks, MOE_BLOCK),
                                      buf_w.reshape(n_blocks, MOE_BLOCK), blk_expert))
    y = routed[:T].astype(xt.dtype) + swiglu(xt, ws_gate, ws_up, ws_down)
    return y.reshape(B, S, D)


def setup_inputs(seed: int = 0) -> dict:
    key = jax.random.key(seed)
    ks = iter(jax.random.split(key, 32))
    L, D, H = DEPTH, D_MODEL, N_ATT_HEADS

    def nrm(shape, fan_in, scale=1.0):
        return jax.random.normal(next(ks), shape, jnp.float32) * (scale * fan_in ** -0.5)

    def gain(shape):
        return 1.0 + 0.02 * jax.random.normal(next(ks), shape, jnp.float32)

    x = jax.random.normal(next(ks), (BATCH, SEQ, D), jnp.float32)
    c = jax.random.normal(next(ks), (BATCH, D), jnp.float32)
    positions = (jnp.arange(SEQ, dtype=jnp.int32)[None, :]
                 + jax.random.randint(next(ks), (BATCH, 1), 0, 512, dtype=jnp.int32))
    return {
        "x": x,
        "c": c,
        "positions": positions,
        "w_ada": nrm((L, D, 6 * D), D, 0.5),
        "b_ada": 0.02 * jax.random.normal(next(ks), (L, 6 * D), jnp.float32),
        "norm1_w": gain((L, D)),
        "w_in": nrm((L, D, IN_COLS), D),
        "q_lat_norm_w": gain((L, Q_LORA_RANK)),
        "w_uq": nrm((L, Q_LORA_RANK, H * QK_HEAD_DIM), Q_LORA_RANK),
        "kv_lat_norm_w": gain((L, KV_LORA_RANK)),
        "w_ukv": nrm((L, KV_LORA_RANK, H * (QK_NOPE_DIM + V_HEAD_DIM)), KV_LORA_RANK),
        "q_norm_w": gain((L, QK_HEAD_DIM)),
        "k_norm_w": gain((L, QK_HEAD_DIM)),
        "lb_fwd": 0.5 * jax.random.normal(next(ks), (L + 1, REC_QK), jnp.float32),
        "lb_bwd": 0.5 * jax.random.normal(next(ks), (L + 1, REC_QK), jnp.float32),
        "rec_norm_w": gain((L, REC_VAL_DIM)),
        "w_out": nrm((L, MIX_WIDTH, D), MIX_WIDTH),
        "norm2_w": gain((L, D)),
        "w_router": nrm((L, D, N_EXPERTS), D),
        "router_bias": 0.01 * jax.random.normal(next(ks), (L, N_EXPERTS), jnp.float32),
        "w_exp_gate": nrm((L, N_EXPERTS, D, EXPERT_DIM), D),
        "w_exp_up": nrm((L, N_EXPERTS, D, EXPERT_DIM), D),
        "w_exp_down": nrm((L, N_EXPERTS, EXPERT_DIM, D), EXPERT_DIM),
        "w_sh_gate": nrm((L, D, SHARED_DIM), D),
        "w_sh_up": nrm((L, D, SHARED_DIM), D),
        "w_sh_down": nrm((L, SHARED_DIM, D), SHARED_DIM),
    }


def reference(x, c, positions, w_ada, b_ada, norm1_w, w_in, q_lat_norm_w, w_uq, kv_lat_norm_w,
              w_ukv, q_norm_w, k_norm_w, lb_fwd, lb_bwd, rec_norm_w, w_out, norm2_w, w_router,
              router_bias, w_exp_gate, w_exp_up, w_exp_down, w_sh_gate, w_sh_up, w_sh_down):
    cos, sin = rope_tables(positions, x.dtype)
    lbf_all = jnp.cumsum(jax.nn.softmax(lb_fwd.astype(jnp.float32), axis=0), axis=0)
    lbb_all = jnp.cumsum(jax.nn.softmax(lb_bwd.astype(jnp.float32), axis=0), axis=0)
    cond = jax.nn.silu(c)
    for l in range(DEPTH):
        mod = (cond @ w_ada[l] + b_ada[l])[:, None, :]
        sh1, sc1, g1, sh2, sc2, g2 = jnp.split(mod, 6, axis=-1)
        h = rms_norm(x, norm1_w[l]) * (1.0 + sc1) + sh1
        proj = h @ w_in[l]
        q_lat, kv_lat, k_rope, r_q, r_ff, r_fb, r_i, r_g = jnp.split(proj, IN_SPLITS, axis=-1)
        att = mla_attention(q_lat, kv_lat, k_rope, cos, sin, q_lat_norm_w[l], w_uq[l],
                            kv_lat_norm_w[l], w_ukv[l], q_norm_w[l], k_norm_w[l])
        rec = hgrn2_mixer(r_q, r_ff, r_fb, r_i, r_g, lbf_all[l].astype(jnp.float32),
                          lbb_all[l].astype(jnp.float32), rec_norm_w[l])
        x = x + g1 * (jnp.concatenate([att, rec], axis=-1) @ w_out[l])
        h2 = rms_norm(x, norm2_w[l]) * (1.0 + sc2) + sh2
        x = x + g2 * moe_ffn(h2, w_router[l], router_bias[l], w_exp_gate[l], w_exp_up[l],
                             w_exp_down[l], w_sh_gate[l], w_sh_up[l], w_sh_down[l])
    return x
```

```python
import functools

import numpy as np
import jax
import jax.numpy as jnp
from jax import lax
from jax.experimental import pallas as pl
from jax.experimental.pallas import tpu as pltpu

F32 = jnp.float32
BF16 = jnp.bfloat16
U32 = jnp.uint32
I32 = jnp.int32

N_HEADS = 8
QK_NOPE = 128
QK_ROPE = 64
QK_DIM = QK_NOPE + QK_ROPE
QK_PAD = 256
V_DIM = 128
Q_RANK = 512
KV_RANK = 256
LAT_PAD = 896
REC_DIM = 128
REC_COLS = 5 * N_HEADS * REC_DIM
ROPE_THETA = 10000.0
N_EXPERTS = 64
TOP_K = 8
N_GROUPS = 8
TOPK_GROUPS = 4
EXPERT_DIM = 512
ROUTED_SCALE = 2.5
MOE_BLOCK = 256
EPS = 1e-6

LANES = 128
SUBLANES = 8
ROW_SLAB = 8
VMEM_LIMIT = 56 * 1024 * 1024

HG_TILE = 128
HG_BAND = 16
HG_LEVELS = (16, 32, 64)


def _cparams(sem, vmem=VMEM_LIMIT):
    return pltpu.CompilerParams(dimension_semantics=sem, vmem_limit_bytes=vmem)


def _split_bf16(a):
    hi = a.astype(BF16)
    lo = (a - hi.astype(F32)).astype(BF16)
    return hi, lo


def _pack_pair(a, b):
    ua = lax.bitcast_convert_type(a.astype(BF16).astype(F32), U32)
    ub = lax.bitcast_convert_type(b.astype(BF16).astype(F32), U32)
    return ua | (ub >> 16)


def _unpack_pair(p):
    a = lax.bitcast_convert_type(p & jnp.uint32(0xFFFF0000), F32).astype(BF16)
    b = lax.bitcast_convert_type(p << 16, F32).astype(BF16)
    return a, b


def _store_slabs(ref, packed):
    n = packed.shape[0]
    for j in range(ROW_SLAB):
        ref[pl.ds(j, n, stride=ROW_SLAB), :] = packed[:, j * LANES:(j + 1) * LANES]


def _load_slab_chunk(ref, j, n):
    return ref[pl.ds(j, n, stride=ROW_SLAB), :]


def _dot(a, b):
    return jnp.dot(a, b, preferred_element_type=F32)


def _dot_nt(a, b):
    return lax.dot_general(a, b, (((1,), (1,)), ((), ())), preferred_element_type=F32)


def _dot_tn(a, b):
    return lax.dot_general(a, b, (((0,), (0,)), ((), ())), preferred_element_type=F32)


def _ada_kernel(c_ref, w_ref, b_ref, o_ref):
    c = c_ref[...]
    cond = c * jax.nn.sigmoid(c)
    o_ref[...] = _dot(cond.astype(BF16), w_ref[...].astype(BF16)) + b_ref[...]


def _ada(c, w_ada, b_ada):
    bsz, d = c.shape
    n = w_ada.shape[1]
    tn = 1024
    return pl.pallas_call(
        _ada_kernel,
        grid=(n // tn,),
        in_specs=[
            pl.BlockSpec((bsz, d), lambda j: (0, 0)),
            pl.BlockSpec((d, tn), lambda j: (0, j)),
            pl.BlockSpec((1, tn), lambda j: (0, j)),
        ],
        out_specs=pl.BlockSpec((bsz, tn), lambda j: (0, j)),
        out_shape=jax.ShapeDtypeStruct((bsz, n), F32),
        compiler_params=_cparams(("arbitrary",)),
        name="ada",
    )(c, w_ada, b_ada.reshape(1, n))


def _modulated_norm(x, mod_ref, nw, shift_row, scale_row):
    r = lax.rsqrt(jnp.mean(x * x, axis=-1, keepdims=True) + EPS)
    scale = mod_ref[0, scale_row:scale_row + 1, :]
    shift = mod_ref[0, shift_row:shift_row + 1, :]
    return (x * r * nw) * (1.0 + scale) + shift


def _inproj_kernel(x_ref, mod_ref, nw_ref, wlat_ref, wrec_ref, lat_ref, rec_ref, h_scr):
    @pl.when(pl.program_id(1) == 0)
    def _():
        h = _modulated_norm(x_ref[...], mod_ref, nw_ref[...], 0, 1).astype(BF16)
        h_scr[...] = h
        lat_ref[...] = _dot(h, wlat_ref[...])

    rec_ref[...] = _dot(h_scr[...], wrec_ref[...])


def _inproj(x2, mod3, norm_w, w_lat, w_rec, seq):
    t, d = x2.shape
    tm, tn = 512, 1024
    per_b = seq // tm
    return pl.pallas_call(
        _inproj_kernel,
        grid=(t // tm, REC_COLS // tn),
        in_specs=[
            pl.BlockSpec((tm, d), lambda i, j: (i, 0)),
            pl.BlockSpec((1, 6, d), lambda i, j: (i // per_b, 0, 0)),
            pl.BlockSpec((1, d), lambda i, j: (0, 0)),
            pl.BlockSpec((d, LAT_PAD), lambda i, j: (0, 0)),
            pl.BlockSpec((d, tn), lambda i, j: (0, j)),
        ],
        out_specs=[
            pl.BlockSpec((tm, LAT_PAD), lambda i, j: (i, 0)),
            pl.BlockSpec((tm, tn), lambda i, j: (i, j)),
        ],
        out_shape=[
            jax.ShapeDtypeStruct((t, LAT_PAD), F32),
            jax.ShapeDtypeStruct((t, REC_COLS), F32),
        ],
        scratch_shapes=[pltpu.VMEM((tm, d), BF16)],
        compiler_params=_cparams(("parallel", "arbitrary")),
        name="inproj",
    )(x2, mod3, norm_w.reshape(1, d), w_lat, w_rec)


def _rope(x2, cos, sin_a, sin_b):
    return (x2 * cos + pltpu.roll(x2, LANES - QK_ROPE // 2, 1) * sin_a
            + pltpu.roll(x2, QK_ROPE // 2, 1) * sin_b)


def _mla_prep_kernel(lat_ref, cos_ref, sa_ref, sb_ref, qlw_ref, wuq_ref, kvlw_ref, wukv_ref,
                     qnw_ref, knw_ref, q_ref, k_ref, v_ref):
    lat = lat_ref[...]
    ql = lat[:, :Q_RANK]
    kvl = lat[:, Q_RANK:Q_RANK + KV_RANK]
    kr = lat[:, Q_RANK + KV_RANK:]
    qn = (ql * lax.rsqrt(jnp.mean(ql * ql, -1, keepdims=True) + EPS) * qlw_ref[...]).astype(BF16)
    kvn = (kvl * lax.rsqrt(jnp.mean(kvl * kvl, -1, keepdims=True) + EPS) * kvlw_ref[...]).astype(BF16)
    cos, sa, sb = cos_ref[...], sa_ref[...], sb_ref[...]
    qnw = qnw_ref[...]
    knw = knw_ref[...]
    kr_ss = jnp.sum(kr * kr, -1, keepdims=True)
    kr_roped = _rope(kr * knw[:, QK_NOPE:], cos, sa, sb)
    scale = QK_DIM ** -0.5
    for h in range(N_HEADS):
        qh = _dot(qn, wuq_ref[h])
        r = lax.rsqrt(jnp.sum(qh * qh, -1, keepdims=True) * (1.0 / QK_DIM) + EPS) * scale
        qh = qh * r * qnw
        q_ref[0, h, :, :QK_NOPE] = qh[:, :QK_NOPE].astype(BF16)
        q_ref[0, h, :, QK_NOPE:] = _rope(qh[:, QK_NOPE:], cos, sa, sb).astype(BF16)
        kvh = _dot(kvn, wukv_ref[h])
        kn = kvh[:, :QK_NOPE]
        rk = lax.rsqrt((jnp.sum(kn * kn, -1, keepdims=True) + kr_ss) * (1.0 / QK_DIM) + EPS)
        k_ref[0, h, :, :QK_NOPE] = (kn * rk * knw[:, :QK_NOPE]).astype(BF16)
        k_ref[0, h, :, QK_NOPE:] = (kr_roped * rk).astype(BF16)
        v_ref[0, h] = kvh[:, QK_NOPE:].astype(BF16)


def _mla_prep(lat, cos, sa, sb, qlw, wuq, kvlw, wukv, qnw, knw, bsz, seq):
    tm = 512
    per_b = seq // tm
    const2 = lambda i: (0, 0)
    const3 = lambda i: (0, 0, 0)
    head_out = lambda w: pl.BlockSpec((1, N_HEADS, tm, w), lambda i: (i // per_b, 0, i % per_b, 0))
    return pl.pallas_call(
        _mla_prep_kernel,
        grid=(bsz * seq // tm,),
        in_specs=[
            pl.BlockSpec((tm, LAT_PAD), lambda i: (i, 0)),
            pl.BlockSpec((tm, LANES), lambda i: (i, 0)),
            pl.BlockSpec((tm, LANES), lambda i: (i, 0)),
            pl.BlockSpec((tm, LANES), lambda i: (i, 0)),
            pl.BlockSpec((1, Q_RANK), const2),
            pl.BlockSpec((N_HEADS, Q_RANK, QK_PAD), const3),
            pl.BlockSpec((1, KV_RANK), const2),
            pl.BlockSpec((N_HEADS, KV_RANK, QK_PAD), const3),
            pl.BlockSpec((1, QK_PAD), const2),
            pl.BlockSpec((1, QK_PAD), const2),
        ],
        out_specs=[head_out(QK_PAD), head_out(QK_PAD), head_out(V_DIM)],
        out_shape=[
            jax.ShapeDtypeStruct((bsz, N_HEADS, seq, QK_PAD), BF16),
            jax.ShapeDtypeStruct((bsz, N_HEADS, seq, QK_PAD), BF16),
            jax.ShapeDtypeStruct((bsz, N_HEADS, seq, V_DIM), BF16),
        ],
        compiler_params=_cparams(("parallel",)),
        name="mla_prep",
    )(lat, cos, sa, sb, qlw, wuq, kvlw, wukv, qnw, knw)


def _attn_kernel(q_ref, k_ref, v_ref, o_ref):
    s = _dot_nt(q_ref[0, 0], k_ref[0, 0])
    m = jnp.max(s, axis=-1, keepdims=True)
    p = jnp.exp(s - m)
    l = jnp.sum(p, axis=-1, keepdims=True)
    o_ref[...] = _dot(p.astype(BF16), v_ref[0, 0]) / l


def _attention(q, k, v):
    bsz, nh, seq, _ = q.shape
    tq = 512
    nq = seq // tq
    return pl.pallas_call(
        _attn_kernel,
        grid=(bsz, nh, nq),
        in_specs=[
            pl.BlockSpec((1, 1, tq, QK_PAD), lambda b, h, i: (b, h, i, 0)),
            pl.BlockSpec((1, 1, seq, QK_PAD), lambda b, h, i: (b, h, 0, 0)),
            pl.BlockSpec((1, 1, seq, V_DIM), lambda b, h, i: (b, h, 0, 0)),
        ],
        out_specs=pl.BlockSpec((tq, V_DIM), lambda b, h, i: (b * nq + i, h)),
        out_shape=jax.ShapeDtypeStruct((bsz * seq, nh * V_DIM), F32),
        compiler_params=_cparams(("parallel", "parallel", "arbitrary")),
        name="attention",
    )(q, k, v)


def _hgrn_constants(rev):
    n = HG_TILE
    t = np.arange(n)[:, None]
    u = np.arange(n)[None, :]
    mats = []
    level_id = np.zeros((n, n), np.float32)
    for li, m in enumerate(HG_LEVELS):
        blk_t, blk_u = t // m, u // m
        same = blk_t == blk_u
        odd = (blk_t % 2) == 1
        if not rev:
            mat = np.where(odd, same & (u <= t), same & (u > t))
            lvl = odd & ((u // m) == blk_t - 1)
        else:
            mat = np.where(~odd, same & (u >= t), same & (u < t))
            lvl = (~odd) & ((u // m) == blk_t + 1)
        mats.append(mat.astype(np.float32))
        level_id[lvl] = li + 1
    if not rev:
        mats.append((u <= t).astype(np.float32))
        mats.append((u > t).astype(np.float32))
    else:
        mats.append((u >= t).astype(np.float32))
        mats.append((u < t).astype(np.float32))
    return np.concatenate(mats, axis=0), level_id


def _hgrn_direction(rev, q_ref, f_scr, kk_scr, v_scr, g_scr, o_scr, cum_ref, lvl_ref, first):
    seq = q_ref.shape[0]
    n_tiles = seq // HG_TILE
    pad = HG_BAND
    sgn = 1 if rev else -1
    row = lax.broadcasted_iota(I32, (HG_TILE, 1), 0)
    ones_b = jnp.ones((HG_TILE, LANES), BF16)
    nlev = len(HG_LEVELS)

    def tile_body(n, state):
        tile = (n_tiles - 1 - n) if rev else n
        t0 = pl.multiple_of(tile * HG_TILE, HG_TILE)
        q = q_ref[pl.ds(t0, HG_TILE), :]
        g = g_scr[pl.ds(t0, HG_TILE), :]
        kk = kk_scr[pl.ds(pad + t0, HG_TILE), :]
        v = v_scr[pl.ds(pad + t0, HG_TILE), :]
        v_b = v.astype(BF16)
        g_hi, g_lo = _split_bf16(g)
        cum = _dot(cum_ref[...], g_hi) + _dot(cum_ref[...], g_lo)
        w = jnp.exp(cum)

        lvl = lvl_ref[...]
        a = jnp.zeros((HG_TILE, HG_TILE), F32)
        for li, m in enumerate(HG_LEVELS):
            odd = ((row // m) % 2) == 1
            is_query = jnp.logical_not(odd) if rev else odd
            x = (jnp.where(is_query, q, kk) * w[li * HG_TILE:(li + 1) * HG_TILE]).astype(BF16)
            a = jnp.where(lvl == float(li + 1), _dot_nt(x, x), a)
        o = _dot(a.astype(BF16), v_b)

        wq = w[nlev * HG_TILE:(nlev + 1) * HG_TILE]
        wk = w[(nlev + 1) * HG_TILE:(nlev + 2) * HG_TILE]
        o = o + _dot_nt((q * wq).astype(BF16), state.astype(BF16))
        total = wq[0:1, :] if rev else wq[HG_TILE - 1:HG_TILE, :]
        new_state = state * total + _dot_tn(v_b, (kk * wk).astype(BF16))

        fprod = None
        for d in range(HG_BAND):
            if d == 1:
                fprod = f_scr[pl.ds(pad + t0, HG_TILE), :]
            elif d > 1:
                fprod = fprod * f_scr[pl.ds(pad + t0 + sgn * (d - 1), HG_TILE), :]
            p = q * kk_scr[pl.ds(pad + t0 + sgn * d, HG_TILE), :]
            if fprod is not None:
                p = p * fprod
            rs = _dot(p.astype(BF16), ones_b)
            o = o + rs * v_scr[pl.ds(pad + t0 + sgn * d, HG_TILE), :]

        if first:
            o_scr[pl.ds(t0, HG_TILE), :] = o
        else:
            o_scr[pl.ds(t0, HG_TILE), :] = o_scr[pl.ds(t0, HG_TILE), :] + o
        return new_state

    lax.fori_loop(0, n_tiles, tile_body, jnp.zeros((V_DIM, REC_DIM), F32))


def _hgrn_kernel(q_ref, zf_ref, zb_ref, v_ref, gate_ref, lbf_ref, lbb_ref, nw_ref,
                 cumf_ref, lvlf_ref, cumb_ref, lvlb_ref, o_ref,
                 f_scr, kk_scr, v_scr, g_scr, o_scr):
    seq = q_ref.shape[0]
    pad = HG_BAND
    zeros_pad = jnp.zeros((pad, LANES), F32)
    for scr in (f_scr, kk_scr, v_scr):
        scr[pl.ds(0, pad), :] = zeros_pad
        scr[pl.ds(pad + seq, pad), :] = zeros_pad
    v_scr[pl.ds(pad, seq), :] = v_ref[...]
    pos = lax.broadcasted_iota(I32, (seq, 1), 0) % HG_BAND

    for rev, z_ref, lb_ref, cum_ref, lvl_ref in ((False, zf_ref, lbf_ref, cumf_ref, lvlf_ref),
                                                 (True, zb_ref, lbb_ref, cumb_ref, lvlb_ref)):
        lb = lb_ref[...]
        f = lb + (1.0 - lb) * jax.nn.sigmoid(z_ref[...])
        g_scr[...] = jnp.log(f)
        kk_scr[pl.ds(pad, seq), :] = 1.0 - f
        boundary = (pos == HG_BAND - 1) if rev else (pos == 0)
        f_scr[pl.ds(pad, seq), :] = jnp.where(boundary, 0.0, f)
        _hgrn_direction(rev, q_ref, f_scr, kk_scr, v_scr, g_scr, o_scr, cum_ref, lvl_ref,
                        first=not rev)

    o = o_scr[...]
    o = o * lax.rsqrt(jnp.mean(o * o, -1, keepdims=True) + EPS) * nw_ref[...]
    gate = gate_ref[...]
    o_ref[...] = o * (gate * jax.nn.sigmoid(gate))


def _hgrn(rec, lbf, lbb, rec_norm_w, bsz, seq):
    cumf, lvlf = _hgrn_constants(False)
    cumb, lvlb = _hgrn_constants(True)
    ncum = cumf.shape[0]
    col = lambda c: pl.BlockSpec((seq, REC_DIM), lambda b, h: (b, c * N_HEADS + h))
    vec = pl.BlockSpec((1, REC_DIM), lambda b, h: (0, h))
    const = lambda r: pl.BlockSpec((r, HG_TILE), lambda b, h: (0, 0))
    padded = pltpu.VMEM((seq + 2 * HG_BAND, LANES), F32)
    return pl.pallas_call(
        _hgrn_kernel,
        grid=(bsz, N_HEADS),
        in_specs=[col(0), col(1), col(2), col(3), col(4), vec, vec,
                  pl.BlockSpec((1, REC_DIM), lambda b, h: (0, 0)),
                  const(ncum), const(HG_TILE), const(ncum), const(HG_TILE)],
        out_specs=pl.BlockSpec((seq, REC_DIM), lambda b, h: (b, h)),
        out_shape=jax.ShapeDtypeStruct((bsz * seq, N_HEADS * REC_DIM), F32),
        scratch_shapes=[padded, padded, padded,
                        pltpu.VMEM((seq, LANES), F32), pltpu.VMEM((seq, LANES), F32)],
        compiler_params=_cparams(("parallel", "parallel")),
        name="hgrn",
    )(rec, rec, rec, rec, rec, lbf, lbb, rec_norm_w.reshape(1, REC_DIM),
      jnp.asarray(cumf, BF16), jnp.asarray(lvlf, F32), jnp.asarray(cumb, BF16), jnp.asarray(lvlb, F32))


def _outproj_kernel(x_ref, att_ref, rec_ref, mod_ref, wa_ref, wr_ref, o_ref):
    y = _dot(att_ref[...].astype(BF16), wa_ref[...]) + _dot(rec_ref[...].astype(BF16), wr_ref[...])
    o_ref[...] = x_ref[...] + mod_ref[0, 2:3, :] * y


def _outproj(x2, att, rec, mod3, w_att, w_rec, seq):
    t, d = x2.shape
    tm = 512
    per_b = seq // tm
    half = att.shape[1]
    return pl.pallas_call(
        _outproj_kernel,
        grid=(t // tm,),
        in_specs=[
            pl.BlockSpec((tm, d), lambda i: (i, 0)),
            pl.BlockSpec((tm, half), lambda i: (i, 0)),
            pl.BlockSpec((tm, half), lambda i: (i, 0)),
            pl.BlockSpec((1, 6, d), lambda i: (i // per_b, 0, 0)),
            pl.BlockSpec((half, d), lambda i: (0, 0)),
            pl.BlockSpec((half, d), lambda i: (0, 0)),
        ],
        out_specs=pl.BlockSpec((tm, d), lambda i: (i, 0)),
        out_shape=jax.ShapeDtypeStruct((t, d), F32),
        compiler_params=_cparams(("parallel",)),
        name="outproj",
    )(x2, att, rec, mod3, w_att, w_rec)


def _router_kernel(x_ref, mod_ref, nw_ref, wrh_ref, wrl_ref, bias_ref, tri_ref,
                   h2b_ref, h2p_ref, idx_ref, wgt_ref, rank_ref, cnt_ref, base_scr):
    i = pl.program_id(0)
    tm = x_ref.shape[0]
    half = x_ref.shape[1] // 2

    @pl.when(i == 0)
    def _():
        base_scr[...] = jnp.zeros_like(base_scr)

    h2 = _modulated_norm(x_ref[...], mod_ref, nw_ref[...], 3, 4)
    h_hi, h_lo = _split_bf16(h2)
    h2b_ref[...] = h_hi
    _store_slabs(h2p_ref, _pack_pair(h2[:, :half], h2[:, half:]))
    logits = _dot_nt(wrh_ref[...], h_hi) + _dot_nt(wrh_ref[...], h_lo) + _dot_nt(wrl_ref[...], h_hi)
    scores = jax.nn.sigmoid(logits)
    sel = scores + bias_ref[...]

    neg = -jnp.inf
    per_grp = N_EXPERTS // N_GROUPS
    sel3 = sel.reshape(N_GROUPS, per_grp, tm)
    within = lax.broadcasted_iota(I32, sel3.shape, 1)
    m1 = jnp.max(sel3, axis=1, keepdims=True)
    first = jnp.min(jnp.where(sel3 == m1, within, per_grp), axis=1, keepdims=True)
    m2 = jnp.max(jnp.where(within == first, neg, sel3), axis=1, keepdims=True)
    gs = (m1 + m2).reshape(N_GROUPS, tm)
    gidx = lax.broadcasted_iota(I32, gs.shape, 0)
    beaten = jnp.zeros(gs.shape, I32)
    for g2 in range(N_GROUPS):
        other = gs[g2:g2 + 1, :]
        beaten = beaten + ((other > gs) | ((other == gs) & (g2 < gidx))).astype(I32)
    keep = (beaten < TOPK_GROUPS).reshape(N_GROUPS, 1, tm)
    cur = jnp.where(keep, sel3, neg).reshape(N_EXPERTS, tm)

    eidx = lax.broadcasted_iota(I32, (N_EXPERTS, tm), 0)
    chosen = jnp.zeros((N_EXPERTS, tm), F32)
    picks, pick_scores = [], []
    for _ in range(TOP_K):
        m = jnp.max(cur, axis=0, keepdims=True)
        pick = jnp.min(jnp.where(cur == m, eidx, N_EXPERTS), axis=0, keepdims=True)
        hit = eidx == pick
        picks.append(pick)
        pick_scores.append(jnp.sum(jnp.where(hit, scores, 0.0), axis=0, keepdims=True))
        chosen = jnp.where(hit, 1.0, chosen)
        cur = jnp.where(hit, neg, cur)

    before = _dot(chosen.astype(BF16), tri_ref[...]) + base_scr[...][:, 0:1]
    denom = pick_scores[0]
    for s in pick_scores[1:]:
        denom = denom + s
    for j in range(TOP_K):
        idx_ref[j:j + 1, :] = picks[j]
        wgt_ref[j:j + 1, :] = pick_scores[j] / denom * ROUTED_SCALE
        rank_ref[j:j + 1, :] = jnp.sum(jnp.where(eidx == picks[j], before, 0.0), axis=0,
                                       keepdims=True).astype(I32)
    base_scr[...] = base_scr[...] + jnp.sum(chosen, axis=1, keepdims=True)
    cnt_ref[...] = base_scr[...]


def _router(x1, mod3, norm_w, wr_hi, wr_lo, bias, seq):
    t, d = x1.shape
    tm = 512
    per_b = seq // tm
    tri = jnp.asarray(np.triu(np.ones((tm, tm), np.float32), 1), BF16)
    row_out = lambda dt: jax.ShapeDtypeStruct((TOP_K, t), dt)
    row_spec = pl.BlockSpec((TOP_K, tm), lambda i: (0, i))
    return pl.pallas_call(
        _router_kernel,
        grid=(t // tm,),
        in_specs=[
            pl.BlockSpec((tm, d), lambda i: (i, 0)),
            pl.BlockSpec((1, 6, d), lambda i: (i // per_b, 0, 0)),
            pl.BlockSpec((1, d), lambda i: (0, 0)),
            pl.BlockSpec((N_EXPERTS, d), lambda i: (0, 0)),
            pl.BlockSpec((N_EXPERTS, d), lambda i: (0, 0)),
            pl.BlockSpec((N_EXPERTS, 1), lambda i: (0, 0)),
            pl.BlockSpec((tm, tm), lambda i: (0, 0)),
        ],
        out_specs=[
            pl.BlockSpec((tm, d), lambda i: (i, 0)),
            pl.BlockSpec((tm * ROW_SLAB, LANES), lambda i: (i, 0)),
            row_spec, row_spec, row_spec,
            pl.BlockSpec((N_EXPERTS, LANES), lambda i: (0, 0)),
        ],
        out_shape=[
            jax.ShapeDtypeStruct((t, d), BF16),
            jax.ShapeDtypeStruct((t * ROW_SLAB, LANES), U32),
            row_out(I32), row_out(F32), row_out(I32),
            jax.ShapeDtypeStruct((N_EXPERTS, LANES), F32),
        ],
        scratch_shapes=[pltpu.VMEM((N_EXPERTS, LANES), F32)],
        compiler_params=_cparams(("arbitrary",)),
        name="router",
    )(x1, mod3, norm_w.reshape(1, d), wr_hi, wr_lo, bias.reshape(N_EXPERTS, 1), tri)


def _dispatch_kernel(dest_ref, pad_end_ref, h2p_ref, xs_ref, zero_scr, sem, *, tc):
    i = pl.program_id(0)

    @pl.when(i == 0)
    def _():
        zero_scr[...] = jnp.zeros_like(zero_scr)

        def fill(e, prev_end):
            end = pad_end_ref[e]

            @pl.when(end > prev_end)
            def _():
                start = pl.multiple_of((end - MOE_BLOCK) * ROW_SLAB, MOE_BLOCK * ROW_SLAB)
                cp = pltpu.make_async_copy(zero_scr, xs_ref.at[pl.ds(start, MOE_BLOCK * ROW_SLAB)], sem)
                cp.start()
                cp.wait()
            return end

        lax.fori_loop(0, N_EXPERTS, fill, jnp.int32(0))

        def fill_unused(blk, carry):
            start = pl.multiple_of(blk * (MOE_BLOCK * ROW_SLAB), MOE_BLOCK * ROW_SLAB)
            cp = pltpu.make_async_copy(zero_scr, xs_ref.at[pl.ds(start, MOE_BLOCK * ROW_SLAB)], sem)
            cp.start()
            cp.wait()
            return carry

        n_blocks = xs_ref.shape[0] // (MOE_BLOCK * ROW_SLAB)
        lax.fori_loop(pad_end_ref[N_EXPERTS - 1] // MOE_BLOCK, n_blocks, fill_unused, 0)

    def issue(t, carry):
        src = h2p_ref.at[pl.ds(pl.multiple_of((i * tc + t) * ROW_SLAB, ROW_SLAB), ROW_SLAB)]
        for j in range(TOP_K):
            d = pl.multiple_of(dest_ref[0, 0, j * tc + t] * ROW_SLAB, ROW_SLAB)
            pltpu.make_async_copy(src, xs_ref.at[pl.ds(d, ROW_SLAB)], sem).start()
        return carry

    lax.fori_loop(0, tc, issue, 0)
    for _ in range(TOP_K):
        pltpu.make_async_copy(h2p_ref.at[pl.ds(0, tc * ROW_SLAB)], xs_ref.at[pl.ds(0, tc * ROW_SLAB)],
                              sem).wait()


def _dispatch(dest_blocks, pad_end, h2p, n_rows, tc):
    t, w = h2p.shape[0] // ROW_SLAB, LANES
    n_rows = n_rows * ROW_SLAB
    return pl.pallas_call(
        functools.partial(_dispatch_kernel, tc=tc),
        grid=(t // tc,),
        in_specs=[
            pl.BlockSpec((1, 1, TOP_K * tc), lambda i: (i, 0, 0), memory_space=pltpu.SMEM),
            pl.BlockSpec(memory_space=pltpu.SMEM),
            pl.BlockSpec(memory_space=pl.ANY),
        ],
        out_specs=pl.BlockSpec(memory_space=pl.ANY),
        out_shape=jax.ShapeDtypeStruct((n_rows, w), U32),
        scratch_shapes=[pltpu.VMEM((MOE_BLOCK * ROW_SLAB, w), U32), pltpu.SemaphoreType.DMA(())],
        compiler_params=_cparams(("arbitrary",)),
        name="dispatch",
    )(dest_blocks, pad_end, h2p)


def _expert_kernel(be_ref, nused_ref, xs_ref, wg_ref, wu_ref, wd_ref, ys_ref,
                   wg_scr, wu_scr, wd_scr, x_scr):
    b = pl.program_id(0)
    e = be_ref[b]
    prev = be_ref[jnp.maximum(b - 1, 0)]
    half = ROW_SLAB * LANES

    @pl.when((b == 0) | (e != prev))
    def _():
        wg_scr[...] = wg_ref[...].astype(BF16)
        wu_scr[...] = wu_ref[...].astype(BF16)
        wd_scr[...] = wd_ref[...].astype(BF16)

    @pl.when(b < nused_ref[0])
    def _():
        for j in range(ROW_SLAB):
            xa, xb = _unpack_pair(_load_slab_chunk(xs_ref, j, MOE_BLOCK))
            x_scr[:, j * LANES:(j + 1) * LANES] = xa
            x_scr[:, half + j * LANES:half + (j + 1) * LANES] = xb
        xfull = x_scr[...]
        gate = _dot(xfull, wg_scr[...])
        up = _dot(xfull, wu_scr[...])
        mid = ((gate * jax.nn.sigmoid(gate)) * up).astype(BF16)
        y = _dot(mid, wd_scr[...])
        _store_slabs(ys_ref, _pack_pair(y[:, :half], y[:, half:]))

    @pl.when(b >= nused_ref[0])
    def _():
        ys_ref[...] = jnp.zeros_like(ys_ref)


def _experts(blk_expert, n_used, xs, w_gate, w_up, w_down):
    n_blocks = xs.shape[0] // (MOE_BLOCK * ROW_SLAB)
    d = 2 * ROW_SLAB * LANES
    last = lambda b, be, nu: jnp.minimum(b, jnp.maximum(nu[0] - 1, 0))
    return pl.pallas_call(
        _expert_kernel,
        grid_spec=pltpu.PrefetchScalarGridSpec(
            num_scalar_prefetch=2,
            grid=(n_blocks,),
            in_specs=[
                pl.BlockSpec((MOE_BLOCK * ROW_SLAB, LANES), lambda b, be, nu: (last(b, be, nu), 0)),
                pl.BlockSpec((None, d, EXPERT_DIM), lambda b, be, nu: (be[b], 0, 0)),
                pl.BlockSpec((None, d, EXPERT_DIM), lambda b, be, nu: (be[b], 0, 0)),
                pl.BlockSpec((None, EXPERT_DIM, d), lambda b, be, nu: (be[b], 0, 0)),
            ],
            out_specs=pl.BlockSpec((MOE_BLOCK * ROW_SLAB, LANES), lambda b, be, nu: (b, 0)),
            scratch_shapes=[pltpu.VMEM((d, EXPERT_DIM), BF16), pltpu.VMEM((d, EXPERT_DIM), BF16),
                            pltpu.VMEM((EXPERT_DIM, d), BF16), pltpu.VMEM((MOE_BLOCK, d), BF16)],
        ),
        out_shape=jax.ShapeDtypeStruct(xs.shape, U32),
        compiler_params=_cparams(("arbitrary",)),
        name="experts",
    )(blk_expert, n_used, xs, w_gate, w_up, w_down)


def _combine_kernel(dest_ref, x1_ref, h2b_ref, wgt_ref, mod_ref, wsg_ref, wsu_ref, wsd_ref, ys_ref,
                    o_ref, gbuf, sem, *, tm):
    def issue(t, carry):
        row = pl.multiple_of(t * ROW_SLAB, ROW_SLAB)
        for j in range(TOP_K):
            d = pl.multiple_of(dest_ref[0, 0, j * tm + t] * ROW_SLAB, ROW_SLAB)
            pltpu.make_async_copy(ys_ref.at[pl.ds(d, ROW_SLAB)], gbuf.at[j, pl.ds(row, ROW_SLAB)],
                                  sem).start()
        return carry

    lax.fori_loop(0, tm, issue, 0)

    h2 = h2b_ref[...]
    gate = _dot(h2, wsg_ref[...])
    up = _dot(h2, wsu_ref[...])
    shared = _dot(((gate * jax.nn.sigmoid(gate)) * up).astype(BF16), wsd_ref[...])

    for j in range(TOP_K):
        pltpu.make_async_copy(ys_ref.at[pl.ds(0, tm * ROW_SLAB)], gbuf.at[j], sem).wait()

    half = ROW_SLAB * LANES
    wgt = wgt_ref[...]
    g2 = mod_ref[0, 5:6, :]
    for c in range(ROW_SLAB):
        ra = jnp.zeros((tm, LANES), F32)
        rb = jnp.zeros((tm, LANES), F32)
        for j in range(TOP_K):
            ya, yb = _unpack_pair(_load_slab_chunk(gbuf.at[j], c, tm))
            wj = wgt[:, j:j + 1]
            ra = ra + wj * ya.astype(F32)
            rb = rb + wj * yb.astype(F32)
        lo = slice(c * LANES, (c + 1) * LANES)
        hi = slice(half + c * LANES, half + (c + 1) * LANES)
        o_ref[:, lo] = x1_ref[:, lo] + g2[:, lo] * (ra + shared[:, lo])
        o_ref[:, hi] = x1_ref[:, hi] + g2[:, hi] * (rb + shared[:, hi])


def _combine(dest_blocks, x1, h2b, wgt_t, mod3, ws_gate, ws_up, ws_down, ys, seq, tm):
    t, d = x1.shape
    per_b = seq // tm
    return pl.pallas_call(
        functools.partial(_combine_kernel, tm=tm),
        grid=(t // tm,),
        in_specs=[
            pl.BlockSpec((1, 1, TOP_K * tm), lambda i: (i, 0, 0), memory_space=pltpu.SMEM),
            pl.BlockSpec((tm, d), lambda i: (i, 0)),
            pl.BlockSpec((tm, d), lambda i: (i, 0)),
            pl.BlockSpec((tm, TOP_K), lambda i: (i, 0)),
            pl.BlockSpec((1, 6, d), lambda i: (i // per_b, 0, 0)),
            pl.BlockSpec((d, EXPERT_DIM), lambda i: (0, 0)),
            pl.BlockSpec((d, EXPERT_DIM), lambda i: (0, 0)),
            pl.BlockSpec((EXPERT_DIM, d), lambda i: (0, 0)),
            pl.BlockSpec(memory_space=pl.ANY),
        ],
        out_specs=pl.BlockSpec((tm, d), lambda i: (i, 0)),
        out_shape=jax.ShapeDtypeStruct((t, d), F32),
        scratch_shapes=[pltpu.VMEM((TOP_K, tm * ROW_SLAB, LANES), U32), pltpu.SemaphoreType.DMA(())],
        compiler_params=_cparams(("arbitrary",)),
        name="combine",
    )(dest_blocks, x1, h2b, wgt_t, mod3, ws_gate, ws_up, ws_down, ys)


def _block_major(a, blk):
    k, t = a.shape
    return a.reshape(k, t // blk, blk).transpose(1, 0, 2).reshape(t // blk, 1, k * blk)


def kernel(x, c, positions, w_ada, b_ada, norm1_w, w_in, q_lat_norm_w, w_uq, kv_lat_norm_w, w_ukv, q_norm_w, k_norm_w, lb_fwd, lb_bwd, rec_norm_w, w_out, norm2_w, w_router, router_bias, w_exp_gate, w_exp_up, w_exp_down, w_sh_gate, w_sh_up, w_sh_down):
    bsz, seq, d = x.shape
    t = bsz * seq
    x2 = x.reshape(t, d)

    w_in0 = w_in[0]
    n_lat = Q_RANK + KV_RANK + QK_ROPE
    w_lat = jnp.pad(w_in0[:, :n_lat], ((0, 0), (0, LAT_PAD - n_lat))).astype(BF16)
    w_rec = w_in0[:, n_lat:].astype(BF16)
    wuq = jnp.pad(w_uq[0].reshape(Q_RANK, N_HEADS, QK_DIM), ((0, 0), (0, 0), (0, QK_PAD - QK_DIM)))
    wuq = wuq.transpose(1, 0, 2).astype(BF16)
    wukv = w_ukv[0].reshape(KV_RANK, N_HEADS, QK_NOPE + V_DIM).transpose(1, 0, 2).astype(BF16)
    qnw = jnp.pad(q_norm_w[0], (0, QK_PAD - QK_DIM)).reshape(1, QK_PAD)
    knw = jnp.pad(k_norm_w[0], (0, QK_PAD - QK_DIM)).reshape(1, QK_PAD)
    half = N_HEADS * V_DIM
    w_out_att = w_out[0, :half].astype(BF16)
    w_out_rec = w_out[0, half:].astype(BF16)
    wr_hi, wr_lo = _split_bf16(w_router[0].T)
    lbf = jnp.cumsum(jax.nn.softmax(lb_fwd.astype(F32), axis=0), axis=0)[0].reshape(1, -1)
    lbb = jnp.cumsum(jax.nn.softmax(lb_bwd.astype(F32), axis=0), axis=0)[0].reshape(1, -1)

    inv_freq = ROPE_THETA ** (-jnp.arange(0, QK_ROPE, 2, dtype=F32) / QK_ROPE)
    ang = positions.astype(F32).reshape(t, 1) * inv_freq[None, :]
    cos_h, sin_h = jnp.cos(ang), jnp.sin(ang)
    zh = jnp.zeros_like(cos_h)
    cos_t = jnp.concatenate([cos_h, cos_h, zh, zh], axis=-1)
    sin_a = jnp.concatenate([-sin_h, zh, zh, zh], axis=-1)
    sin_b = jnp.concatenate([zh, sin_h, zh, zh], axis=-1)

    mod3 = _ada(c, w_ada[0], b_ada[0]).reshape(bsz, 6, d)
    lat, rec_in = _inproj(x2, mod3, norm1_w[0], w_lat, w_rec, seq)
    q, k, v = _mla_prep(lat, cos_t, sin_a, sin_b, q_lat_norm_w[0].reshape(1, -1), wuq,
                        kv_lat_norm_w[0].reshape(1, -1), wukv, qnw, knw, bsz, seq)
    att = _attention(q, k, v)
    rec = _hgrn(rec_in, lbf, lbb, rec_norm_w[0], bsz, seq)
    x1 = _outproj(x2, att, rec, mod3, w_out_att, w_out_rec, seq)

    h2b, h2p, top_idx, top_w, rank, counts = _router(x1, mod3, norm2_w[0], wr_hi, wr_lo,
                                                     router_bias[0], seq)
    counts = counts[:, 0].astype(I32)
    padded = (counts + MOE_BLOCK - 1) // MOE_BLOCK * MOE_BLOCK
    pad_end = jnp.cumsum(padded)
    pad_start = pad_end - padded
    dest = jnp.take(pad_start, top_idx) + rank
    n_blocks = -(-(t * TOP_K) // MOE_BLOCK) + N_EXPERTS
    n_used = (pad_end[-1] // MOE_BLOCK).reshape(1).astype(I32)
    blk_starts = jnp.arange(n_blocks, dtype=I32) * MOE_BLOCK
    blk_expert = jnp.searchsorted(pad_end, jnp.minimum(blk_starts, pad_end[-1] - 1), side='right')
    blk_expert = jnp.minimum(blk_expert, N_EXPERTS - 1).astype(I32)

    tc = 512
    xs = _dispatch(_block_major(dest, tc), pad_end.astype(I32), h2p, n_blocks * MOE_BLOCK, tc)
    ys = _experts(blk_expert, n_used, xs, w_exp_gate[0], w_exp_up[0], w_exp_down[0])
    tm = 256
    out = _combine(_block_major(dest, tm), x1, h2b, top_w.T, mod3, w_sh_gate[0].astype(BF16),
                   w_sh_up[0].astype(BF16), w_sh_down[0].astype(BF16), ys, seq, tm)
    return out.reshape(bsz, seq, d)
```

```python
import functools

import numpy as np
import jax
import jax.numpy as jnp
from jax import lax
from jax.experimental import pallas as pl
from jax.experimental.pallas import tpu as pltpu

F32 = jnp.float32
BF16 = jnp.bfloat16
U32 = jnp.uint32
I32 = jnp.int32

N_HEADS = 8
QK_NOPE = 128
QK_ROPE = 64
QK_DIM = QK_NOPE + QK_ROPE
QK_PAD = 256
V_DIM = 128
Q_RANK = 512
KV_RANK = 256
LAT_PAD = 896
REC_DIM = 128
REC_COLS = 5 * N_HEADS * REC_DIM
ROPE_THETA = 10000.0
N_EXPERTS = 64
TOP_K = 8
N_GROUPS = 8
TOPK_GROUPS = 4
EXPERT_DIM = 512
ROUTED_SCALE = 2.5
MOE_BLOCK = 256
EPS = 1e-6

LANES = 128
SUBLANES = 8
ROW_SLAB = 8
VMEM_LIMIT = 56 * 1024 * 1024

HG_TILE = 128
HG_BAND = 16
HG_LEVELS = (16, 32, 64)


def _cparams(sem, vmem=VMEM_LIMIT):
    return pltpu.CompilerParams(dimension_semantics=sem, vmem_limit_bytes=vmem)


def _split_bf16(a):
    hi = a.astype(BF16)
    lo = (a - hi.astype(F32)).astype(BF16)
    return hi, lo


def _pack_pair(a, b):
    ua = lax.bitcast_convert_type(a.astype(BF16).astype(F32), U32)
    ub = lax.bitcast_convert_type(b.astype(BF16).astype(F32), U32)
    return ua | (ub >> 16)


def _unpack_pair(p):
    a = lax.bitcast_convert_type(p & jnp.uint32(0xFFFF0000), F32).astype(BF16)
    b = lax.bitcast_convert_type(p << 16, F32).astype(BF16)
    return a, b


def _store_slabs(ref, packed):
    n = packed.shape[0]
    for j in range(ROW_SLAB):
        ref[pl.ds(j, n, stride=ROW_SLAB), :] = packed[:, j * LANES:(j + 1) * LANES]


def _load_slab_chunk(ref, j, n):
    return ref[pl.ds(j, n, stride=ROW_SLAB), :]


def _dot(a, b):
    return jnp.dot(a, b, preferred_element_type=F32)


def _dot_nt(a, b):
    return lax.dot_general(a, b, (((1,), (1,)), ((), ())), preferred_element_type=F32)


def _dot_tn(a, b):
    return lax.dot_general(a, b, (((0,), (0,)), ((), ())), preferred_element_type=F32)


def _ada_kernel(c_ref, w_ref, b_ref, o_ref):
    c = c_ref[...]
    cond = c * jax.nn.sigmoid(c)
    o_ref[...] = _dot(cond.astype(BF16), w_ref[...].astype(BF16)) + b_ref[...]


def _ada(c, w_ada, b_ada):
    bsz, d = c.shape
    n = w_ada.shape[1]
    tn = 1024
    return pl.pallas_call(
        _ada_kernel,
        grid=(n // tn,),
        in_specs=[
            pl.BlockSpec((bsz, d), lambda j: (0, 0)),
            pl.BlockSpec((d, tn), lambda j: (0, j)),
            pl.BlockSpec((1, tn), lambda j: (0, j)),
        ],
        out_specs=pl.BlockSpec((bsz, tn), lambda j: (0, j)),
        out_shape=jax.ShapeDtypeStruct((bsz, n), F32),
        compiler_params=_cparams(("arbitrary",)),
        name="ada",
    )(c, w_ada, b_ada.reshape(1, n))


def _modulated_norm(x, mod_ref, nw, shift_row, scale_row):
    r = lax.rsqrt(jnp.mean(x * x, axis=-1, keepdims=True) + EPS)
    scale = mod_ref[0, scale_row:scale_row + 1, :]
    shift = mod_ref[0, shift_row:shift_row + 1, :]
    return (x * r * nw) * (1.0 + scale) + shift


def _inproj_kernel(x_ref, mod_ref, nw_ref, wlat_ref, wrec_ref, lat_ref, rec_ref, h_scr):
    @pl.when(pl.program_id(1) == 0)
    def _():
        h = _modulated_norm(x_ref[...], mod_ref, nw_ref[...], 0, 1).astype(BF16)
        h_scr[...] = h
        lat_ref[...] = _dot(h, wlat_ref[...])

    rec_ref[...] = _dot(h_scr[...], wrec_ref[...])


def _inproj(x2, mod3, norm_w, w_lat, w_rec, seq):
    t, d = x2.shape
    tm, tn = 512, 1024
    per_b = seq // tm
    return pl.pallas_call(
        _inproj_kernel,
        grid=(t // tm, REC_COLS // tn),
        in_specs=[
            pl.BlockSpec((tm, d), lambda i, j: (i, 0)),
            pl.BlockSpec((1, 6, d), lambda i, j: (i // per_b, 0, 0)),
            pl.BlockSpec((1, d), lambda i, j: (0, 0)),
            pl.BlockSpec((d, LAT_PAD), lambda i, j: (0, 0)),
            pl.BlockSpec((d, tn), lambda i, j: (0, j)),
        ],
        out_specs=[
            pl.BlockSpec((tm, LAT_PAD), lambda i, j: (i, 0)),
            pl.BlockSpec((tm, tn), lambda i, j: (i, j)),
        ],
        out_shape=[
            jax.ShapeDtypeStruct((t, LAT_PAD), F32),
            jax.ShapeDtypeStruct((t, REC_COLS), F32),
        ],
        scratch_shapes=[pltpu.VMEM((tm, d), BF16)],
        compiler_params=_cparams(("parallel", "arbitrary")),
        name="inproj",
    )(x2, mod3, norm_w.reshape(1, d), w_lat, w_rec)


def _rope(x2, cos, sin_a, sin_b):
    return (x2 * cos + pltpu.roll(x2, LANES - QK_ROPE // 2, 1) * sin_a
            + pltpu.roll(x2, QK_ROPE // 2, 1) * sin_b)


def _mla_prep_kernel(lat_ref, cos_ref, sa_ref, sb_ref, qlw_ref, wuq_ref, kvlw_ref, wukv_ref,
                     qnw_ref, knw_ref, q_ref, k_ref, v_ref):
    lat = lat_ref[...]
    ql = lat[:, :Q_RANK]
    kvl = lat[:, Q_RANK:Q_RANK + KV_RANK]
    kr = lat[:, Q_RANK + KV_RANK:]
    qn = (ql * lax.rsqrt(jnp.mean(ql * ql, -1, keepdims=True) + EPS) * qlw_ref[...]).astype(BF16)
    kvn = (kvl * lax.rsqrt(jnp.mean(kvl * kvl, -1, keepdims=True) + EPS) * kvlw_ref[...]).astype(BF16)
    cos, sa, sb = cos_ref[...], sa_ref[...], sb_ref[...]
    qnw = qnw_ref[...]
    knw = knw_ref[...]
    kr_ss = jnp.sum(kr * kr, -1, keepdims=True)
    kr_roped = _rope(kr * knw[:, QK_NOPE:], cos, sa, sb)
    scale = QK_DIM ** -0.5
    for h in range(N_HEADS):
        qh = _dot(qn, wuq_ref[h])
        r = lax.rsqrt(jnp.sum(qh * qh, -1, keepdims=True) * (1.0 / QK_DIM) + EPS) * scale
        qh = qh * r * qnw
        q_ref[0, h, :, :QK_NOPE] = qh[:, :QK_NOPE].astype(BF16)
        q_ref[0, h, :, QK_NOPE:] = _rope(qh[:, QK_NOPE:], cos, sa, sb).astype(BF16)
        kvh = _dot(kvn, wukv_ref[h])
        kn = kvh[:, :QK_NOPE]
        rk = lax.rsqrt((jnp.sum(kn * kn, -1, keepdims=True) + kr_ss) * (1.0 / QK_DIM) + EPS)
        k_ref[0, h, :, :QK_NOPE] = (kn * rk * knw[:, :QK_NOPE]).astype(BF16)
        k_ref[0, h, :, QK_NOPE:] = (kr_roped * rk).astype(BF16)
        v_ref[0, h] = kvh[:, QK_NOPE:].astype(BF16)


def _mla_prep(lat, cos, sa, sb, qlw, wuq, kvlw, wukv, qnw, knw, bsz, seq):
    tm = 512
    per_b = seq // tm
    const2 = lambda i: (0, 0)
    const3 = lambda i: (0, 0, 0)
    head_out = lambda w: pl.BlockSpec((1, N_HEADS, tm, w), lambda i: (i // per_b, 0, i % per_b, 0))
    return pl.pallas_call(
        _mla_prep_kernel,
        grid=(bsz * seq // tm,),
        in_specs=[
            pl.BlockSpec((tm, LAT_PAD), lambda i: (i, 0)),
            pl.BlockSpec((tm, LANES), lambda i: (i, 0)),
            pl.BlockSpec((tm, LANES), lambda i: (i, 0)),
            pl.BlockSpec((tm, LANES), lambda i: (i, 0)),
            pl.BlockSpec((1, Q_RANK), const2),
            pl.BlockSpec((N_HEADS, Q_RANK, QK_PAD), const3),
            pl.BlockSpec((1, KV_RANK), const2),
            pl.BlockSpec((N_HEADS, KV_RANK, QK_PAD), const3),
            pl.BlockSpec((1, QK_PAD), const2),
            pl.BlockSpec((1, QK_PAD), const2),
        ],
        out_specs=[head_out(QK_PAD), head_out(QK_PAD), head_out(V_DIM)],
        out_shape=[
            jax.ShapeDtypeStruct((bsz, N_HEADS, seq, QK_PAD), BF16),
            jax.ShapeDtypeStruct((bsz, N_HEADS, seq, QK_PAD), BF16),
            jax.ShapeDtypeStruct((bsz, N_HEADS, seq, V_DIM), BF16),
        ],
        compiler_params=_cparams(("parallel",)),
        name="mla_prep",
    )(lat, cos, sa, sb, qlw, wuq, kvlw, wukv, qnw, knw)


def _attn_kernel(q_ref, k_ref, v_ref, o_ref):
    s = _dot_nt(q_ref[0, 0], k_ref[0, 0])
    m = jnp.max(s, axis=-1, keepdims=True)
    p = jnp.exp(s - m)
    l = jnp.sum(p, axis=-1, keepdims=True)
    o_ref[...] = _dot(p.astype(BF16), v_ref[0, 0]) / l


def _attention(q, k, v):
    bsz, nh, seq, _ = q.shape
    tq = 512
    nq = seq // tq
    return pl.pallas_call(
        _attn_kernel,
        grid=(bsz, nh, nq),
        in_specs=[
            pl.BlockSpec((1, 1, tq, QK_PAD), lambda b, h, i: (b, h, i, 0)),
            pl.BlockSpec((1, 1, seq, QK_PAD), lambda b, h, i: (b, h, 0, 0)),
            pl.BlockSpec((1, 1, seq, V_DIM), lambda b, h, i: (b, h, 0, 0)),
        ],
        out_specs=pl.BlockSpec((tq, V_DIM), lambda b, h, i: (b * nq + i, h)),
        out_shape=jax.ShapeDtypeStruct((bsz * seq, nh * V_DIM), F32),
        compiler_params=_cparams(("parallel", "parallel", "arbitrary")),
        name="attention",
    )(q, k, v)


def _hgrn_constants(rev):
    n = HG_TILE
    t = np.arange(n)[:, None]
    u = np.arange(n)[None, :]
    mats = []
    level_id = np.zeros((n, n), np.float32)
    for li, m in enumerate(HG_LEVELS):
        blk_t, blk_u = t // m, u // m
        same = blk_t == blk_u
        odd = (blk_t % 2) == 1
        if not rev:
            mat = np.where(odd, same & (u <= t), same & (u > t))
            lvl = odd & ((u // m) == blk_t - 1)
        else:
            mat = np.where(~odd, same & (u >= t), same & (u < t))
            lvl = (~odd) & ((u // m) == blk_t + 1)
        mats.append(mat.astype(np.float32))
        level_id[lvl] = li + 1
    if not rev:
        mats.append((u <= t).astype(np.float32))
        mats.append((u > t).astype(np.float32))
    else:
        mats.append((u >= t).astype(np.float32))
        mats.append((u < t).astype(np.float32))
    return np.concatenate(mats, axis=0), level_id


def _hgrn_direction(rev, q_ref, f_scr, kk_scr, v_scr, g_scr, o_scr, cum_ref, lvl_ref, first):
    seq = q_ref.shape[0]
    n_tiles = seq // HG_TILE
    pad = HG_BAND
    sgn = 1 if rev else -1
    row = lax.broadcasted_iota(I32, (HG_TILE, 1), 0)
    ones_b = jnp.ones((HG_TILE, LANES), BF16)
    nlev = len(HG_LEVELS)

    def tile_body(n, state):
        tile = (n_tiles - 1 - n) if rev else n
        t0 = pl.multiple_of(tile * HG_TILE, HG_TILE)
        q = q_ref[pl.ds(t0, HG_TILE), :]
        g = g_scr[pl.ds(t0, HG_TILE), :]
        kk = kk_scr[pl.ds(pad + t0, HG_TILE), :]
        v = v_scr[pl.ds(pad + t0, HG_TILE), :]
        v_b = v.astype(BF16)
        g_hi, g_lo = _split_bf16(g)
        cum = _dot(cum_ref[...], g_hi) + _dot(cum_ref[...], g_lo)
        w = jnp.exp(cum)

        lvl = lvl_ref[...]
        a = jnp.zeros((HG_TILE, HG_TILE), F32)
        for li, m in enumerate(HG_LEVELS):
            odd = ((row // m) % 2) == 1
            is_query = jnp.logical_not(odd) if rev else odd
            x = (jnp.where(is_query, q, kk) * w[li * HG_TILE:(li + 1) * HG_TILE]).astype(BF16)
            a = jnp.where(lvl == float(li + 1), _dot_nt(x, x), a)
        o = _dot(a.astype(BF16), v_b)

        wq = w[nlev * HG_TILE:(nlev + 1) * HG_TILE]
        wk = w[(nlev + 1) * HG_TILE:(nlev + 2) * HG_TILE]
        o = o + _dot_nt((q * wq).astype(BF16), state.astype(BF16))
        total = wq[0:1, :] if rev else wq[HG_TILE - 1:HG_TILE, :]
        new_state = state * total + _dot_tn(v_b, (kk * wk).astype(BF16))

        fprod = None
        for d in range(HG_BAND):
            if d == 1:
                fprod = f_scr[pl.ds(pad + t0, HG_TILE), :]
            elif d > 1:
                fprod = fprod * f_scr[pl.ds(pad + t0 + sgn * (d - 1), HG_TILE), :]
            p = q * kk_scr[pl.ds(pad + t0 + sgn * d, HG_TILE), :]
            if fprod is not None:
                p = p * fprod
            rs = _dot(p.astype(BF16), ones_b)
            o = o + rs * v_scr[pl.ds(pad + t0 + sgn * d, HG_TILE), :]

        if first:
            o_scr[pl.ds(t0, HG_TILE), :] = o
        else:
            o_scr[pl.ds(t0, HG_TILE), :] = o_scr[pl.ds(t0, HG_TILE), :] + o
        return new_state

    lax.fori_loop(0, n_tiles, tile_body, jnp.zeros((V_DIM, REC_DIM), F32))


def _hgrn_kernel(q_ref, zf_ref, zb_ref, v_ref, gate_ref, lbf_ref, lbb_ref, nw_ref,
                 cumf_ref, lvlf_ref, cumb_ref, lvlb_ref, o_ref,
                 f_scr, kk_scr, v_scr, g_scr, o_scr):
    seq = q_ref.shape[0]
    pad = HG_BAND
    zeros_pad = jnp.zeros((pad, LANES), F32)
    for scr in (f_scr, kk_scr, v_scr):
        scr[pl.ds(0, pad), :] = zeros_pad
        scr[pl.ds(pad + seq, pad), :] = zeros_pad
    v_scr[pl.ds(pad, seq), :] = v_ref[...]
    pos = lax.broadcasted_iota(I32, (seq, 1), 0) % HG_BAND

    for rev, z_ref, lb_ref, cum_ref, lvl_ref in ((False, zf_ref, lbf_ref, cumf_ref, lvlf_ref),
                                                 (True, zb_ref, lbb_ref, cumb_ref, lvlb_ref)):
        lb = lb_ref[...]
        f = lb + (1.0 - lb) * jax.nn.sigmoid(z_ref[...])
        g_scr[...] = jnp.log(f)
        kk_scr[pl.ds(pad, seq), :] = 1.0 - f
        boundary = (pos == HG_BAND - 1) if rev else (pos == 0)
        f_scr[pl.ds(pad, seq), :] = jnp.where(boundary, 0.0, f)
        _hgrn_direction(rev, q_ref, f_scr, kk_scr, v_scr, g_scr, o_scr, cum_ref, lvl_ref,
                        first=not rev)

    o = o_scr[...]
    o = o * lax.rsqrt(jnp.mean(o * o, -1, keepdims=True) + EPS) * nw_ref[...]
    gate = gate_ref[...]
    o_ref[...] = o * (gate * jax.nn.sigmoid(gate))


def _hgrn(rec, lbf, lbb, rec_norm_w, bsz, seq):
    cumf, lvlf = _hgrn_constants(False)
    cumb, lvlb = _hgrn_constants(True)
    ncum = cumf.shape[0]
    col = lambda c: pl.BlockSpec((seq, REC_DIM), lambda b, h: (b, c * N_HEADS + h))
    vec = pl.BlockSpec((1, REC_DIM), lambda b, h: (0, h))
    const = lambda r: pl.BlockSpec((r, HG_TILE), lambda b, h: (0, 0))
    padded = pltpu.VMEM((seq + 2 * HG_BAND, LANES), F32)
    return pl.pallas_call(
        _hgrn_kernel,
        grid=(bsz, N_HEADS),
        in_specs=[col(0), col(1), col(2), col(3), col(4), vec, vec,
                  pl.BlockSpec((1, REC_DIM), lambda b, h: (0, 0)),
                  const(ncum), const(HG_TILE), const(ncum), const(HG_TILE)],
        out_specs=pl.BlockSpec((seq, REC_DIM), lambda b, h: (b, h)),
        out_shape=jax.ShapeDtypeStruct((bsz * seq, N_HEADS * REC_DIM), F32),
        scratch_shapes=[padded, padded, padded,
                        pltpu.VMEM((seq, LANES), F32), pltpu.VMEM((seq, LANES), F32)],
        compiler_params=_cparams(("parallel", "parallel")),
        name="hgrn",
    )(rec, rec, rec, rec, rec, lbf, lbb, rec_norm_w.reshape(1, REC_DIM),
      jnp.asarray(cumf, BF16), jnp.asarray(lvlf, F32), jnp.asarray(cumb, BF16), jnp.asarray(lvlb, F32))


def _outproj_kernel(x_ref, att_ref, rec_ref, mod_ref, wa_ref, wr_ref, o_ref):
    y = _dot(att_ref[...].astype(BF16), wa_ref[...]) + _dot(rec_ref[...].astype(BF16), wr_ref[...])
    o_ref[...] = x_ref[...] + mod_ref[0, 2:3, :] * y


def _outproj(x2, att, rec, mod3, w_att, w_rec, seq):
    t, d = x2.shape
    tm = 512
    per_b = seq // tm
    half = att.shape[1]
    return pl.pallas_call(
        _outproj_kernel,
        grid=(t // tm,),
        in_specs=[
            pl.BlockSpec((tm, d), lambda i: (i, 0)),
            pl.BlockSpec((tm, half), lambda i: (i, 0)),
            pl.BlockSpec((tm, half), lambda i: (i, 0)),
            pl.BlockSpec((1, 6, d), lambda i: (i // per_b, 0, 0)),
            pl.BlockSpec((half, d), lambda i: (0, 0)),
            pl.BlockSpec((half, d), lambda i: (0, 0)),
        ],
        out_specs=pl.BlockSpec((tm, d), lambda i: (i, 0)),
        out_shape=jax.ShapeDtypeStruct((t, d), F32),
        compiler_params=_cparams(("parallel",)),
        name="outproj",
    )(x2, att, rec, mod3, w_att, w_rec)


def _router_kernel(x_ref, mod_ref, nw_ref, wrh_ref, wrl_ref, bias_ref, tri_ref,
                   h2b_ref, h2p_ref, idx_ref, wgt_ref, rank_ref, cnt_ref, base_scr):
    i = pl.program_id(0)
    tm = x_ref.shape[0]
    half = x_ref.shape[1] // 2

    @pl.when(i == 0)
    def _():
        base_scr[...] = jnp.zeros_like(base_scr)

    h2 = _modulated_norm(x_ref[...], mod_ref, nw_ref[...], 3, 4)
    h_hi, h_lo = _split_bf16(h2)
    h2b_ref[...] = h_hi
    _store_slabs(h2p_ref, _pack_pair(h2[:, :half], h2[:, half:]))
    logits = _dot_nt(wrh_ref[...], h_hi) + _dot_nt(wrh_ref[...], h_lo) + _dot_nt(wrl_ref[...], h_hi)
    scores = jax.nn.sigmoid(logits)
    sel = scores + bias_ref[...]

    neg = -jnp.inf
    per_grp = N_EXPERTS // N_GROUPS
    sel3 = sel.reshape(N_GROUPS, per_grp, tm)
    within = lax.broadcasted_iota(I32, sel3.shape, 1)
    m1 = jnp.max(sel3, axis=1, keepdims=True)
    first = jnp.min(jnp.where(sel3 == m1, within, per_grp), axis=1, keepdims=True)
    m2 = jnp.max(jnp.where(within == first, neg, sel3), axis=1, keepdims=True)
    gs = (m1 + m2).reshape(N_GROUPS, tm)
    gidx = lax.broadcasted_iota(I32, gs.shape, 0)
    beaten = jnp.zeros(gs.shape, I32)
    for g2 in range(N_GROUPS):
        other = gs[g2:g2 + 1, :]
        beaten = beaten + ((other > gs) | ((other == gs) & (g2 < gidx))).astype(I32)
    keep = (beaten < TOPK_GROUPS).reshape(N_GROUPS, 1, tm)
    cur = jnp.where(keep, sel3, neg).reshape(N_EXPERTS, tm)

    eidx = lax.broadcasted_iota(I32, (N_EXPERTS, tm), 0)
    chosen = jnp.zeros((N_EXPERTS, tm), F32)
    picks, pick_scores = [], []
    for _ in range(TOP_K):
        m = jnp.max(cur, axis=0, keepdims=True)
        pick = jnp.min(jnp.where(cur == m, eidx, N_EXPERTS), axis=0, keepdims=True)
        hit = eidx == pick
        picks.append(pick)
        pick_scores.append(jnp.sum(jnp.where(hit, scores, 0.0), axis=0, keepdims=True))
        chosen = jnp.where(hit, 1.0, chosen)
        cur = jnp.where(hit, neg, cur)

    before = _dot(chosen.astype(BF16), tri_ref[...]) + base_scr[...][:, 0:1]
    denom = pick_scores[0]
    for s in pick_scores[1:]:
        denom = denom + s
    for j in range(TOP_K):
        idx_ref[j:j + 1, :] = picks[j]
        wgt_ref[j:j + 1, :] = pick_scores[j] / denom * ROUTED_SCALE
        rank_ref[j:j + 1, :] = jnp.sum(jnp.where(eidx == picks[j], before, 0.0), axis=0,
                                       keepdims=True).astype(I32)
    base_scr[...] = base_scr[...] + jnp.sum(chosen, axis=1, keepdims=True)
    cnt_ref[...] = base_scr[...]


def _router(x1, mod3, norm_w, wr_hi, wr_lo, bias, seq):
    t, d = x1.shape
    tm = 512
    per_b = seq // tm
    tri = jnp.asarray(np.triu(np.ones((tm, tm), np.float32), 1), BF16)
    row_out = lambda dt: jax.ShapeDtypeStruct((TOP_K, t), dt)
    row_spec = pl.BlockSpec((TOP_K, tm), lambda i: (0, i))
    return pl.pallas_call(
        _router_kernel,
        grid=(t // tm,),
        in_specs=[
            pl.BlockSpec((tm, d), lambda i: (i, 0)),
            pl.BlockSpec((1, 6, d), lambda i: (i // per_b, 0, 0)),
            pl.BlockSpec((1, d), lambda i: (0, 0)),
            pl.BlockSpec((N_EXPERTS, d), lambda i: (0, 0)),
            pl.BlockSpec((N_EXPERTS, d), lambda i: (0, 0)),
            pl.BlockSpec((N_EXPERTS, 1), lambda i: (0, 0)),
            pl.BlockSpec((tm, tm), lambda i: (0, 0)),
        ],
        out_specs=[
            pl.BlockSpec((tm, d), lambda i: (i, 0)),
            pl.BlockSpec((tm * ROW_SLAB, LANES), lambda i: (i, 0)),
            row_spec, row_spec, row_spec,
            pl.BlockSpec((N_EXPERTS, LANES), lambda i: (0, 0)),
        ],
        out_shape=[
            jax.ShapeDtypeStruct((t, d), BF16),
            jax.ShapeDtypeStruct((t * ROW_SLAB, LANES), U32),
            row_out(I32), row_out(F32), row_out(I32),
            jax.ShapeDtypeStruct((N_EXPERTS, LANES), F32),
        ],
        scratch_shapes=[pltpu.VMEM((N_EXPERTS, LANES), F32)],
        compiler_params=_cparams(("arbitrary",)),
        name="router",
    )(x1, mod3, norm_w.reshape(1, d), wr_hi, wr_lo, bias.reshape(N_EXPERTS, 1), tri)


def _dispatch_kernel(dest_ref, pad_end_ref, h2p_ref, xs_ref, zero_scr, sem, *, tc):
    i = pl.program_id(0)

    @pl.when(i == 0)
    def _():
        zero_scr[...] = jnp.zeros_like(zero_scr)

        def fill(e, prev_end):
            end = pad_end_ref[e]

            @pl.when(end > prev_end)
            def _():
                start = pl.multiple_of((end - MOE_BLOCK) * ROW_SLAB, MOE_BLOCK * ROW_SLAB)
                cp = pltpu.make_async_copy(zero_scr, xs_ref.at[pl.ds(start, MOE_BLOCK * ROW_SLAB)], sem)
                cp.start()
                cp.wait()
            return end

        lax.fori_loop(0, N_EXPERTS, fill, jnp.int32(0))

        def fill_unused(blk, carry):
            start = pl.multiple_of(blk * (MOE_BLOCK * ROW_SLAB), MOE_BLOCK * ROW_SLAB)
            cp = pltpu.make_async_copy(zero_scr, xs_ref.at[pl.ds(start, MOE_BLOCK * ROW_SLAB)], sem)
            cp.start()
            cp.wait()
            return carry

        n_blocks = xs_ref.shape[0] // (MOE_BLOCK * ROW_SLAB)
        lax.fori_loop(pad_end_ref[N_EXPERTS - 1] // MOE_BLOCK, n_blocks, fill_unused, 0)

    def issue(t, carry):
        src = h2p_ref.at[pl.ds(pl.multiple_of(t * ROW_SLAB, ROW_SLAB), ROW_SLAB)]
        for j in range(TOP_K):
            d = pl.multiple_of(dest_ref[0, 0, j * tc + t] * ROW_SLAB, ROW_SLAB)
            pltpu.make_async_copy(src, xs_ref.at[pl.ds(d, ROW_SLAB)], sem).start()
        return carry

    lax.fori_loop(0, tc, issue, 0)
    for _ in range(TOP_K):
        pltpu.make_async_copy(h2p_ref, xs_ref.at[pl.ds(0, tc * ROW_SLAB)], sem).wait()


def _dispatch(dest_blocks, pad_end, h2p, n_rows, tc):
    t, w = h2p.shape[0] // ROW_SLAB, LANES
    n_rows = n_rows * ROW_SLAB
    return pl.pallas_call(
        functools.partial(_dispatch_kernel, tc=tc),
        grid=(t // tc,),
        in_specs=[
            pl.BlockSpec((1, 1, TOP_K * tc), lambda i: (i, 0, 0), memory_space=pltpu.SMEM),
            pl.BlockSpec(memory_space=pltpu.SMEM),
            pl.BlockSpec((tc * ROW_SLAB, w), lambda i: (i, 0)),
        ],
        out_specs=pl.BlockSpec(memory_space=pl.ANY),
        out_shape=jax.ShapeDtypeStruct((n_rows, w), U32),
        scratch_shapes=[pltpu.VMEM((MOE_BLOCK * ROW_SLAB, w), U32), pltpu.SemaphoreType.DMA(())],
        compiler_params=_cparams(("arbitrary",)),
        name="dispatch",
    )(dest_blocks, pad_end, h2p)


def _expert_kernel(be_ref, nused_ref, xs_ref, wg_ref, wu_ref, wd_ref, ys_ref,
                   wg_scr, wu_scr, wd_scr, x_scr):
    b = pl.program_id(0)
    e = be_ref[b]
    prev = be_ref[jnp.maximum(b - 1, 0)]
    half = ROW_SLAB * LANES

    @pl.when((b == 0) | (e != prev))
    def _():
        wg_scr[...] = wg_ref[...].astype(BF16)
        wu_scr[...] = wu_ref[...].astype(BF16)
        wd_scr[...] = wd_ref[...].astype(BF16)

    @pl.when(b < nused_ref[0])
    def _():
        for j in range(ROW_SLAB):
            xa, xb = _unpack_pair(_load_slab_chunk(xs_ref, j, MOE_BLOCK))
            x_scr[:, j * LANES:(j + 1) * LANES] = xa
            x_scr[:, half + j * LANES:half + (j + 1) * LANES] = xb
        xfull = x_scr[...]
        gate = _dot(xfull, wg_scr[...])
        up = _dot(xfull, wu_scr[...])
        mid = ((gate * jax.nn.sigmoid(gate)) * up).astype(BF16)
        y = _dot(mid, wd_scr[...])
        _store_slabs(ys_ref, _pack_pair(y[:, :half], y[:, half:]))

    @pl.when(b >= nused_ref[0])
    def _():
        ys_ref[...] = jnp.zeros_like(ys_ref)


def _experts(blk_expert, n_used, xs, w_gate, w_up, w_down):
    n_blocks = xs.shape[0] // (MOE_BLOCK * ROW_SLAB)
    d = 2 * ROW_SLAB * LANES
    last = lambda b, be, nu: jnp.minimum(b, jnp.maximum(nu[0] - 1, 0))
    return pl.pallas_call(
        _expert_kernel,
        grid_spec=pltpu.PrefetchScalarGridSpec(
            num_scalar_prefetch=2,
            grid=(n_blocks,),
            in_specs=[
                pl.BlockSpec((MOE_BLOCK * ROW_SLAB, LANES), lambda b, be, nu: (last(b, be, nu), 0)),
                pl.BlockSpec((None, d, EXPERT_DIM), lambda b, be, nu: (be[b], 0, 0)),
                pl.BlockSpec((None, d, EXPERT_DIM), lambda b, be, nu: (be[b], 0, 0)),
                pl.BlockSpec((None, EXPERT_DIM, d), lambda b, be, nu: (be[b], 0, 0)),
            ],
            out_specs=pl.BlockSpec((MOE_BLOCK * ROW_SLAB, LANES), lambda b, be, nu: (b, 0)),
            scratch_shapes=[pltpu.VMEM((d, EXPERT_DIM), BF16), pltpu.VMEM((d, EXPERT_DIM), BF16),
                            pltpu.VMEM((EXPERT_DIM, d), BF16), pltpu.VMEM((MOE_BLOCK, d), BF16)],
        ),
        out_shape=jax.ShapeDtypeStruct(xs.shape, U32),
        compiler_params=_cparams(("arbitrary",)),
        name="experts",
    )(blk_expert, n_used, xs, w_gate, w_up, w_down)


def _combine_kernel(dest_ref, x1_ref, h2b_ref, wgt_ref, mod_ref, wsg_ref, wsu_ref, wsd_ref, ys_ref,
                    o_ref, gbuf, sem, *, tm):
    def issue(t, carry):
        row = pl.multiple_of(t * ROW_SLAB, ROW_SLAB)
        for j in range(TOP_K):
            d = pl.multiple_of(dest_ref[0, 0, j * tm + t] * ROW_SLAB, ROW_SLAB)
            pltpu.make_async_copy(ys_ref.at[pl.ds(d, ROW_SLAB)], gbuf.at[j, pl.ds(row, ROW_SLAB)],
                                  sem).start()
        return carry

    lax.fori_loop(0, tm, issue, 0)

    h2 = h2b_ref[...]
    gate = _dot(h2, wsg_ref[...])
    up = _dot(h2, wsu_ref[...])
    shared = _dot(((gate * jax.nn.sigmoid(gate)) * up).astype(BF16), wsd_ref[...])

    for j in range(TOP_K):
        pltpu.make_async_copy(ys_ref.at[pl.ds(0, tm * ROW_SLAB)], gbuf.at[j], sem).wait()

    half = ROW_SLAB * LANES
    wgt = wgt_ref[...]
    g2 = mod_ref[0, 5:6, :]
    for c in range(ROW_SLAB):
        ra = jnp.zeros((tm, LANES), F32)
        rb = jnp.zeros((tm, LANES), F32)
        for j in range(TOP_K):
            ya, yb = _unpack_pair(_load_slab_chunk(gbuf.at[j], c, tm))
            wj = wgt[:, j:j + 1]
            ra = ra + wj * ya.astype(F32)
            rb = rb + wj * yb.astype(F32)
        lo = slice(c * LANES, (c + 1) * LANES)
        hi = slice(half + c * LANES, half + (c + 1) * LANES)
        o_ref[:, lo] = x1_ref[:, lo] + g2[:, lo] * (ra + shared[:, lo])
        o_ref[:, hi] = x1_ref[:, hi] + g2[:, hi] * (rb + shared[:, hi])


def _combine(dest_blocks, x1, h2b, wgt_t, mod3, ws_gate, ws_up, ws_down, ys, seq, tm):
    t, d = x1.shape
    per_b = seq // tm
    return pl.pallas_call(
        functools.partial(_combine_kernel, tm=tm),
        grid=(t // tm,),
        in_specs=[
            pl.BlockSpec((1, 1, TOP_K * tm), lambda i: (i, 0, 0), memory_space=pltpu.SMEM),
            pl.BlockSpec((tm, d), lambda i: (i, 0)),
            pl.BlockSpec((tm, d), lambda i: (i, 0)),
            pl.BlockSpec((tm, TOP_K), lambda i: (i, 0)),
            pl.BlockSpec((1, 6, d), lambda i: (i // per_b, 0, 0)),
            pl.BlockSpec((d, EXPERT_DIM), lambda i: (0, 0)),
            pl.BlockSpec((d, EXPERT_DIM), lambda i: (0, 0)),
            pl.BlockSpec((EXPERT_DIM, d), lambda i: (0, 0)),
            pl.BlockSpec(memory_space=pl.ANY),
        ],
        out_specs=pl.BlockSpec((tm, d), lambda i: (i, 0)),
        out_shape=jax.ShapeDtypeStruct((t, d), F32),
        scratch_shapes=[pltpu.VMEM((TOP_K, tm * ROW_SLAB, LANES), U32), pltpu.SemaphoreType.DMA(())],
        compiler_params=_cparams(("arbitrary",)),
        name="combine",
    )(dest_blocks, x1, h2b, wgt_t, mod3, ws_gate, ws_up, ws_down, ys)


def _block_major(a, blk):
    k, t = a.shape
    return a.reshape(k, t // blk, blk).transpose(1, 0, 2).reshape(t // blk, 1, k * blk)


def kernel(x, c, positions, w_ada, b_ada, norm1_w, w_in, q_lat_norm_w, w_uq, kv_lat_norm_w, w_ukv, q_norm_w, k_norm_w, lb_fwd, lb_bwd, rec_norm_w, w_out, norm2_w, w_router, router_bias, w_exp_gate, w_exp_up, w_exp_down, w_sh_gate, w_sh_up, w_sh_down):
    bsz, seq, d = x.shape
    t = bsz * seq
    x2 = x.reshape(t, d)

    w_in0 = w_in[0]
    n_lat = Q_RANK + KV_RANK + QK_ROPE
    w_lat = jnp.pad(w_in0[:, :n_lat], ((0, 0), (0, LAT_PAD - n_lat))).astype(BF16)
    w_rec = w_in0[:, n_lat:].astype(BF16)
    wuq = jnp.pad(w_uq[0].reshape(Q_RANK, N_HEADS, QK_DIM), ((0, 0), (0, 0), (0, QK_PAD - QK_DIM)))
    wuq = wuq.transpose(1, 0, 2).astype(BF16)
    wukv = w_ukv[0].reshape(KV_RANK, N_HEADS, QK_NOPE + V_DIM).transpose(1, 0, 2).astype(BF16)
    qnw = jnp.pad(q_norm_w[0], (0, QK_PAD - QK_DIM)).reshape(1, QK_PAD)
    knw = jnp.pad(k_norm_w[0], (0, QK_PAD - QK_DIM)).reshape(1, QK_PAD)
    half = N_HEADS * V_DIM
    w_out_att = w_out[0, :half].astype(BF16)
    w_out_rec = w_out[0, half:].astype(BF16)
    wr_hi, wr_lo = _split_bf16(w_router[0].T)
    lbf = jnp.cumsum(jax.nn.softmax(lb_fwd.astype(F32), axis=0), axis=0)[0].reshape(1, -1)
    lbb = jnp.cumsum(jax.nn.softmax(lb_bwd.astype(F32), axis=0), axis=0)[0].reshape(1, -1)

    inv_freq = ROPE_THETA ** (-jnp.arange(0, QK_ROPE, 2, dtype=F32) / QK_ROPE)
    ang = positions.astype(F32).reshape(t, 1) * inv_freq[None, :]
    cos_h, sin_h = jnp.cos(ang), jnp.sin(ang)
    zh = jnp.zeros_like(cos_h)
    cos_t = jnp.concatenate([cos_h, cos_h, zh, zh], axis=-1)
    sin_a = jnp.concatenate([-sin_h, zh, zh, zh], axis=-1)
    sin_b = jnp.concatenate([zh, sin_h, zh, zh], axis=-1)

    mod3 = _ada(c, w_ada[0], b_ada[0]).reshape(bsz, 6, d)
    lat, rec_in = _inproj(x2, mod3, norm1_w[0], w_lat, w_rec, seq)
    q, k, v = _mla_prep(lat, cos_t, sin_a, sin_b, q_lat_norm_w[0].reshape(1, -1), wuq,
                        kv_lat_norm_w[0].reshape(1, -1), wukv, qnw, knw, bsz, seq)
    att = _attention(q, k, v)
    rec = _hgrn(rec_in, lbf, lbb, rec_norm_w[0], bsz, seq)
    x1 = _outproj(x2, att, rec, mod3, w_out_att, w_out_rec, seq)

    h2b, h2p, top_idx, top_w, rank, counts = _router(x1, mod3, norm2_w[0], wr_hi, wr_lo,
                                                     router_bias[0], seq)
    counts = counts[:, 0].astype(I32)
    padded = (counts + MOE_BLOCK - 1) // MOE_BLOCK * MOE_BLOCK
    pad_end = jnp.cumsum(padded)
    pad_start = pad_end - padded
    experts = jnp.arange(N_EXPERTS, dtype=I32)
    dest = rank + jnp.sum(jnp.where(top_idx[:, :, None] == experts, pad_start, 0), axis=-1)
    n_blocks = -(-(t * TOP_K) // MOE_BLOCK) + N_EXPERTS
    n_used = (pad_end[-1] // MOE_BLOCK).reshape(1).astype(I32)
    blk_starts = jnp.minimum(jnp.arange(n_blocks, dtype=I32) * MOE_BLOCK, pad_end[-1] - 1)
    blk_expert = jnp.sum((pad_end[None, :] <= blk_starts[:, None]).astype(I32), axis=-1)
    blk_expert = jnp.minimum(blk_expert, N_EXPERTS - 1).astype(I32)

    tc = 512
    xs = _dispatch(_block_major(dest, tc), pad_end.astype(I32), h2p, n_blocks * MOE_BLOCK, tc)
    ys = _experts(blk_expert, n_used, xs, w_exp_gate[0], w_exp_up[0], w_exp_down[0])
    tm = 256
    out = _combine(_block_major(dest, tm), x1, h2b, top_w.T, mod3, w_sh_gate[0].astype(BF16),
                   w_sh_up[0].astype(BF16), w_sh_down[0].astype(BF16), ys, seq, tm)
    return out.reshape(bsz, seq, d)
```

```python
import functools

import numpy as np
import jax
import jax.numpy as jnp
from jax import lax
from jax.experimental import pallas as pl
from jax.experimental.pallas import tpu as pltpu

F32 = jnp.float32
BF16 = jnp.bfloat16
U32 = jnp.uint32
I32 = jnp.int32

N_HEADS = 8
QK_NOPE = 128
QK_ROPE = 64
QK_DIM = QK_NOPE + QK_ROPE
QK_PAD = 256
V_DIM = 128
Q_RANK = 512
KV_RANK = 256
LAT_PAD = 896
REC_DIM = 128
REC_COLS = 5 * N_HEADS * REC_DIM
ROPE_THETA = 10000.0
N_EXPERTS = 64
TOP_K = 8
N_GROUPS = 8
TOPK_GROUPS = 4
EXPERT_DIM = 512
ROUTED_SCALE = 2.5
MOE_BLOCK = 256
EPS = 1e-6

LANES = 128
SUBLANES = 8
ROW_SLAB = 8
VMEM_LIMIT = 56 * 1024 * 1024

ATT_CHUNK = 256
HG_TILE = 128
HG_BAND = 8
HG_LEVELS = (8, 16, 32, 64)


def _cparams(sem, vmem=VMEM_LIMIT):
    return pltpu.CompilerParams(dimension_semantics=sem, vmem_limit_bytes=vmem)


def _split_bf16(a):
    hi = a.astype(BF16)
    lo = (a - hi.astype(F32)).astype(BF16)
    return hi, lo


def _pack_pair(a, b):
    ua = lax.bitcast_convert_type(a.astype(BF16).astype(F32), U32)
    ub = lax.bitcast_convert_type(b.astype(BF16).astype(F32), U32)
    return ua | (ub >> 16)


def _unpack_pair(p):
    a = lax.bitcast_convert_type(p & jnp.uint32(0xFFFF0000), F32).astype(BF16)
    b = lax.bitcast_convert_type(p << 16, F32).astype(BF16)
    return a, b


def _store_slabs(ref, packed):
    n = packed.shape[0]
    for j in range(ROW_SLAB):
        ref[pl.ds(j, n, stride=ROW_SLAB), :] = packed[:, j * LANES:(j + 1) * LANES]


def _load_slab_chunk(ref, j, n):
    return ref[pl.ds(j, n, stride=ROW_SLAB), :]


def _dot(a, b):
    return jnp.dot(a, b, preferred_element_type=F32)


def _dot_nt(a, b):
    return lax.dot_general(a, b, (((1,), (1,)), ((), ())), preferred_element_type=F32)


def _dot_tn(a, b):
    return lax.dot_general(a, b, (((0,), (0,)), ((), ())), preferred_element_type=F32)


def _ada_kernel(c_ref, w_ref, b_ref, o_ref):
    c = c_ref[...]
    cond = c * jax.nn.sigmoid(c)
    o_ref[...] = _dot(cond.astype(BF16), w_ref[...].astype(BF16)) + b_ref[...]


def _ada(c, w_ada, b_ada):
    bsz, d = c.shape
    n = w_ada.shape[1]
    tn = 1024
    return pl.pallas_call(
        _ada_kernel,
        grid=(n // tn,),
        in_specs=[
            pl.BlockSpec((bsz, d), lambda j: (0, 0)),
            pl.BlockSpec((d, tn), lambda j: (0, j)),
            pl.BlockSpec((1, tn), lambda j: (0, j)),
        ],
        out_specs=pl.BlockSpec((bsz, tn), lambda j: (0, j)),
        out_shape=jax.ShapeDtypeStruct((bsz, n), F32),
        compiler_params=_cparams(("arbitrary",)),
        name="ada",
    )(c, w_ada, b_ada.reshape(1, n))


def _modulated_norm(x, mod_ref, nw, shift_row, scale_row):
    r = lax.rsqrt(jnp.mean(x * x, axis=-1, keepdims=True) + EPS)
    scale = mod_ref[0, scale_row:scale_row + 1, :]
    shift = mod_ref[0, shift_row:shift_row + 1, :]
    return (x * r * nw) * (1.0 + scale) + shift


def _inproj_kernel(x_ref, mod_ref, nw_ref, wlat_ref, wrec_ref, lat_ref, rec_ref, h_scr):
    @pl.when(pl.program_id(1) == 0)
    def _():
        h = _modulated_norm(x_ref[...], mod_ref, nw_ref[...], 0, 1).astype(BF16)
        h_scr[...] = h
        lat_ref[...] = _dot(h, wlat_ref[...])

    rec_ref[...] = _dot(h_scr[...], wrec_ref[...])


def _inproj(x2, mod3, norm_w, w_lat, w_rec, seq):
    t, d = x2.shape
    tm, tn = 1024, 512
    per_b = seq // tm
    return pl.pallas_call(
        _inproj_kernel,
        grid=(t // tm, REC_COLS // tn),
        in_specs=[
            pl.BlockSpec((tm, d), lambda i, j: (i, 0)),
            pl.BlockSpec((1, 6, d), lambda i, j: (i // per_b, 0, 0)),
            pl.BlockSpec((1, d), lambda i, j: (0, 0)),
            pl.BlockSpec((d, LAT_PAD), lambda i, j: (0, 0)),
            pl.BlockSpec((d, tn), lambda i, j: (0, j)),
        ],
        out_specs=[
            pl.BlockSpec((tm, LAT_PAD), lambda i, j: (i, 0)),
            pl.BlockSpec((tm, tn), lambda i, j: (i, j)),
        ],
        out_shape=[
            jax.ShapeDtypeStruct((t, LAT_PAD), F32),
            jax.ShapeDtypeStruct((t, REC_COLS), F32),
        ],
        scratch_shapes=[pltpu.VMEM((tm, d), BF16)],
        compiler_params=_cparams(("parallel", "arbitrary")),
        name="inproj",
    )(x2, mod3, norm_w.reshape(1, d), w_lat, w_rec)


def _rope(x2, cos, sin_a, sin_b):
    return (x2 * cos + pltpu.roll(x2, LANES - QK_ROPE // 2, 1) * sin_a
            + pltpu.roll(x2, QK_ROPE // 2, 1) * sin_b)


def _mla_prep_kernel(lat_ref, cos_ref, sa_ref, sb_ref, qlw_ref, wuq_ref, kvlw_ref, wukv_ref,
                     qnw_ref, knw_ref, q_ref, k_ref, v_ref):
    lat = lat_ref[...]
    ql = lat[:, :Q_RANK]
    kvl = lat[:, Q_RANK:Q_RANK + KV_RANK]
    kr = lat[:, Q_RANK + KV_RANK:]
    qn = (ql * lax.rsqrt(jnp.mean(ql * ql, -1, keepdims=True) + EPS) * qlw_ref[...]).astype(BF16)
    kvn = (kvl * lax.rsqrt(jnp.mean(kvl * kvl, -1, keepdims=True) + EPS) * kvlw_ref[...]).astype(BF16)
    cos, sa, sb = cos_ref[...], sa_ref[...], sb_ref[...]
    qnw = qnw_ref[...]
    knw = knw_ref[...]
    kr_ss = jnp.sum(kr * kr, -1, keepdims=True)
    kr_roped = _rope(kr * knw[:, QK_NOPE:], cos, sa, sb)
    scale = QK_DIM ** -0.5
    for h in range(N_HEADS):
        qh = _dot(qn, wuq_ref[h])
        r = lax.rsqrt(jnp.sum(qh * qh, -1, keepdims=True) * (1.0 / QK_DIM) + EPS) * scale
        qh = qh * r * qnw
        q_ref[0, h, :, :QK_NOPE] = qh[:, :QK_NOPE].astype(BF16)
        q_ref[0, h, :, QK_NOPE:] = _rope(qh[:, QK_NOPE:], cos, sa, sb).astype(BF16)
        kvh = _dot(kvn, wukv_ref[h])
        kn = kvh[:, :QK_NOPE]
        rk = lax.rsqrt((jnp.sum(kn * kn, -1, keepdims=True) + kr_ss) * (1.0 / QK_DIM) + EPS)
        k_ref[0, h, :, :QK_NOPE] = (kn * rk * knw[:, :QK_NOPE]).astype(BF16)
        k_ref[0, h, :, QK_NOPE:] = (kr_roped * rk).astype(BF16)
        v_ref[0, h] = kvh[:, QK_NOPE:].astype(BF16)


def _mla_prep(lat, cos, sa, sb, qlw, wuq, kvlw, wukv, qnw, knw, bsz, seq):
    tm = 512
    per_b = seq // tm
    const2 = lambda i: (0, 0)
    const3 = lambda i: (0, 0, 0)
    head_out = lambda w: pl.BlockSpec((1, N_HEADS, tm, w), lambda i: (i // per_b, 0, i % per_b, 0))
    return pl.pallas_call(
        _mla_prep_kernel,
        grid=(bsz * seq // tm,),
        in_specs=[
            pl.BlockSpec((tm, LAT_PAD), lambda i: (i, 0)),
            pl.BlockSpec((tm, LANES), lambda i: (i, 0)),
            pl.BlockSpec((tm, LANES), lambda i: (i, 0)),
            pl.BlockSpec((tm, LANES), lambda i: (i, 0)),
            pl.BlockSpec((1, Q_RANK), const2),
            pl.BlockSpec((N_HEADS, Q_RANK, QK_PAD), const3),
            pl.BlockSpec((1, KV_RANK), const2),
            pl.BlockSpec((N_HEADS, KV_RANK, QK_PAD), const3),
            pl.BlockSpec((1, QK_PAD), const2),
            pl.BlockSpec((1, QK_PAD), const2),
        ],
        out_specs=[head_out(QK_PAD), head_out(QK_PAD), head_out(V_DIM)],
        out_shape=[
            jax.ShapeDtypeStruct((bsz, N_HEADS, seq, QK_PAD), BF16),
            jax.ShapeDtypeStruct((bsz, N_HEADS, seq, QK_PAD), BF16),
            jax.ShapeDtypeStruct((bsz, N_HEADS, seq, V_DIM), BF16),
        ],
        compiler_params=_cparams(("parallel",)),
        name="mla_prep",
    )(lat, cos, sa, sb, qlw, wuq, kvlw, wukv, qnw, knw)


def _attn_kernel(q_ref, k_ref, v_ref, o_ref):
    k = k_ref[0, 0]
    v = v_ref[0, 0]
    for c in range(q_ref.shape[2] // ATT_CHUNK):
        rows = pl.ds(c * ATT_CHUNK, ATT_CHUNK)
        s = _dot_nt(q_ref[0, 0, rows, :], k)
        m = jnp.max(s, axis=-1, keepdims=True)
        p = jnp.exp(s - m)
        l = jnp.sum(p, axis=-1, keepdims=True)
        o_ref[rows, :] = _dot(p.astype(BF16), v) / l


def _attention(q, k, v):
    bsz, nh, seq, _ = q.shape
    tq = 1024
    nq = seq // tq
    return pl.pallas_call(
        _attn_kernel,
        grid=(bsz, nh, nq),
        in_specs=[
            pl.BlockSpec((1, 1, tq, QK_PAD), lambda b, h, i: (b, h, i, 0)),
            pl.BlockSpec((1, 1, seq, QK_PAD), lambda b, h, i: (b, h, 0, 0)),
            pl.BlockSpec((1, 1, seq, V_DIM), lambda b, h, i: (b, h, 0, 0)),
        ],
        out_specs=pl.BlockSpec((tq, V_DIM), lambda b, h, i: (b * nq + i, h)),
        out_shape=jax.ShapeDtypeStruct((bsz * seq, nh * V_DIM), F32),
        compiler_params=_cparams(("parallel", "parallel", "arbitrary")),
        name="attention",
    )(q, k, v)


def _hgrn_constants():
    n = HG_TILE
    t = np.arange(n)[:, None]
    u = np.arange(n)[None, :]
    tri = np.concatenate([(u <= t), (u >= t)], axis=0).astype(np.float32)
    lvl = np.zeros((n, n), np.float32)
    for li, m in enumerate(HG_LEVELS):
        lvl[((t // (2 * m)) == (u // (2 * m))) & ((t // m) != (u // m))] = li + 1
    ones2 = np.zeros((2 * LANES, 2 * LANES), np.float32)
    ones2[:LANES, :LANES] = 1.0
    ones2[LANES:, LANES:] = 1.0
    return tri, lvl, ones2


def _level_stacks(m, q, kkf, kkb, bf, cb):
    zero = jnp.zeros((m, LANES), F32)
    q_rows, k_rows = [], []
    for a in range(0, HG_TILE, 2 * m):
        e, o = slice(a, a + m), slice(a + m, a + 2 * m)
        ref_f = bf[a + m - 1:a + m, :]
        ref_b = cb[a + m:a + m + 1, :]
        wf_e = jnp.exp(ref_f - bf[e])
        wf_o = jnp.exp(bf[o] - ref_f)
        wb_e = jnp.exp(cb[e] - ref_b)
        wb_o = jnp.exp(ref_b - cb[o])
        q_rows.append(jnp.concatenate([zero, q[e] * wb_e], axis=1))
        q_rows.append(jnp.concatenate([q[o] * wf_o, zero], axis=1))
        k_rows.append(jnp.concatenate([kkf[e] * wf_e, zero], axis=1))
        k_rows.append(jnp.concatenate([zero, kkb[o] * wb_o], axis=1))
    return (jnp.concatenate(q_rows, axis=0).astype(BF16), jnp.concatenate(k_rows, axis=0).astype(BF16))


def _hgrn_kernel(q_ref, zf_ref, zb_ref, v_ref, gate_ref, lbf_ref, lbb_ref, nw_ref,
                 tri_ref, lvl_ref, ones_ref, o_ref,
                 ff_scr, fb_scr, kf_scr, kb_scr, v_scr, gf_scr, gb_scr, o_scr,
                 qe_scr, u_scr, dec_scr, st_scr):
    seq = q_ref.shape[0]
    n_tiles = seq // HG_TILE
    pad = HG_BAND
    zeros_pad = jnp.zeros((pad, LANES), F32)
    for scr in (ff_scr, fb_scr, kf_scr, kb_scr, v_scr):
        scr[pl.ds(0, pad), :] = zeros_pad
        scr[pl.ds(pad + seq, pad), :] = zeros_pad
    v_scr[pl.ds(pad, seq), :] = v_ref[...]
    pos = lax.broadcasted_iota(I32, (seq, 1), 0) % HG_BAND

    lbf = lbf_ref[...]
    f = lbf + (1.0 - lbf) * jax.nn.sigmoid(zf_ref[...])
    gf_scr[...] = jnp.log(f)
    kf_scr[pl.ds(pad, seq), :] = 1.0 - f
    ff_scr[pl.ds(pad, seq), :] = jnp.where(pos == 0, 0.0, f)
    lbb = lbb_ref[...]
    f = lbb + (1.0 - lbb) * jax.nn.sigmoid(zb_ref[...])
    gb_scr[...] = jnp.log(f)
    kb_scr[pl.ds(pad, seq), :] = 1.0 - f
    fb_scr[pl.ds(pad, seq), :] = jnp.where(pos == HG_BAND - 1, 0.0, f)

    def tile_body(i, carry):
        t0 = pl.multiple_of(i * HG_TILE, HG_TILE)
        q = q_ref[pl.ds(t0, HG_TILE), :]
        v = v_scr[pl.ds(pad + t0, HG_TILE), :]
        kkf = kf_scr[pl.ds(pad + t0, HG_TILE), :]
        kkb = kb_scr[pl.ds(pad + t0, HG_TILE), :]
        v_b = v.astype(BF16)
        gf_hi, gf_lo = _split_bf16(gf_scr[pl.ds(t0, HG_TILE), :])
        gb_hi, gb_lo = _split_bf16(gb_scr[pl.ds(t0, HG_TILE), :])
        pf = _dot(tri_ref[pl.ds(0, HG_TILE), :], jnp.concatenate([gf_hi, gf_lo], axis=1))
        pb = _dot(tri_ref[pl.ds(HG_TILE, HG_TILE), :], jnp.concatenate([gb_hi, gb_lo], axis=1))
        bf = pf[:, :LANES] + pf[:, LANES:]
        cb = pb[:, :LANES] + pb[:, LANES:]

        last_f = bf[HG_TILE - 1:HG_TILE, :]
        first_b = cb[0:1, :]
        wq_f = jnp.exp(bf)
        wq_b = jnp.exp(cb)
        ke = jnp.concatenate([kkf * jnp.exp(last_f - bf), kkb * jnp.exp(first_b - cb)], axis=1)
        u_scr[i] = _dot_tn(v_b, ke.astype(BF16))
        qe_scr[i] = jnp.concatenate([q * wq_f, q * wq_b], axis=1).astype(BF16)
        dec_scr[pl.ds(i, 1), :] = jnp.concatenate([wq_f[HG_TILE - 1:HG_TILE, :], wq_b[0:1, :]], axis=1)

        lvl = lvl_ref[...]
        a = jnp.zeros((HG_TILE, HG_TILE), F32)
        for li, m in enumerate(HG_LEVELS):
            qs, ks = _level_stacks(m, q, kkf, kkb, bf, cb)
            a = jnp.where(lvl == float(li + 1), _dot_nt(qs, ks), a)
        o = _dot(a.astype(BF16), v_b)

        ones2 = ones_ref[...]
        pf_run = None
        pb_run = None
        for d in range(HG_BAND):
            if d == 1:
                pf_run = ff_scr[pl.ds(pad + t0, HG_TILE), :]
                pb_run = fb_scr[pl.ds(pad + t0, HG_TILE), :]
            elif d > 1:
                pf_run = pf_run * ff_scr[pl.ds(pad + t0 - (d - 1), HG_TILE), :]
                pb_run = pb_run * fb_scr[pl.ds(pad + t0 + (d - 1), HG_TILE), :]
            p_f = q * kf_scr[pl.ds(pad + t0 - d, HG_TILE), :]
            p_b = q * kb_scr[pl.ds(pad + t0 + d, HG_TILE), :]
            if d > 0:
                p_f = p_f * pf_run
                p_b = p_b * pb_run
            rs = _dot(jnp.concatenate([p_f, p_b], axis=1).astype(BF16), ones2)
            o = (o + rs[:, :LANES] * v_scr[pl.ds(pad + t0 - d, HG_TILE), :]
                 + rs[:, LANES:] * v_scr[pl.ds(pad + t0 + d, HG_TILE), :])
        o_scr[pl.ds(t0, HG_TILE), :] = o
        return carry

    lax.fori_loop(0, n_tiles, tile_body, 0, unroll=2)

    def scan_f(i, state):
        st_scr[i] = state.astype(BF16)
        return state * dec_scr[pl.ds(i, 1), :][:, :LANES] + u_scr[i][:, :LANES]

    lax.fori_loop(0, n_tiles, scan_f, jnp.zeros((V_DIM, REC_DIM), F32))

    def scan_b(n, state):
        i = n_tiles - 1 - n
        t0 = pl.multiple_of(i * HG_TILE, HG_TILE)
        both = jnp.concatenate([st_scr[i], state.astype(BF16)], axis=1)
        o_scr[pl.ds(t0, HG_TILE), :] = o_scr[pl.ds(t0, HG_TILE), :] + _dot_nt(qe_scr[i], both)
        return state * dec_scr[pl.ds(i, 1), :][:, LANES:] + u_scr[i][:, LANES:]

    lax.fori_loop(0, n_tiles, scan_b, jnp.zeros((V_DIM, REC_DIM), F32), unroll=4)

    o = o_scr[...]
    o = o * lax.rsqrt(jnp.mean(o * o, -1, keepdims=True) + EPS) * nw_ref[...]
    gate = gate_ref[...]
    o_ref[...] = o * (gate * jax.nn.sigmoid(gate))


def _hgrn(rec, lbf, lbb, rec_norm_w, bsz, seq):
    tri, lvl, ones2 = _hgrn_constants()
    n_tiles = seq // HG_TILE
    col = lambda c: pl.BlockSpec((seq, REC_DIM), lambda b, h: (b, c * N_HEADS + h))
    vec = pl.BlockSpec((1, REC_DIM), lambda b, h: (0, h))
    const = lambda r, c: pl.BlockSpec((r, c), lambda b, h: (0, 0))
    padded = pltpu.VMEM((seq + 2 * HG_BAND, LANES), F32)
    plain = pltpu.VMEM((seq, LANES), F32)
    return pl.pallas_call(
        _hgrn_kernel,
        grid=(bsz, N_HEADS),
        in_specs=[col(0), col(1), col(2), col(3), col(4), vec, vec, const(1, REC_DIM),
                  const(2 * HG_TILE, HG_TILE), const(HG_TILE, HG_TILE), const(2 * LANES, 2 * LANES)],
        out_specs=pl.BlockSpec((seq, REC_DIM), lambda b, h: (b, h)),
        out_shape=jax.ShapeDtypeStruct((bsz * seq, N_HEADS * REC_DIM), F32),
        scratch_shapes=[padded, padded, padded, padded, padded, plain, plain, plain,
                        pltpu.VMEM((n_tiles, HG_TILE, 2 * LANES), BF16),
                        pltpu.VMEM((n_tiles, V_DIM, 2 * LANES), F32),
                        pltpu.VMEM((n_tiles, 2 * LANES), F32),
                        pltpu.VMEM((n_tiles, V_DIM, LANES), BF16)],
        compiler_params=_cparams(("parallel", "parallel")),
        name="hgrn",
    )(rec, rec, rec, rec, rec, lbf, lbb, rec_norm_w.reshape(1, REC_DIM),
      jnp.asarray(tri, BF16), jnp.asarray(lvl, F32), jnp.asarray(ones2, BF16))


def _outproj_kernel(x_ref, att_ref, rec_ref, mod_ref, wa_ref, wr_ref, o_ref):
    y = _dot(att_ref[...].astype(BF16), wa_ref[...]) + _dot(rec_ref[...].astype(BF16), wr_ref[...])
    o_ref[...] = x_ref[...] + mod_ref[0, 2:3, :] * y


def _outproj(x2, att, rec, mod3, w_att, w_rec, seq):
    t, d = x2.shape
    tm = 512
    per_b = seq // tm
    half = att.shape[1]
    return pl.pallas_call(
        _outproj_kernel,
        grid=(t // tm,),
        in_specs=[
            pl.BlockSpec((tm, d), lambda i: (i, 0)),
            pl.BlockSpec((tm, half), lambda i: (i, 0)),
            pl.BlockSpec((tm, half), lambda i: (i, 0)),
            pl.BlockSpec((1, 6, d), lambda i: (i // per_b, 0, 0)),
            pl.BlockSpec((half, d), lambda i: (0, 0)),
            pl.BlockSpec((half, d), lambda i: (0, 0)),
        ],
        out_specs=pl.BlockSpec((tm, d), lambda i: (i, 0)),
        out_shape=jax.ShapeDtypeStruct((t, d), F32),
        compiler_params=_cparams(("parallel",)),
        name="outproj",
    )(x2, att, rec, mod3, w_att, w_rec)


def _router_kernel(x_ref, mod_ref, nw_ref, wrh_ref, wrl_ref, bias_ref, tri_ref,
                   h2b_ref, h2p_ref, idx_ref, wgt_ref, rank_ref, cnt_ref, base_scr):
    i = pl.program_id(0)
    tm = x_ref.shape[0]
    half = x_ref.shape[1] // 2

    @pl.when(i == 0)
    def _():
        base_scr[...] = jnp.zeros_like(base_scr)

    h2 = _modulated_norm(x_ref[...], mod_ref, nw_ref[...], 3, 4)
    h_hi, h_lo = _split_bf16(h2)
    h2b_ref[...] = h_hi
    _store_slabs(h2p_ref, _pack_pair(h2[:, :half], h2[:, half:]))
    logits = _dot_nt(wrh_ref[...], h_hi) + _dot_nt(wrh_ref[...], h_lo) + _dot_nt(wrl_ref[...], h_hi)
    scores = jax.nn.sigmoid(logits)
    sel = scores + bias_ref[...]

    neg = -jnp.inf
    per_grp = N_EXPERTS // N_GROUPS
    sel3 = sel.reshape(N_GROUPS, per_grp, tm)
    within = lax.broadcasted_iota(I32, sel3.shape, 1)
    m1 = jnp.max(sel3, axis=1, keepdims=True)
    first = jnp.min(jnp.where(sel3 == m1, within, per_grp), axis=1, keepdims=True)
    m2 = jnp.max(jnp.where(within == first, neg, sel3), axis=1, keepdims=True)
    gs = (m1 + m2).reshape(N_GROUPS, tm)
    gidx = lax.broadcasted_iota(I32, gs.shape, 0)
    beaten = jnp.zeros(gs.shape, I32)
    for g2 in range(N_GROUPS):
        other = gs[g2:g2 + 1, :]
        beaten = beaten + ((other > gs) | ((other == gs) & (g2 < gidx))).astype(I32)
    keep = (beaten < TOPK_GROUPS).reshape(N_GROUPS, 1, tm)
    cur = jnp.where(keep, sel3, neg).reshape(N_EXPERTS, tm)

    eidx = lax.broadcasted_iota(I32, (N_EXPERTS, tm), 0)
    chosen = jnp.zeros((N_EXPERTS, tm), F32)
    picks, pick_scores = [], []
    for _ in range(TOP_K):
        m = jnp.max(cur, axis=0, keepdims=True)
        pick = jnp.min(jnp.where(cur == m, eidx, N_EXPERTS), axis=0, keepdims=True)
        hit = eidx == pick
        picks.append(pick)
        pick_scores.append(jnp.sum(jnp.where(hit, scores, 0.0), axis=0, keepdims=True))
        chosen = jnp.where(hit, 1.0, chosen)
        cur = jnp.where(hit, neg, cur)

    before = _dot(chosen.astype(BF16), tri_ref[...]) + base_scr[...][:, 0:1]
    denom = pick_scores[0]
    for s in pick_scores[1:]:
        denom = denom + s
    for j in range(TOP_K):
        idx_ref[j:j + 1, :] = picks[j]
        wgt_ref[j:j + 1, :] = pick_scores[j] / denom * ROUTED_SCALE
        rank_ref[j:j + 1, :] = jnp.sum(jnp.where(eidx == picks[j], before, 0.0), axis=0,
                                       keepdims=True).astype(I32)
    base_scr[...] = base_scr[...] + jnp.sum(chosen, axis=1, keepdims=True)
    cnt_ref[...] = base_scr[...]


def _router(x1, mod3, norm_w, wr_hi, wr_lo, bias, seq):
    t, d = x1.shape
    tm = 512
    per_b = seq // tm
    tri = jnp.asarray(np.triu(np.ones((tm, tm), np.float32), 1), BF16)
    row_out = lambda dt: jax.ShapeDtypeStruct((TOP_K, t), dt)
    row_spec = pl.BlockSpec((TOP_K, tm), lambda i: (0, i))
    return pl.pallas_call(
        _router_kernel,
        grid=(t // tm,),
        in_specs=[
            pl.BlockSpec((tm, d), lambda i: (i, 0)),
            pl.BlockSpec((1, 6, d), lambda i: (i // per_b, 0, 0)),
            pl.BlockSpec((1, d), lambda i: (0, 0)),
            pl.BlockSpec((N_EXPERTS, d), lambda i: (0, 0)),
            pl.BlockSpec((N_EXPERTS, d), lambda i: (0, 0)),
            pl.BlockSpec((N_EXPERTS, 1), lambda i: (0, 0)),
            pl.BlockSpec((tm, tm), lambda i: (0, 0)),
        ],
        out_specs=[
            pl.BlockSpec((tm, d), lambda i: (i, 0)),
            pl.BlockSpec((tm * ROW_SLAB, LANES), lambda i: (i, 0)),
            row_spec, row_spec, row_spec,
            pl.BlockSpec((N_EXPERTS, LANES), lambda i: (0, 0)),
        ],
        out_shape=[
            jax.ShapeDtypeStruct((t, d), BF16),
            jax.ShapeDtypeStruct((t * ROW_SLAB, LANES), U32),
            row_out(I32), row_out(F32), row_out(I32),
            jax.ShapeDtypeStruct((N_EXPERTS, LANES), F32),
        ],
        scratch_shapes=[pltpu.VMEM((N_EXPERTS, LANES), F32)],
        compiler_params=_cparams(("arbitrary",)),
        name="router",
    )(x1, mod3, norm_w.reshape(1, d), wr_hi, wr_lo, bias.reshape(N_EXPERTS, 1), tri)


def _dispatch_kernel(dest_ref, pad_end_ref, h2p_ref, xs_ref, zero_scr, sem, *, tc):
    i = pl.program_id(0)

    @pl.when(i == 0)
    def _():
        zero_scr[...] = jnp.zeros_like(zero_scr)

        def fill(e, prev_end):
            end = pad_end_ref[e]

            @pl.when(end > prev_end)
            def _():
                start = pl.multiple_of((end - MOE_BLOCK) * ROW_SLAB, MOE_BLOCK * ROW_SLAB)
                cp = pltpu.make_async_copy(zero_scr, xs_ref.at[pl.ds(start, MOE_BLOCK * ROW_SLAB)], sem)
                cp.start()
                cp.wait()
            return end

        lax.fori_loop(0, N_EXPERTS, fill, jnp.int32(0))

        def fill_unused(blk, carry):
            start = pl.multiple_of(blk * (MOE_BLOCK * ROW_SLAB), MOE_BLOCK * ROW_SLAB)
            cp = pltpu.make_async_copy(zero_scr, xs_ref.at[pl.ds(start, MOE_BLOCK * ROW_SLAB)], sem)
            cp.start()
            cp.wait()
            return carry

        n_blocks = xs_ref.shape[0] // (MOE_BLOCK * ROW_SLAB)
        lax.fori_loop(pad_end_ref[N_EXPERTS - 1] // MOE_BLOCK, n_blocks, fill_unused, 0)

    def issue(t, carry):
        src = h2p_ref.at[pl.ds(pl.multiple_of(t * ROW_SLAB, ROW_SLAB), ROW_SLAB)]
        for j in range(TOP_K):
            d = pl.multiple_of(dest_ref[0, 0, j * tc + t] * ROW_SLAB, ROW_SLAB)
            pltpu.make_async_copy(src, xs_ref.at[pl.ds(d, ROW_SLAB)], sem).start()
        return carry

    lax.fori_loop(0, tc, issue, 0)
    for _ in range(TOP_K):
        pltpu.make_async_copy(h2p_ref, xs_ref.at[pl.ds(0, tc * ROW_SLAB)], sem).wait()


def _dispatch(dest_blocks, pad_end, h2p, n_rows, tc):
    t, w = h2p.shape[0] // ROW_SLAB, LANES
    n_rows = n_rows * ROW_SLAB
    return pl.pallas_call(
        functools.partial(_dispatch_kernel, tc=tc),
        grid=(t // tc,),
        in_specs=[
            pl.BlockSpec((1, 1, TOP_K * tc), lambda i: (i, 0, 0), memory_space=pltpu.SMEM),
            pl.BlockSpec(memory_space=pltpu.SMEM),
            pl.BlockSpec((tc * ROW_SLAB, w), lambda i: (i, 0)),
        ],
        out_specs=pl.BlockSpec(memory_space=pl.ANY),
        out_shape=jax.ShapeDtypeStruct((n_rows, w), U32),
        scratch_shapes=[pltpu.VMEM((MOE_BLOCK * ROW_SLAB, w), U32), pltpu.SemaphoreType.DMA(())],
        compiler_params=_cparams(("arbitrary",)),
        name="dispatch",
    )(dest_blocks, pad_end, h2p)


def _expert_kernel(be_ref, nused_ref, xs_ref, wg_ref, wu_ref, wd_ref, ys_ref,
                   wg_scr, wu_scr, wd_scr, x_scr):
    b = pl.program_id(0)
    e = be_ref[b]
    prev = be_ref[jnp.maximum(b - 1, 0)]
    half = ROW_SLAB * LANES

    @pl.when((b == 0) | (e != prev))
    def _():
        wg_scr[...] = wg_ref[...].astype(BF16)
        wu_scr[...] = wu_ref[...].astype(BF16)
        wd_scr[...] = wd_ref[...].astype(BF16)

    @pl.when(b < nused_ref[0])
    def _():
        for j in range(ROW_SLAB):
            xa, xb = _unpack_pair(_load_slab_chunk(xs_ref, j, MOE_BLOCK))
            x_scr[:, j * LANES:(j + 1) * LANES] = xa
            x_scr[:, half + j * LANES:half + (j + 1) * LANES] = xb
        xfull = x_scr[...]
        gate = _dot(xfull, wg_scr[...])
        up = _dot(xfull, wu_scr[...])
        mid = ((gate * jax.nn.sigmoid(gate)) * up).astype(BF16)
        y = _dot(mid, wd_scr[...])
        _store_slabs(ys_ref, _pack_pair(y[:, :half], y[:, half:]))

    @pl.when(b >= nused_ref[0])
    def _():
        ys_ref[...] = jnp.zeros_like(ys_ref)


def _experts(blk_expert, n_used, xs, w_gate, w_up, w_down):
    n_blocks = xs.shape[0] // (MOE_BLOCK * ROW_SLAB)
    d = 2 * ROW_SLAB * LANES
    last = lambda b, be, nu: jnp.minimum(b, jnp.maximum(nu[0] - 1, 0))
    return pl.pallas_call(
        _expert_kernel,
        grid_spec=pltpu.PrefetchScalarGridSpec(
            num_scalar_prefetch=2,
            grid=(n_blocks,),
            in_specs=[
                pl.BlockSpec((MOE_BLOCK * ROW_SLAB, LANES), lambda b, be, nu: (last(b, be, nu), 0)),
                pl.BlockSpec((None, d, EXPERT_DIM), lambda b, be, nu: (be[b], 0, 0)),
                pl.BlockSpec((None, d, EXPERT_DIM), lambda b, be, nu: (be[b], 0, 0)),
                pl.BlockSpec((None, EXPERT_DIM, d), lambda b, be, nu: (be[b], 0, 0)),
            ],
            out_specs=pl.BlockSpec((MOE_BLOCK * ROW_SLAB, LANES), lambda b, be, nu: (b, 0)),
            scratch_shapes=[pltpu.VMEM((d, EXPERT_DIM), BF16), pltpu.VMEM((d, EXPERT_DIM), BF16),
                            pltpu.VMEM((EXPERT_DIM, d), BF16), pltpu.VMEM((MOE_BLOCK, d), BF16)],
        ),
        out_shape=jax.ShapeDtypeStruct(xs.shape, U32),
        compiler_params=_cparams(("arbitrary",)),
        name="experts",
    )(blk_expert, n_used, xs, w_gate, w_up, w_down)


def _combine_kernel(dest_ref, x1_ref, h2b_ref, wgt_ref, mod_ref, wsg_ref, wsu_ref, wsd_ref, ys_ref,
                    o_ref, gbuf, sem, *, tm):
    def issue(t, carry):
        row = pl.multiple_of(t * ROW_SLAB, ROW_SLAB)
        for j in range(TOP_K):
            d = pl.multiple_of(dest_ref[0, 0, j * tm + t] * ROW_SLAB, ROW_SLAB)
            pltpu.make_async_copy(ys_ref.at[pl.ds(d, ROW_SLAB)], gbuf.at[j, pl.ds(row, ROW_SLAB)],
                                  sem).start()
        return carry

    lax.fori_loop(0, tm, issue, 0)

    h2 = h2b_ref[...]
    gate = _dot(h2, wsg_ref[...])
    up = _dot(h2, wsu_ref[...])
    shared = _dot(((gate * jax.nn.sigmoid(gate)) * up).astype(BF16), wsd_ref[...])

    for j in range(TOP_K):
        pltpu.make_async_copy(ys_ref.at[pl.ds(0, tm * ROW_SLAB)], gbuf.at[j], sem).wait()

    half = ROW_SLAB * LANES
    wgt = wgt_ref[...]
    g2 = mod_ref[0, 5:6, :]
    for c in range(ROW_SLAB):
        ra = jnp.zeros((tm, LANES), F32)
        rb = jnp.zeros((tm, LANES), F32)
        for j in range(TOP_K):
            ya, yb = _unpack_pair(_load_slab_chunk(gbuf.at[j], c, tm))
            wj = wgt[:, j:j + 1]
            ra = ra + wj * ya.astype(F32)
            rb = rb + wj * yb.astype(F32)
        lo = slice(c * LANES, (c + 1) * LANES)
        hi = slice(half + c * LANES, half + (c + 1) * LANES)
        o_ref[:, lo] = x1_ref[:, lo] + g2[:, lo] * (ra + shared[:, lo])
        o_ref[:, hi] = x1_ref[:, hi] + g2[:, hi] * (rb + shared[:, hi])


def _combine(dest_blocks, x1, h2b, wgt_t, mod3, ws_gate, ws_up, ws_down, ys, seq, tm):
    t, d = x1.shape
    per_b = seq // tm
    return pl.pallas_call(
        functools.partial(_combine_kernel, tm=tm),
        grid=(t // tm,),
        in_specs=[
            pl.BlockSpec((1, 1, TOP_K * tm), lambda i: (i, 0, 0), memory_space=pltpu.SMEM),
            pl.BlockSpec((tm, d), lambda i: (i, 0)),
            pl.BlockSpec((tm, d), lambda i: (i, 0)),
            pl.BlockSpec((tm, TOP_K), lambda i: (i, 0)),
            pl.BlockSpec((1, 6, d), lambda i: (i // per_b, 0, 0)),
            pl.BlockSpec((d, EXPERT_DIM), lambda i: (0, 0)),
            pl.BlockSpec((d, EXPERT_DIM), lambda i: (0, 0)),
            pl.BlockSpec((EXPERT_DIM, d), lambda i: (0, 0)),
            pl.BlockSpec(memory_space=pl.ANY),
        ],
        out_specs=pl.BlockSpec((tm, d), lambda i: (i, 0)),
        out_shape=jax.ShapeDtypeStruct((t, d), F32),
        scratch_shapes=[pltpu.VMEM((TOP_K, tm * ROW_SLAB, LANES), U32), pltpu.SemaphoreType.DMA(())],
        compiler_params=_cparams(("arbitrary",)),
        name="combine",
    )(dest_blocks, x1, h2b, wgt_t, mod3, ws_gate, ws_up, ws_down, ys)


def _block_major(a, blk):
    k, t = a.shape
    return a.reshape(k, t // blk, blk).transpose(1, 0, 2).reshape(t // blk, 1, k * blk)


def kernel(x, c, positions, w_ada, b_ada, norm1_w, w_in, q_lat_norm_w, w_uq, kv_lat_norm_w, w_ukv, q_norm_w, k_norm_w, lb_fwd, lb_bwd, rec_norm_w, w_out, norm2_w, w_router, router_bias, w_exp_gate, w_exp_up, w_exp_down, w_sh_gate, w_sh_up, w_sh_down):
    bsz, seq, d = x.shape
    t = bsz * seq
    x2 = x.reshape(t, d)

    w_in0 = w_in[0]
    n_lat = Q_RANK + KV_RANK + QK_ROPE
    w_lat = jnp.pad(w_in0[:, :n_lat], ((0, 0), (0, LAT_PAD - n_lat))).astype(BF16)
    w_rec = w_in0[:, n_lat:].astype(BF16)
    wuq = jnp.pad(w_uq[0].reshape(Q_RANK, N_HEADS, QK_DIM), ((0, 0), (0, 0), (0, QK_PAD - QK_DIM)))
    wuq = wuq.transpose(1, 0, 2).astype(BF16)
    wukv = w_ukv[0].reshape(KV_RANK, N_HEADS, QK_NOPE + V_DIM).transpose(1, 0, 2).astype(BF16)
    qnw = jnp.pad(q_norm_w[0], (0, QK_PAD - QK_DIM)).reshape(1, QK_PAD)
    knw = jnp.pad(k_norm_w[0], (0, QK_PAD - QK_DIM)).reshape(1, QK_PAD)
    half = N_HEADS * V_DIM
    w_out_att = w_out[0, :half].astype(BF16)
    w_out_rec = w_out[0, half:].astype(BF16)
    wr_hi, wr_lo = _split_bf16(w_router[0].T)
    lbf = jnp.cumsum(jax.nn.softmax(lb_fwd.astype(F32), axis=0), axis=0)[0].reshape(1, -1)
    lbb = jnp.cumsum(jax.nn.softmax(lb_bwd.astype(F32), axis=0), axis=0)[0].reshape(1, -1)

    inv_freq = ROPE_THETA ** (-jnp.arange(0, QK_ROPE, 2, dtype=F32) / QK_ROPE)
    ang = positions.astype(F32).reshape(t, 1) * inv_freq[None, :]
    cos_h, sin_h = jnp.cos(ang), jnp.sin(ang)
    zh = jnp.zeros_like(cos_h)
    cos_t = jnp.concatenate([cos_h, cos_h, zh, zh], axis=-1)
    sin_a = jnp.concatenate([-sin_h, zh, zh, zh], axis=-1)
    sin_b = jnp.concatenate([zh, sin_h, zh, zh], axis=-1)

    mod3 = _ada(c, w_ada[0], b_ada[0]).reshape(bsz, 6, d)
    lat, rec_in = _inproj(x2, mod3, norm1_w[0], w_lat, w_rec, seq)
    q, k, v = _mla_prep(lat, cos_t, sin_a, sin_b, q_lat_norm_w[0].reshape(1, -1), wuq,
                        kv_lat_norm_w[0].reshape(1, -1), wukv, qnw, knw, bsz, seq)
    att = _attention(q, k, v)
    rec = _hgrn(rec_in, lbf, lbb, rec_norm_w[0], bsz, seq)
    x1 = _outproj(x2, att, rec, mod3, w_out_att, w_out_rec, seq)

    h2b, h2p, top_idx, top_w, rank, counts = _router(x1, mod3, norm2_w[0], wr_hi, wr_lo,
                                                     router_bias[0], seq)
    counts = counts[:, 0].astype(I32)
    padded = (counts + MOE_BLOCK - 1) // MOE_BLOCK * MOE_BLOCK
    pad_end = jnp.cumsum(padded)
    pad_start = pad_end - padded
    experts = jnp.arange(N_EXPERTS, dtype=I32)
    dest = rank + jnp.sum(jnp.where(top_idx[:, :, None] == experts, pad_start, 0), axis=-1)
    n_blocks = -(-(t * TOP_K) // MOE_BLOCK) + N_EXPERTS
    n_used = (pad_end[-1] // MOE_BLOCK).reshape(1).astype(I32)
    blk_starts = jnp.minimum(jnp.arange(n_blocks, dtype=I32) * MOE_BLOCK, pad_end[-1] - 1)
    blk_expert = jnp.sum((pad_end[None, :] <= blk_starts[:, None]).astype(I32), axis=-1)
    blk_expert = jnp.minimum(blk_expert, N_EXPERTS - 1).astype(I32)

    tc = 512
    xs = _dispatch(_block_major(dest, tc), pad_end.astype(I32), h2p, n_blocks * MOE_BLOCK, tc)
    ys = _experts(blk_expert, n_used, xs, w_exp_gate[0], w_exp_up[0], w_exp_down[0])
    tm = 256
    out = _combine(_block_major(dest, tm), x1, h2b, top_w.T, mod3, w_sh_gate[0].astype(BF16),
                   w_sh_up[0].astype(BF16), w_sh_down[0].astype(BF16), ys, seq, tm)
    return out.reshape(bsz, seq, d)
```

```python
import functools

import numpy as np
import jax
import jax.numpy as jnp
from jax import lax
from jax.experimental import pallas as pl
from jax.experimental.pallas import tpu as pltpu

F32 = jnp.float32
BF16 = jnp.bfloat16
U32 = jnp.uint32
I32 = jnp.int32

N_HEADS = 8
QK_NOPE = 128
QK_ROPE = 64
QK_DIM = QK_NOPE + QK_ROPE
QK_PAD = 256
V_DIM = 128
Q_RANK = 512
KV_RANK = 256
LAT_PAD = 896
REC_DIM = 128
REC_COLS = 5 * N_HEADS * REC_DIM
ROPE_THETA = 10000.0
N_EXPERTS = 64
TOP_K = 8
N_GROUPS = 8
TOPK_GROUPS = 4
EXPERT_DIM = 512
ROUTED_SCALE = 2.5
MOE_BLOCK = 256
EPS = 1e-6

LANES = 128
SUBLANES = 8
ROW_SLAB = 8
VMEM_LIMIT = 56 * 1024 * 1024

ATT_CHUNK = 256
HG_TILE = 128
HG_BAND = 8
HG_LEVELS = (8, 16, 32, 64)


def _cparams(sem, vmem=VMEM_LIMIT):
    return pltpu.CompilerParams(dimension_semantics=sem, vmem_limit_bytes=vmem)


def _split_bf16(a):
    hi = a.astype(BF16)
    lo = (a - hi.astype(F32)).astype(BF16)
    return hi, lo


def _pack_pair(a, b):
    ua = lax.bitcast_convert_type(a.astype(BF16).astype(F32), U32)
    ub = lax.bitcast_convert_type(b.astype(BF16).astype(F32), U32)
    return ua | (ub >> 16)


def _unpack_pair(p):
    a = lax.bitcast_convert_type(p & jnp.uint32(0xFFFF0000), F32).astype(BF16)
    b = lax.bitcast_convert_type(p << 16, F32).astype(BF16)
    return a, b


def _store_slabs(ref, packed):
    n = packed.shape[0]
    for j in range(ROW_SLAB):
        ref[pl.ds(j, n, stride=ROW_SLAB), :] = packed[:, j * LANES:(j + 1) * LANES]


def _load_slab_chunk(ref, j, n):
    return ref[pl.ds(j, n, stride=ROW_SLAB), :]


def _dot(a, b):
    return jnp.dot(a, b, preferred_element_type=F32)


def _dot_nt(a, b):
    return lax.dot_general(a, b, (((1,), (1,)), ((), ())), preferred_element_type=F32)


def _dot_tn(a, b):
    return lax.dot_general(a, b, (((0,), (0,)), ((), ())), preferred_element_type=F32)


def _ada_kernel(c_ref, w_ref, b_ref, o_ref):
    c = c_ref[...]
    cond = c * jax.nn.sigmoid(c)
    o_ref[...] = _dot(cond.astype(BF16), w_ref[...].astype(BF16)) + b_ref[...]


def _ada(c, w_ada, b_ada):
    bsz, d = c.shape
    n = w_ada.shape[1]
    tn = 1024
    return pl.pallas_call(
        _ada_kernel,
        grid=(n // tn,),
        in_specs=[
            pl.BlockSpec((bsz, d), lambda j: (0, 0)),
            pl.BlockSpec((d, tn), lambda j: (0, j)),
            pl.BlockSpec((1, tn), lambda j: (0, j)),
        ],
        out_specs=pl.BlockSpec((bsz, tn), lambda j: (0, j)),
        out_shape=jax.ShapeDtypeStruct((bsz, n), F32),
        compiler_params=_cparams(("arbitrary",)),
        name="ada",
    )(c, w_ada, b_ada.reshape(1, n))


def _modulated_norm(x, mod_ref, nw, shift_row, scale_row):
    r = lax.rsqrt(jnp.mean(x * x, axis=-1, keepdims=True) + EPS)
    scale = mod_ref[0, scale_row:scale_row + 1, :]
    shift = mod_ref[0, shift_row:shift_row + 1, :]
    return (x * r * nw) * (1.0 + scale) + shift


def _inproj_kernel(x_ref, mod_ref, nw_ref, wlat_ref, wrec_ref, lat_ref, rec_ref, h_scr):
    @pl.when(pl.program_id(1) == 0)
    def _():
        h = _modulated_norm(x_ref[...], mod_ref, nw_ref[...], 0, 1).astype(BF16)
        h_scr[...] = h
        lat_ref[...] = _dot(h, wlat_ref[...])

    rec_ref[...] = _dot(h_scr[...], wrec_ref[...])


def _inproj(x2, mod3, norm_w, w_lat, w_rec, seq):
    t, d = x2.shape
    tm, tn = 1024, 512
    per_b = seq // tm
    return pl.pallas_call(
        _inproj_kernel,
        grid=(t // tm, REC_COLS // tn),
        in_specs=[
            pl.BlockSpec((tm, d), lambda i, j: (i, 0)),
            pl.BlockSpec((1, 6, d), lambda i, j: (i // per_b, 0, 0)),
            pl.BlockSpec((1, d), lambda i, j: (0, 0)),
            pl.BlockSpec((d, LAT_PAD), lambda i, j: (0, 0)),
            pl.BlockSpec((d, tn), lambda i, j: (0, j)),
        ],
        out_specs=[
            pl.BlockSpec((tm, LAT_PAD), lambda i, j: (i, 0)),
            pl.BlockSpec((tm, tn), lambda i, j: (i, j)),
        ],
        out_shape=[
            jax.ShapeDtypeStruct((t, LAT_PAD), F32),
            jax.ShapeDtypeStruct((t, REC_COLS), F32),
        ],
        scratch_shapes=[pltpu.VMEM((tm, d), BF16)],
        compiler_params=_cparams(("parallel", "arbitrary")),
        name="inproj",
    )(x2, mod3, norm_w.reshape(1, d), w_lat, w_rec)


def _rope(x2, cos, sin_a, sin_b):
    return (x2 * cos + pltpu.roll(x2, LANES - QK_ROPE // 2, 1) * sin_a
            + pltpu.roll(x2, QK_ROPE // 2, 1) * sin_b)


def _mla_prep_kernel(lat_ref, cos_ref, sa_ref, sb_ref, qlw_ref, wuq_ref, kvlw_ref, wukv_ref,
                     qnw_ref, knw_ref, q_ref, k_ref, v_ref):
    lat = lat_ref[...]
    ql = lat[:, :Q_RANK]
    kvl = lat[:, Q_RANK:Q_RANK + KV_RANK]
    kr = lat[:, Q_RANK + KV_RANK:]
    qn = (ql * lax.rsqrt(jnp.mean(ql * ql, -1, keepdims=True) + EPS) * qlw_ref[...]).astype(BF16)
    kvn = (kvl * lax.rsqrt(jnp.mean(kvl * kvl, -1, keepdims=True) + EPS) * kvlw_ref[...]).astype(BF16)
    cos, sa, sb = cos_ref[...], sa_ref[...], sb_ref[...]
    qnw = qnw_ref[...]
    knw = knw_ref[...]
    kr_ss = jnp.sum(kr * kr, -1, keepdims=True)
    kr_roped = _rope(kr * knw[:, QK_NOPE:], cos, sa, sb)
    scale = QK_DIM ** -0.5
    for h in range(N_HEADS):
        qh = _dot(qn, wuq_ref[h])
        r = lax.rsqrt(jnp.sum(qh * qh, -1, keepdims=True) * (1.0 / QK_DIM) + EPS) * scale
        qh = qh * r * qnw
        q_ref[0, h, :, :QK_NOPE] = qh[:, :QK_NOPE].astype(BF16)
        q_ref[0, h, :, QK_NOPE:] = _rope(qh[:, QK_NOPE:], cos, sa, sb).astype(BF16)
        kvh = _dot(kvn, wukv_ref[h])
        kn = kvh[:, :QK_NOPE]
        rk = lax.rsqrt((jnp.sum(kn * kn, -1, keepdims=True) + kr_ss) * (1.0 / QK_DIM) + EPS)
        k_ref[0, h, :, :QK_NOPE] = (kn * rk * knw[:, :QK_NOPE]).astype(BF16)
        k_ref[0, h, :, QK_NOPE:] = (kr_roped * rk).astype(BF16)
        v_ref[0, h] = kvh[:, QK_NOPE:].astype(BF16)


def _mla_prep(lat, cos, sa, sb, qlw, wuq, kvlw, wukv, qnw, knw, bsz, seq):
    tm = 512
    per_b = seq // tm
    const2 = lambda i: (0, 0)
    const3 = lambda i: (0, 0, 0)
    head_out = lambda w: pl.BlockSpec((1, N_HEADS, tm, w), lambda i: (i // per_b, 0, i % per_b, 0))
    return pl.pallas_call(
        _mla_prep_kernel,
        grid=(bsz * seq // tm,),
        in_specs=[
            pl.BlockSpec((tm, LAT_PAD), lambda i: (i, 0)),
            pl.BlockSpec((tm, LANES), lambda i: (i, 0)),
            pl.BlockSpec((tm, LANES), lambda i: (i, 0)),
            pl.BlockSpec((tm, LANES), lambda i: (i, 0)),
            pl.BlockSpec((1, Q_RANK), const2),
            pl.BlockSpec((N_HEADS, Q_RANK, QK_PAD), const3),
            pl.BlockSpec((1, KV_RANK), const2),
            pl.BlockSpec((N_HEADS, KV_RANK, QK_PAD), const3),
            pl.BlockSpec((1, QK_PAD), const2),
            pl.BlockSpec((1, QK_PAD), const2),
        ],
        out_specs=[head_out(QK_PAD), head_out(QK_PAD), head_out(V_DIM)],
        out_shape=[
            jax.ShapeDtypeStruct((bsz, N_HEADS, seq, QK_PAD), BF16),
            jax.ShapeDtypeStruct((bsz, N_HEADS, seq, QK_PAD), BF16),
            jax.ShapeDtypeStruct((bsz, N_HEADS, seq, V_DIM), BF16),
        ],
        compiler_params=_cparams(("parallel",)),
        name="mla_prep",
    )(lat, cos, sa, sb, qlw, wuq, kvlw, wukv, qnw, knw)


def _attn_kernel(q_ref, k_ref, v_ref, o_ref):
    k = k_ref[0, 0]
    v = v_ref[0, 0]
    for c in range(q_ref.shape[2] // ATT_CHUNK):
        rows = pl.ds(c * ATT_CHUNK, ATT_CHUNK)
        s = _dot_nt(q_ref[0, 0, rows, :], k)
        m = jnp.max(s, axis=-1, keepdims=True)
        p = jnp.exp(s - m)
        l = jnp.sum(p, axis=-1, keepdims=True)
        o_ref[rows, :] = _dot(p.astype(BF16), v) / l


def _attention(q, k, v):
    bsz, nh, seq, _ = q.shape
    tq = 1024
    nq = seq // tq
    return pl.pallas_call(
        _attn_kernel,
        grid=(bsz, nh, nq),
        in_specs=[
            pl.BlockSpec((1, 1, tq, QK_PAD), lambda b, h, i: (b, h, i, 0)),
            pl.BlockSpec((1, 1, seq, QK_PAD), lambda b, h, i: (b, h, 0, 0)),
            pl.BlockSpec((1, 1, seq, V_DIM), lambda b, h, i: (b, h, 0, 0)),
        ],
        out_specs=pl.BlockSpec((tq, V_DIM), lambda b, h, i: (b * nq + i, h)),
        out_shape=jax.ShapeDtypeStruct((bsz * seq, nh * V_DIM), F32),
        compiler_params=_cparams(("parallel", "parallel", "arbitrary")),
        name="attention",
    )(q, k, v)


def _hgrn_constants():
    n = HG_TILE
    t = np.arange(n)[:, None]
    u = np.arange(n)[None, :]
    tri = np.concatenate([(u <= t), (u >= t)], axis=0).astype(np.float32)
    lvl = np.zeros((n, n), np.float32)
    for li, m in enumerate(HG_LEVELS):
        lvl[((t // (2 * m)) == (u // (2 * m))) & ((t // m) != (u // m))] = li + 1
    ones2 = np.zeros((2 * LANES, 2 * LANES), np.float32)
    ones2[:LANES, :LANES] = 1.0
    ones2[LANES:, LANES:] = 1.0
    return tri, lvl, ones2


def _level_stacks(m, q, kkf, kkb, bf, cb):
    zero = jnp.zeros((m, LANES), F32)
    q_rows, k_rows = [], []
    for a in range(0, HG_TILE, 2 * m):
        e, o = slice(a, a + m), slice(a + m, a + 2 * m)
        ref_f = bf[a + m - 1:a + m, :]
        ref_b = cb[a + m:a + m + 1, :]
        wf_e = jnp.exp(ref_f - bf[e])
        wf_o = jnp.exp(bf[o] - ref_f)
        wb_e = jnp.exp(cb[e] - ref_b)
        wb_o = jnp.exp(ref_b - cb[o])
        q_rows.append(jnp.concatenate([zero, q[e] * wb_e], axis=1))
        q_rows.append(jnp.concatenate([q[o] * wf_o, zero], axis=1))
        k_rows.append(jnp.concatenate([kkf[e] * wf_e, zero], axis=1))
        k_rows.append(jnp.concatenate([zero, kkb[o] * wb_o], axis=1))
    return (jnp.concatenate(q_rows, axis=0).astype(BF16), jnp.concatenate(k_rows, axis=0).astype(BF16))


def _hgrn_kernel(q_ref, zf_ref, zb_ref, v_ref, gate_ref, lbf_ref, lbb_ref, nw_ref,
                 tri_ref, lvl_ref, ones_ref, o_ref,
                 ff_scr, fb_scr, kf_scr, kb_scr, v_scr, gf_scr, gb_scr, o_scr,
                 qe_scr, u_scr, dec_scr, st_scr):
    seq = q_ref.shape[0]
    n_tiles = seq // HG_TILE
    pad = HG_BAND
    zeros_pad = jnp.zeros((pad, LANES), F32)
    for scr in (ff_scr, fb_scr, kf_scr, kb_scr, v_scr):
        scr[pl.ds(0, pad), :] = zeros_pad
        scr[pl.ds(pad + seq, pad), :] = zeros_pad
    v_scr[pl.ds(pad, seq), :] = v_ref[...]
    pos = lax.broadcasted_iota(I32, (seq, 1), 0) % HG_BAND

    lbf = lbf_ref[...]
    f = lbf + (1.0 - lbf) * jax.nn.sigmoid(zf_ref[...])
    gf_scr[...] = jnp.log(f)
    kf_scr[pl.ds(pad, seq), :] = 1.0 - f
    ff_scr[pl.ds(pad, seq), :] = jnp.where(pos == 0, 0.0, f)
    lbb = lbb_ref[...]
    f = lbb + (1.0 - lbb) * jax.nn.sigmoid(zb_ref[...])
    gb_scr[...] = jnp.log(f)
    kb_scr[pl.ds(pad, seq), :] = 1.0 - f
    fb_scr[pl.ds(pad, seq), :] = jnp.where(pos == HG_BAND - 1, 0.0, f)

    def tile_body(i, carry):
        t0 = pl.multiple_of(i * HG_TILE, HG_TILE)
        q = q_ref[pl.ds(t0, HG_TILE), :]
        v = v_scr[pl.ds(pad + t0, HG_TILE), :]
        kkf = kf_scr[pl.ds(pad + t0, HG_TILE), :]
        kkb = kb_scr[pl.ds(pad + t0, HG_TILE), :]
        v_b = v.astype(BF16)
        gf_hi, gf_lo = _split_bf16(gf_scr[pl.ds(t0, HG_TILE), :])
        gb_hi, gb_lo = _split_bf16(gb_scr[pl.ds(t0, HG_TILE), :])
        pf = _dot(tri_ref[pl.ds(0, HG_TILE), :], jnp.concatenate([gf_hi, gf_lo], axis=1))
        pb = _dot(tri_ref[pl.ds(HG_TILE, HG_TILE), :], jnp.concatenate([gb_hi, gb_lo], axis=1))
        bf = pf[:, :LANES] + pf[:, LANES:]
        cb = pb[:, :LANES] + pb[:, LANES:]

        last_f = bf[HG_TILE - 1:HG_TILE, :]
        first_b = cb[0:1, :]
        wq_f = jnp.exp(bf)
        wq_b = jnp.exp(cb)
        ke = jnp.concatenate([kkf * jnp.exp(last_f - bf), kkb * jnp.exp(first_b - cb)], axis=1)
        u_scr[i] = _dot_tn(v_b, ke.astype(BF16))
        qe_scr[i] = jnp.concatenate([q * wq_f, q * wq_b], axis=1).astype(BF16)
        dec_scr[pl.ds(i, 1), :] = jnp.concatenate([wq_f[HG_TILE - 1:HG_TILE, :], wq_b[0:1, :]], axis=1)

        lvl = lvl_ref[...]
        a = jnp.zeros((HG_TILE, HG_TILE), F32)
        for li, m in enumerate(HG_LEVELS):
            qs, ks = _level_stacks(m, q, kkf, kkb, bf, cb)
            a = jnp.where(lvl == float(li + 1), _dot_nt(qs, ks), a)
        o = _dot(a.astype(BF16), v_b)

        ones2 = ones_ref[...]
        pf_run = None
        pb_run = None
        for d in range(HG_BAND):
            if d == 1:
                pf_run = ff_scr[pl.ds(pad + t0, HG_TILE), :]
                pb_run = fb_scr[pl.ds(pad + t0, HG_TILE), :]
            elif d > 1:
                pf_run = pf_run * ff_scr[pl.ds(pad + t0 - (d - 1), HG_TILE), :]
                pb_run = pb_run * fb_scr[pl.ds(pad + t0 + (d - 1), HG_TILE), :]
            p_f = q * kf_scr[pl.ds(pad + t0 - d, HG_TILE), :]
            p_b = q * kb_scr[pl.ds(pad + t0 + d, HG_TILE), :]
            if d > 0:
                p_f = p_f * pf_run
                p_b = p_b * pb_run
            rs = _dot(jnp.concatenate([p_f, p_b], axis=1).astype(BF16), ones2)
            o = (o + rs[:, :LANES] * v_scr[pl.ds(pad + t0 - d, HG_TILE), :]
                 + rs[:, LANES:] * v_scr[pl.ds(pad + t0 + d, HG_TILE), :])
        o_scr[pl.ds(t0, HG_TILE), :] = o
        return carry

    lax.fori_loop(0, n_tiles, tile_body, 0, unroll=2)

    def scan_f(i, state):
        st_scr[i] = state.astype(BF16)
        return state * dec_scr[pl.ds(i, 1), :][:, :LANES] + u_scr[i][:, :LANES]

    lax.fori_loop(0, n_tiles, scan_f, jnp.zeros((V_DIM, REC_DIM), F32))

    def scan_b(n, state):
        i = n_tiles - 1 - n
        t0 = pl.multiple_of(i * HG_TILE, HG_TILE)
        both = jnp.concatenate([st_scr[i], state.astype(BF16)], axis=1)
        o_scr[pl.ds(t0, HG_TILE), :] = o_scr[pl.ds(t0, HG_TILE), :] + _dot_nt(qe_scr[i], both)
        return state * dec_scr[pl.ds(i, 1), :][:, LANES:] + u_scr[i][:, LANES:]

    lax.fori_loop(0, n_tiles, scan_b, jnp.zeros((V_DIM, REC_DIM), F32), unroll=4)

    o = o_scr[...]
    o = o * lax.rsqrt(jnp.mean(o * o, -1, keepdims=True) + EPS) * nw_ref[...]
    gate = gate_ref[...]
    o_ref[...] = o * (gate * jax.nn.sigmoid(gate))


def _hgrn(rec, lbf, lbb, rec_norm_w, bsz, seq):
    tri, lvl, ones2 = _hgrn_constants()
    n_tiles = seq // HG_TILE
    col = lambda c: pl.BlockSpec((seq, REC_DIM), lambda b, h: (b, c * N_HEADS + h))
    vec = pl.BlockSpec((1, REC_DIM), lambda b, h: (0, h))
    const = lambda r, c: pl.BlockSpec((r, c), lambda b, h: (0, 0))
    padded = pltpu.VMEM((seq + 2 * HG_BAND, LANES), F32)
    plain = pltpu.VMEM((seq, LANES), F32)
    return pl.pallas_call(
        _hgrn_kernel,
        grid=(bsz, N_HEADS),
        in_specs=[col(0), col(1), col(2), col(3), col(4), vec, vec, const(1, REC_DIM),
                  const(2 * HG_TILE, HG_TILE), const(HG_TILE, HG_TILE), const(2 * LANES, 2 * LANES)],
        out_specs=pl.BlockSpec((seq, REC_DIM), lambda b, h: (b, h)),
        out_shape=jax.ShapeDtypeStruct((bsz * seq, N_HEADS * REC_DIM), F32),
        scratch_shapes=[padded, padded, padded, padded, padded, plain, plain, plain,
                        pltpu.VMEM((n_tiles, HG_TILE, 2 * LANES), BF16),
                        pltpu.VMEM((n_tiles, V_DIM, 2 * LANES), F32),
                        pltpu.VMEM((n_tiles, 2 * LANES), F32),
                        pltpu.VMEM((n_tiles, V_DIM, LANES), BF16)],
        compiler_params=_cparams(("parallel", "parallel")),
        name="hgrn",
    )(rec, rec, rec, rec, rec, lbf, lbb, rec_norm_w.reshape(1, REC_DIM),
      jnp.asarray(tri, BF16), jnp.asarray(lvl, F32), jnp.asarray(ones2, BF16))


def _outproj_kernel(x_ref, att_ref, rec_ref, mod_ref, wa_ref, wr_ref, o_ref):
    y = _dot(att_ref[...].astype(BF16), wa_ref[...]) + _dot(rec_ref[...].astype(BF16), wr_ref[...])
    o_ref[...] = x_ref[...] + mod_ref[0, 2:3, :] * y


def _outproj(x2, att, rec, mod3, w_att, w_rec, seq):
    t, d = x2.shape
    tm = 512
    per_b = seq // tm
    half = att.shape[1]
    return pl.pallas_call(
        _outproj_kernel,
        grid=(t // tm,),
        in_specs=[
            pl.BlockSpec((tm, d), lambda i: (i, 0)),
            pl.BlockSpec((tm, half), lambda i: (i, 0)),
            pl.BlockSpec((tm, half), lambda i: (i, 0)),
            pl.BlockSpec((1, 6, d), lambda i: (i // per_b, 0, 0)),
            pl.BlockSpec((half, d), lambda i: (0, 0)),
            pl.BlockSpec((half, d), lambda i: (0, 0)),
        ],
        out_specs=pl.BlockSpec((tm, d), lambda i: (i, 0)),
        out_shape=jax.ShapeDtypeStruct((t, d), F32),
        compiler_params=_cparams(("parallel",)),
        name="outproj",
    )(x2, att, rec, mod3, w_att, w_rec)


def _router_kernel(x_ref, mod_ref, nw_ref, wrh_ref, wrl_ref, bias_ref, tri_ref,
                   h2b_ref, h2p_ref, idx_ref, wgt_ref, rank_ref, cnt_ref, base_scr):
    i = pl.program_id(0)
    tm = x_ref.shape[0]
    half = x_ref.shape[1] // 2

    @pl.when(i == 0)
    def _():
        base_scr[...] = jnp.zeros_like(base_scr)

    h2 = _modulated_norm(x_ref[...], mod_ref, nw_ref[...], 3, 4)
    h_hi, h_lo = _split_bf16(h2)
    h2b_ref[...] = h_hi
    _store_slabs(h2p_ref, _pack_pair(h2[:, :half], h2[:, half:]))
    logits = _dot_nt(wrh_ref[...], h_hi) + _dot_nt(wrh_ref[...], h_lo) + _dot_nt(wrl_ref[...], h_hi)
    scores = jax.nn.sigmoid(logits)
    sel = scores + bias_ref[...]

    neg = -jnp.inf
    per_grp = N_EXPERTS // N_GROUPS
    sel3 = sel.reshape(N_GROUPS, per_grp, tm)
    within = lax.broadcasted_iota(I32, sel3.shape, 1)
    m1 = jnp.max(sel3, axis=1, keepdims=True)
    first = jnp.min(jnp.where(sel3 == m1, within, per_grp), axis=1, keepdims=True)
    m2 = jnp.max(jnp.where(within == first, neg, sel3), axis=1, keepdims=True)
    gs = (m1 + m2).reshape(N_GROUPS, tm)
    gidx = lax.broadcasted_iota(I32, gs.shape, 0)
    beaten = jnp.zeros(gs.shape, I32)
    for g2 in range(N_GROUPS):
        other = gs[g2:g2 + 1, :]
        beaten = beaten + ((other > gs) | ((other == gs) & (g2 < gidx))).astype(I32)
    keep = (beaten < TOPK_GROUPS).reshape(N_GROUPS, 1, tm)
    cur = jnp.where(keep, sel3, neg).reshape(N_EXPERTS, tm)

    eidx = lax.broadcasted_iota(I32, (N_EXPERTS, tm), 0)
    chosen = jnp.zeros((N_EXPERTS, tm), F32)
    picks, pick_scores = [], []
    for _ in range(TOP_K):
        m = jnp.max(cur, axis=0, keepdims=True)
        pick = jnp.min(jnp.where(cur == m, eidx, N_EXPERTS), axis=0, keepdims=True)
        hit = eidx == pick
        picks.append(pick)
        pick_scores.append(jnp.sum(jnp.where(hit, scores, 0.0), axis=0, keepdims=True))
        chosen = jnp.where(hit, 1.0, chosen)
        cur = jnp.where(hit, neg, cur)

    before = _dot(chosen.astype(BF16), tri_ref[...]) + base_scr[...][:, 0:1]
    denom = pick_scores[0]
    for s in pick_scores[1:]:
        denom = denom + s
    for j in range(TOP_K):
        idx_ref[j:j + 1, :] = picks[j]
        wgt_ref[j:j + 1, :] = pick_scores[j] / denom * ROUTED_SCALE
        rank_ref[j:j + 1, :] = jnp.sum(jnp.where(eidx == picks[j], before, 0.0), axis=0,
                                       keepdims=True).astype(I32)
    base_scr[...] = base_scr[...] + jnp.sum(chosen, axis=1, keepdims=True)
    cnt_ref[...] = base_scr[...]


def _router(x1, mod3, norm_w, wr_hi, wr_lo, bias, seq):
    t, d = x1.shape
    tm = 512
    per_b = seq // tm
    tri = jnp.asarray(np.triu(np.ones((tm, tm), np.float32), 1), BF16)
    row_out = lambda dt: jax.ShapeDtypeStruct((TOP_K, t), dt)
    row_spec = pl.BlockSpec((TOP_K, tm), lambda i: (0, i))
    return pl.pallas_call(
        _router_kernel,
        grid=(t // tm,),
        in_specs=[
            pl.BlockSpec((tm, d), lambda i: (i, 0)),
            pl.BlockSpec((1, 6, d), lambda i: (i // per_b, 0, 0)),
            pl.BlockSpec((1, d), lambda i: (0, 0)),
            pl.BlockSpec((N_EXPERTS, d), lambda i: (0, 0)),
            pl.BlockSpec((N_EXPERTS, d), lambda i: (0, 0)),
            pl.BlockSpec((N_EXPERTS, 1), lambda i: (0, 0)),
            pl.BlockSpec((tm, tm), lambda i: (0, 0)),
        ],
        out_specs=[
            pl.BlockSpec((tm, d), lambda i: (i, 0)),
            pl.BlockSpec((tm * ROW_SLAB, LANES), lambda i: (i, 0)),
            row_spec, row_spec, row_spec,
            pl.BlockSpec((N_EXPERTS, LANES), lambda i: (0, 0)),
        ],
        out_shape=[
            jax.ShapeDtypeStruct((t, d), BF16),
            jax.ShapeDtypeStruct((t * ROW_SLAB, LANES), U32),
            row_out(I32), row_out(F32), row_out(I32),
            jax.ShapeDtypeStruct((N_EXPERTS, LANES), F32),
        ],
        scratch_shapes=[pltpu.VMEM((N_EXPERTS, LANES), F32)],
        compiler_params=_cparams(("arbitrary",)),
        name="router",
    )(x1, mod3, norm_w.reshape(1, d), wr_hi, wr_lo, bias.reshape(N_EXPERTS, 1), tri)


def _gather_block(src_ref, h2p_ref, dst, sem):
    for i in range(MOE_BLOCK):
        src = pl.multiple_of(src_ref[0, 0, i], ROW_SLAB)
        pltpu.make_async_copy(h2p_ref.at[pl.ds(src, ROW_SLAB)], dst.at[pl.ds(i * ROW_SLAB, ROW_SLAB)],
                              sem).start()


def _expert_kernel(be_ref, nused_ref, src_ref, h2p_ref, wg_ref, wu_ref, wd_ref, ys_ref,
                   wg_scr, wu_scr, wd_scr, x_scr, xbuf, sems):
    s = pl.program_id(0)
    n_used = nused_ref[0]
    half = ROW_SLAB * LANES
    slot = lax.rem(s, 2)
    other = 1 - slot

    @pl.when(s == 0)
    def _():
        _gather_block(src_ref, h2p_ref, xbuf.at[1], sems.at[1])

    @pl.when((s == 0) | (be_ref[s] != be_ref[jnp.maximum(s - 1, 0)]))
    def _():
        wg_scr[...] = wg_ref[...].astype(BF16)
        wu_scr[...] = wu_ref[...].astype(BF16)
        wd_scr[...] = wd_ref[...].astype(BF16)

    @pl.when(s <= n_used)
    def _():
        pltpu.make_async_copy(h2p_ref.at[pl.ds(0, MOE_BLOCK * ROW_SLAB)], xbuf.at[other], sems.at[other]).wait()
        xcur = xbuf.at[other]
        for j in range(ROW_SLAB):
            xa, xb = _unpack_pair(_load_slab_chunk(xcur, j, MOE_BLOCK))
            x_scr[:, j * LANES:(j + 1) * LANES] = xa
            x_scr[:, half + j * LANES:half + (j + 1) * LANES] = xb
        _gather_block(src_ref, h2p_ref, xbuf.at[slot], sems.at[slot])
        xfull = x_scr[...]
        gate = _dot(xfull, wg_scr[...])
        up = _dot(xfull, wu_scr[...])
        mid = ((gate * jax.nn.sigmoid(gate)) * up).astype(BF16)
        y = _dot(mid, wd_scr[...])
        _store_slabs(ys_ref, _pack_pair(y[:, :half], y[:, half:]))

    @pl.when(s > n_used)
    def _():
        ys_ref[...] = jnp.zeros_like(ys_ref)

    @pl.when(s == n_used + 1)
    def _():
        pltpu.make_async_copy(h2p_ref.at[pl.ds(0, MOE_BLOCK * ROW_SLAB)], xbuf.at[other], sems.at[other]).wait()


def _experts(step_expert, n_used, src_rows, h2p, w_gate, w_up, w_down):
    n_steps = step_expert.shape[0]
    n_blocks = n_steps - 1
    d = 2 * ROW_SLAB * LANES
    wspec = lambda r, c: pl.BlockSpec((None, r, c), lambda s, be, nu: (be[s], 0, 0))
    return pl.pallas_call(
        _expert_kernel,
        grid_spec=pltpu.PrefetchScalarGridSpec(
            num_scalar_prefetch=2,
            grid=(n_steps,),
            in_specs=[
                pl.BlockSpec((1, 1, MOE_BLOCK), lambda s, be, nu: (jnp.minimum(s, nu[0] - 1), 0, 0),
                             memory_space=pltpu.SMEM),
                pl.BlockSpec(memory_space=pl.ANY),
                wspec(d, EXPERT_DIM), wspec(d, EXPERT_DIM), wspec(EXPERT_DIM, d),
            ],
            out_specs=pl.BlockSpec((MOE_BLOCK * ROW_SLAB, LANES),
                                   lambda s, be, nu: (jnp.maximum(s - 1, 0), 0)),
            scratch_shapes=[pltpu.VMEM((d, EXPERT_DIM), BF16), pltpu.VMEM((d, EXPERT_DIM), BF16),
                            pltpu.VMEM((EXPERT_DIM, d), BF16), pltpu.VMEM((MOE_BLOCK, d), BF16),
                            pltpu.VMEM((2, MOE_BLOCK * ROW_SLAB, LANES), U32),
                            pltpu.SemaphoreType.DMA((2,))],
        ),
        out_shape=jax.ShapeDtypeStruct((n_blocks * MOE_BLOCK * ROW_SLAB, LANES), U32),
        compiler_params=_cparams(("arbitrary",)),
        name="experts",
    )(step_expert, n_used, src_rows, h2p, w_gate, w_up, w_down)


def _combine_kernel(dest_ref, x1_ref, h2b_ref, wgt_ref, mod_ref, wsg_ref, wsu_ref, wsd_ref, ys_ref,
                    o_ref, gbuf, sem, *, tm):
    def issue(t, carry):
        row = pl.multiple_of(t * ROW_SLAB, ROW_SLAB)
        for j in range(TOP_K):
            d = pl.multiple_of(dest_ref[0, 0, j * tm + t] * ROW_SLAB, ROW_SLAB)
            pltpu.make_async_copy(ys_ref.at[pl.ds(d, ROW_SLAB)], gbuf.at[j, pl.ds(row, ROW_SLAB)],
                                  sem).start()
        return carry

    lax.fori_loop(0, tm, issue, 0)

    h2 = h2b_ref[...]
    gate = _dot(h2, wsg_ref[...])
    up = _dot(h2, wsu_ref[...])
    shared = _dot(((gate * jax.nn.sigmoid(gate)) * up).astype(BF16), wsd_ref[...])

    for j in range(TOP_K):
        pltpu.make_async_copy(ys_ref.at[pl.ds(0, tm * ROW_SLAB)], gbuf.at[j], sem).wait()

    half = ROW_SLAB * LANES
    wgt = wgt_ref[...]
    g2 = mod_ref[0, 5:6, :]
    for c in range(ROW_SLAB):
        ra = jnp.zeros((tm, LANES), F32)
        rb = jnp.zeros((tm, LANES), F32)
        for j in range(TOP_K):
            ya, yb = _unpack_pair(_load_slab_chunk(gbuf.at[j], c, tm))
            wj = wgt[:, j:j + 1]
            ra = ra + wj * ya.astype(F32)
            rb = rb + wj * yb.astype(F32)
        lo = slice(c * LANES, (c + 1) * LANES)
        hi = slice(half + c * LANES, half + (c + 1) * LANES)
        o_ref[:, lo] = x1_ref[:, lo] + g2[:, lo] * (ra + shared[:, lo])
        o_ref[:, hi] = x1_ref[:, hi] + g2[:, hi] * (rb + shared[:, hi])


def _combine(dest_blocks, x1, h2b, wgt_t, mod3, ws_gate, ws_up, ws_down, ys, seq, tm):
    t, d = x1.shape
    per_b = seq // tm
    return pl.pallas_call(
        functools.partial(_combine_kernel, tm=tm),
        grid=(t // tm,),
        in_specs=[
            pl.BlockSpec((1, 1, TOP_K * tm), lambda i: (i, 0, 0), memory_space=pltpu.SMEM),
            pl.BlockSpec((tm, d), lambda i: (i, 0)),
            pl.BlockSpec((tm, d), lambda i: (i, 0)),
            pl.BlockSpec((tm, TOP_K), lambda i: (i, 0)),
            pl.BlockSpec((1, 6, d), lambda i: (i // per_b, 0, 0)),
            pl.BlockSpec((d, EXPERT_DIM), lambda i: (0, 0)),
            pl.BlockSpec((d, EXPERT_DIM), lambda i: (0, 0)),
            pl.BlockSpec((EXPERT_DIM, d), lambda i: (0, 0)),
            pl.BlockSpec(memory_space=pl.ANY),
        ],
        out_specs=pl.BlockSpec((tm, d), lambda i: (i, 0)),
        out_shape=jax.ShapeDtypeStruct((t, d), F32),
        scratch_shapes=[pltpu.VMEM((TOP_K, tm * ROW_SLAB, LANES), U32), pltpu.SemaphoreType.DMA(())],
        compiler_params=_cparams(("arbitrary",)),
        name="combine",
    )(dest_blocks, x1, h2b, wgt_t, mod3, ws_gate, ws_up, ws_down, ys)


def _block_major(a, blk):
    k, t = a.shape
    return a.reshape(k, t // blk, blk).transpose(1, 0, 2).reshape(t // blk, 1, k * blk)


def kernel(x, c, positions, w_ada, b_ada, norm1_w, w_in, q_lat_norm_w, w_uq, kv_lat_norm_w, w_ukv, q_norm_w, k_norm_w, lb_fwd, lb_bwd, rec_norm_w, w_out, norm2_w, w_router, router_bias, w_exp_gate, w_exp_up, w_exp_down, w_sh_gate, w_sh_up, w_sh_down):
    bsz, seq, d = x.shape
    t = bsz * seq
    x2 = x.reshape(t, d)

    w_in0 = w_in[0]
    n_lat = Q_RANK + KV_RANK + QK_ROPE
    w_lat = jnp.pad(w_in0[:, :n_lat], ((0, 0), (0, LAT_PAD - n_lat))).astype(BF16)
    w_rec = w_in0[:, n_lat:].astype(BF16)
    wuq = jnp.pad(w_uq[0].reshape(Q_RANK, N_HEADS, QK_DIM), ((0, 0), (0, 0), (0, QK_PAD - QK_DIM)))
    wuq = wuq.transpose(1, 0, 2).astype(BF16)
    wukv = w_ukv[0].reshape(KV_RANK, N_HEADS, QK_NOPE + V_DIM).transpose(1, 0, 2).astype(BF16)
    qnw = jnp.pad(q_norm_w[0], (0, QK_PAD - QK_DIM)).reshape(1, QK_PAD)
    knw = jnp.pad(k_norm_w[0], (0, QK_PAD - QK_DIM)).reshape(1, QK_PAD)
    half = N_HEADS * V_DIM
    w_out_att = w_out[0, :half].astype(BF16)
    w_out_rec = w_out[0, half:].astype(BF16)
    wr_hi, wr_lo = _split_bf16(w_router[0].T)
    lbf = jnp.cumsum(jax.nn.softmax(lb_fwd.astype(F32), axis=0), axis=0)[0].reshape(1, -1)
    lbb = jnp.cumsum(jax.nn.softmax(lb_bwd.astype(F32), axis=0), axis=0)[0].reshape(1, -1)

    inv_freq = ROPE_THETA ** (-jnp.arange(0, QK_ROPE, 2, dtype=F32) / QK_ROPE)
    ang = positions.astype(F32).reshape(t, 1) * inv_freq[None, :]
    cos_h, sin_h = jnp.cos(ang), jnp.sin(ang)
    zh = jnp.zeros_like(cos_h)
    cos_t = jnp.concatenate([cos_h, cos_h, zh, zh], axis=-1)
    sin_a = jnp.concatenate([-sin_h, zh, zh, zh], axis=-1)
    sin_b = jnp.concatenate([zh, sin_h, zh, zh], axis=-1)

    mod3 = _ada(c, w_ada[0], b_ada[0]).reshape(bsz, 6, d)
    lat, rec_in = _inproj(x2, mod3, norm1_w[0], w_lat, w_rec, seq)
    q, k, v = _mla_prep(lat, cos_t, sin_a, sin_b, q_lat_norm_w[0].reshape(1, -1), wuq,
                        kv_lat_norm_w[0].reshape(1, -1), wukv, qnw, knw, bsz, seq)
    att = _attention(q, k, v)
    rec = _hgrn(rec_in, lbf, lbb, rec_norm_w[0], bsz, seq)
    x1 = _outproj(x2, att, rec, mod3, w_out_att, w_out_rec, seq)

    h2b, h2p, top_idx, top_w, rank, counts = _router(x1, mod3, norm2_w[0], wr_hi, wr_lo,
                                                     router_bias[0], seq)
    counts = counts[:, 0].astype(I32)
    padded = (counts + MOE_BLOCK - 1) // MOE_BLOCK * MOE_BLOCK
    pad_end = jnp.cumsum(padded)
    pad_start = pad_end - padded
    experts = jnp.arange(N_EXPERTS, dtype=I32)
    dest = rank + jnp.sum(jnp.where(top_idx[:, :, None] == experts, pad_start, 0), axis=-1)
    n_blocks = -(-(t * TOP_K) // MOE_BLOCK) + N_EXPERTS
    n_used = pad_end[-1] // MOE_BLOCK
    blk_row0 = jnp.minimum(jnp.arange(n_blocks, dtype=I32), n_used - 1) * MOE_BLOCK
    blk_expert = jnp.sum((pad_end[None, :] <= blk_row0[:, None]).astype(I32), axis=-1)
    step_blk = jnp.clip(jnp.arange(n_blocks + 1, dtype=I32) - 1, 0, n_used - 1)
    step_expert = jnp.sum(jnp.where(step_blk[:, None] == jnp.arange(n_blocks, dtype=I32)[None, :],
                                    blk_expert[None, :], 0), axis=-1)
    n_assign = t * TOP_K
    filler = jnp.arange(MOE_BLOCK - 1, dtype=I32)[None, :] < (padded - counts)[:, None]
    filler_keys = jnp.where(filler, experts[:, None], N_EXPERTS).reshape(-1)
    order = jnp.argsort(jnp.concatenate([top_idx.T.reshape(-1), filler_keys]), stable=True).astype(I32)
    src_rows = jnp.where(order < n_assign, (order // TOP_K) * ROW_SLAB, 0)
    src_rows = jnp.pad(src_rows, (0, n_blocks * MOE_BLOCK - src_rows.shape[0]))
    src_rows = src_rows.reshape(n_blocks, 1, MOE_BLOCK)

    ys = _experts(step_expert.astype(I32), n_used.reshape(1).astype(I32), src_rows, h2p,
                  w_exp_gate[0], w_exp_up[0], w_exp_down[0])
    tm = 256
    out = _combine(_block_major(dest, tm), x1, h2b, top_w.T, mod3, w_sh_gate[0].astype(BF16),
                   w_sh_up[0].astype(BF16), w_sh_down[0].astype(BF16), ys, seq, tm)
    return out.reshape(bsz, seq, d)
```

```python
import functools

import numpy as np
import jax
import jax.numpy as jnp
from jax import lax
from jax.experimental import pallas as pl
from jax.experimental.pallas import tpu as pltpu

F32 = jnp.float32
BF16 = jnp.bfloat16
U32 = jnp.uint32
I32 = jnp.int32

N_HEADS = 8
QK_NOPE = 128
QK_ROPE = 64
QK_DIM = QK_NOPE + QK_ROPE
QK_PAD = 256
V_DIM = 128
Q_RANK = 512
KV_RANK = 256
LAT_PAD = 896
REC_DIM = 128
REC_COLS = 5 * N_HEADS * REC_DIM
ROPE_THETA = 10000.0
N_EXPERTS = 64
TOP_K = 8
N_GROUPS = 8
TOPK_GROUPS = 4
EXPERT_DIM = 512
ROUTED_SCALE = 2.5
MOE_BLOCK = 256
EPS = 1e-6

LANES = 128
SUBLANES = 8
ROW_SLAB = 8
VMEM_LIMIT = 56 * 1024 * 1024

ATT_CHUNK = 256
HG_TILE = 128
HG_BAND = 8
HG_LEVELS = (8, 16, 32, 64)


def _cparams(sem, vmem=VMEM_LIMIT):
    return pltpu.CompilerParams(dimension_semantics=sem, vmem_limit_bytes=vmem)


def _split_bf16(a):
    hi = a.astype(BF16)
    lo = (a - hi.astype(F32)).astype(BF16)
    return hi, lo


def _pack_pair(a, b):
    ua = lax.bitcast_convert_type(a.astype(BF16).astype(F32), U32)
    ub = lax.bitcast_convert_type(b.astype(BF16).astype(F32), U32)
    return ua | (ub >> 16)


def _unpack_pair(p):
    a = lax.bitcast_convert_type(p & jnp.uint32(0xFFFF0000), F32).astype(BF16)
    b = lax.bitcast_convert_type(p << 16, F32).astype(BF16)
    return a, b


def _store_slabs(ref, packed):
    n = packed.shape[0]
    for j in range(ROW_SLAB):
        ref[pl.ds(j, n, stride=ROW_SLAB), :] = packed[:, j * LANES:(j + 1) * LANES]


def _load_slab_chunk(ref, j, n):
    return ref[pl.ds(j, n, stride=ROW_SLAB), :]


def _dot(a, b):
    return jnp.dot(a, b, preferred_element_type=F32)


def _dot_nt(a, b):
    return lax.dot_general(a, b, (((1,), (1,)), ((), ())), preferred_element_type=F32)


def _dot_tn(a, b):
    return lax.dot_general(a, b, (((0,), (0,)), ((), ())), preferred_element_type=F32)


def _ada_kernel(c_ref, w_ref, b_ref, o_ref):
    c = c_ref[...]
    cond = c * jax.nn.sigmoid(c)
    o_ref[...] = _dot(cond.astype(BF16), w_ref[...].astype(BF16)) + b_ref[...]


def _ada(c, w_ada, b_ada):
    bsz, d = c.shape
    n = w_ada.shape[1]
    tn = 1024
    return pl.pallas_call(
        _ada_kernel,
        grid=(n // tn,),
        in_specs=[
            pl.BlockSpec((bsz, d), lambda j: (0, 0)),
            pl.BlockSpec((d, tn), lambda j: (0, j)),
            pl.BlockSpec((1, tn), lambda j: (0, j)),
        ],
        out_specs=pl.BlockSpec((bsz, tn), lambda j: (0, j)),
        out_shape=jax.ShapeDtypeStruct((bsz, n), F32),
        compiler_params=_cparams(("arbitrary",)),
        name="ada",
    )(c, w_ada, b_ada.reshape(1, n))


def _modulated_norm(x, mod_ref, nw, shift_row, scale_row):
    r = lax.rsqrt(jnp.mean(x * x, axis=-1, keepdims=True) + EPS)
    scale = mod_ref[0, scale_row:scale_row + 1, :]
    shift = mod_ref[0, shift_row:shift_row + 1, :]
    return (x * r * nw) * (1.0 + scale) + shift


def _inproj_kernel(x_ref, mod_ref, nw_ref, wlat_ref, wrec_ref, lat_ref, rec_ref, h_scr):
    @pl.when(pl.program_id(1) == 0)
    def _():
        h = _modulated_norm(x_ref[...], mod_ref, nw_ref[...], 0, 1).astype(BF16)
        h_scr[...] = h
        lat_ref[...] = _dot(h, wlat_ref[...])

    rec_ref[...] = _dot(h_scr[...], wrec_ref[...])


def _inproj(x2, mod3, norm_w, w_lat, w_rec, seq):
    t, d = x2.shape
    tm, tn = 1024, 512
    per_b = seq // tm
    return pl.pallas_call(
        _inproj_kernel,
        grid=(t // tm, REC_COLS // tn),
        in_specs=[
            pl.BlockSpec((tm, d), lambda i, j: (i, 0)),
            pl.BlockSpec((1, 6, d), lambda i, j: (i // per_b, 0, 0)),
            pl.BlockSpec((1, d), lambda i, j: (0, 0)),
            pl.BlockSpec((d, LAT_PAD), lambda i, j: (0, 0)),
            pl.BlockSpec((d, tn), lambda i, j: (0, j)),
        ],
        out_specs=[
            pl.BlockSpec((tm, LAT_PAD), lambda i, j: (i, 0)),
            pl.BlockSpec((tm, tn), lambda i, j: (i, j)),
        ],
        out_shape=[
            jax.ShapeDtypeStruct((t, LAT_PAD), F32),
            jax.ShapeDtypeStruct((t, REC_COLS), F32),
        ],
        scratch_shapes=[pltpu.VMEM((tm, d), BF16)],
        compiler_params=_cparams(("parallel", "arbitrary")),
        name="inproj",
    )(x2, mod3, norm_w.reshape(1, d), w_lat, w_rec)


def _rope(x2, cos, sin_a, sin_b):
    return (x2 * cos + pltpu.roll(x2, LANES - QK_ROPE // 2, 1) * sin_a
            + pltpu.roll(x2, QK_ROPE // 2, 1) * sin_b)


def _mla_prep_kernel(lat_ref, cos_ref, sa_ref, sb_ref, qlw_ref, wuq_ref, kvlw_ref, wukv_ref,
                     qnw_ref, knw_ref, q_ref, k_ref, v_ref):
    lat = lat_ref[...]
    ql = lat[:, :Q_RANK]
    kvl = lat[:, Q_RANK:Q_RANK + KV_RANK]
    kr = lat[:, Q_RANK + KV_RANK:]
    qn = (ql * lax.rsqrt(jnp.mean(ql * ql, -1, keepdims=True) + EPS) * qlw_ref[...]).astype(BF16)
    kvn = (kvl * lax.rsqrt(jnp.mean(kvl * kvl, -1, keepdims=True) + EPS) * kvlw_ref[...]).astype(BF16)
    cos, sa, sb = cos_ref[...], sa_ref[...], sb_ref[...]
    qnw = qnw_ref[...]
    knw = knw_ref[...]
    kr_ss = jnp.sum(kr * kr, -1, keepdims=True)
    kr_roped = _rope(kr * knw[:, QK_NOPE:], cos, sa, sb)
    scale = QK_DIM ** -0.5
    for h in range(N_HEADS):
        qh = _dot(qn, wuq_ref[h])
        r = lax.rsqrt(jnp.sum(qh * qh, -1, keepdims=True) * (1.0 / QK_DIM) + EPS) * scale
        qh = qh * r * qnw
        q_ref[0, h, :, :QK_NOPE] = qh[:, :QK_NOPE].astype(BF16)
        q_ref[0, h, :, QK_NOPE:] = _rope(qh[:, QK_NOPE:], cos, sa, sb).astype(BF16)
        kvh = _dot(kvn, wukv_ref[h])
        kn = kvh[:, :QK_NOPE]
        rk = lax.rsqrt((jnp.sum(kn * kn, -1, keepdims=True) + kr_ss) * (1.0 / QK_DIM) + EPS)
        k_ref[0, h, :, :QK_NOPE] = (kn * rk * knw[:, :QK_NOPE]).astype(BF16)
        k_ref[0, h, :, QK_NOPE:] = (kr_roped * rk).astype(BF16)
        v_ref[0, h] = kvh[:, QK_NOPE:].astype(BF16)


def _mla_prep(lat, cos, sa, sb, qlw, wuq, kvlw, wukv, qnw, knw, bsz, seq):
    tm = 512
    per_b = seq // tm
    const2 = lambda i: (0, 0)
    const3 = lambda i: (0, 0, 0)
    head_out = lambda w: pl.BlockSpec((1, N_HEADS, tm, w), lambda i: (i // per_b, 0, i % per_b, 0))
    return pl.pallas_call(
        _mla_prep_kernel,
        grid=(bsz * seq // tm,),
        in_specs=[
            pl.BlockSpec((tm, LAT_PAD), lambda i: (i, 0)),
            pl.BlockSpec((tm, LANES), lambda i: (i, 0)),
            pl.BlockSpec((tm, LANES), lambda i: (i, 0)),
            pl.BlockSpec((tm, LANES), lambda i: (i, 0)),
            pl.BlockSpec((1, Q_RANK), const2),
            pl.BlockSpec((N_HEADS, Q_RANK, QK_PAD), const3),
            pl.BlockSpec((1, KV_RANK), const2),
            pl.BlockSpec((N_HEADS, KV_RANK, QK_PAD), const3),
            pl.BlockSpec((1, QK_PAD), const2),
            pl.BlockSpec((1, QK_PAD), const2),
        ],
        out_specs=[head_out(QK_PAD), head_out(QK_PAD), head_out(V_DIM)],
        out_shape=[
            jax.ShapeDtypeStruct((bsz, N_HEADS, seq, QK_PAD), BF16),
            jax.ShapeDtypeStruct((bsz, N_HEADS, seq, QK_PAD), BF16),
            jax.ShapeDtypeStruct((bsz, N_HEADS, seq, V_DIM), BF16),
        ],
        compiler_params=_cparams(("parallel",)),
        name="mla_prep",
    )(lat, cos, sa, sb, qlw, wuq, kvlw, wukv, qnw, knw)


def _attn_kernel(q_ref, k_ref, v_ref, o_ref):
    k = k_ref[0, 0]
    v = v_ref[0, 0]
    for c in range(q_ref.shape[2] // ATT_CHUNK):
        rows = pl.ds(c * ATT_CHUNK, ATT_CHUNK)
        s = _dot_nt(q_ref[0, 0, rows, :], k)
        m = jnp.max(s, axis=-1, keepdims=True)
        p = jnp.exp(s - m)
        l = jnp.sum(p, axis=-1, keepdims=True)
        o_ref[rows, :] = _dot(p.astype(BF16), v) / l


def _attention(q, k, v):
    bsz, nh, seq, _ = q.shape
    tq = 1024
    nq = seq // tq
    return pl.pallas_call(
        _attn_kernel,
        grid=(bsz, nh, nq),
        in_specs=[
            pl.BlockSpec((1, 1, tq, QK_PAD), lambda b, h, i: (b, h, i, 0)),
            pl.BlockSpec((1, 1, seq, QK_PAD), lambda b, h, i: (b, h, 0, 0)),
            pl.BlockSpec((1, 1, seq, V_DIM), lambda b, h, i: (b, h, 0, 0)),
        ],
        out_specs=pl.BlockSpec((tq, V_DIM), lambda b, h, i: (b * nq + i, h)),
        out_shape=jax.ShapeDtypeStruct((bsz * seq, nh * V_DIM), F32),
        compiler_params=_cparams(("parallel", "parallel", "arbitrary")),
        name="attention",
    )(q, k, v)


def _hgrn_constants():
    n = HG_TILE
    t = np.arange(n)[:, None]
    u = np.arange(n)[None, :]
    tri = np.concatenate([(u <= t), (u >= t)], axis=0).astype(np.float32)
    lvl = np.zeros((n, n), np.float32)
    for li, m in enumerate(HG_LEVELS):
        lvl[((t // (2 * m)) == (u // (2 * m))) & ((t // m) != (u // m))] = li + 1
    ones2 = np.zeros((2 * LANES, 2 * LANES), np.float32)
    ones2[:LANES, :LANES] = 1.0
    ones2[LANES:, LANES:] = 1.0
    return tri, lvl, ones2


def _level_stacks(m, q, kkf, kkb, bf, cb):
    zero = jnp.zeros((m, LANES), F32)
    q_rows, k_rows = [], []
    for a in range(0, HG_TILE, 2 * m):
        e, o = slice(a, a + m), slice(a + m, a + 2 * m)
        ref_f = bf[a + m - 1:a + m, :]
        ref_b = cb[a + m:a + m + 1, :]
        wf_e = jnp.exp(ref_f - bf[e])
        wf_o = jnp.exp(bf[o] - ref_f)
        wb_e = jnp.exp(cb[e] - ref_b)
        wb_o = jnp.exp(ref_b - cb[o])
        q_rows.append(jnp.concatenate([zero, q[e] * wb_e], axis=1))
        q_rows.append(jnp.concatenate([q[o] * wf_o, zero], axis=1))
        k_rows.append(jnp.concatenate([kkf[e] * wf_e, zero], axis=1))
        k_rows.append(jnp.concatenate([zero, kkb[o] * wb_o], axis=1))
    return (jnp.concatenate(q_rows, axis=0).astype(BF16), jnp.concatenate(k_rows, axis=0).astype(BF16))


def _hgrn_kernel(q_ref, zf_ref, zb_ref, v_ref, gate_ref, lbf_ref, lbb_ref, nw_ref,
                 tri_ref, lvl_ref, ones_ref, o_ref,
                 ff_scr, fb_scr, kf_scr, kb_scr, v_scr, gf_scr, gb_scr, o_scr,
                 qe_scr, u_scr, dec_scr, st_scr):
    seq = q_ref.shape[0]
    n_tiles = seq // HG_TILE
    pad = HG_BAND
    zeros_pad = jnp.zeros((pad, LANES), F32)
    for scr in (ff_scr, fb_scr, kf_scr, kb_scr, v_scr):
        scr[pl.ds(0, pad), :] = zeros_pad
        scr[pl.ds(pad + seq, pad), :] = zeros_pad
    v_scr[pl.ds(pad, seq), :] = v_ref[...]
    pos = lax.broadcasted_iota(I32, (seq, 1), 0) % HG_BAND

    lbf = lbf_ref[...]
    f = lbf + (1.0 - lbf) * jax.nn.sigmoid(zf_ref[...])
    gf_scr[...] = jnp.log(f)
    kf_scr[pl.ds(pad, seq), :] = 1.0 - f
    ff_scr[pl.ds(pad, seq), :] = jnp.where(pos == 0, 0.0, f)
    lbb = lbb_ref[...]
    f = lbb + (1.0 - lbb) * jax.nn.sigmoid(zb_ref[...])
    gb_scr[...] = jnp.log(f)
    kb_scr[pl.ds(pad, seq), :] = 1.0 - f
    fb_scr[pl.ds(pad, seq), :] = jnp.where(pos == HG_BAND - 1, 0.0, f)

    def tile_body(i, carry):
        t0 = pl.multiple_of(i * HG_TILE, HG_TILE)
        q = q_ref[pl.ds(t0, HG_TILE), :]
        v = v_scr[pl.ds(pad + t0, HG_TILE), :]
        kkf = kf_scr[pl.ds(pad + t0, HG_TILE), :]
        kkb = kb_scr[pl.ds(pad + t0, HG_TILE), :]
        v_b = v.astype(BF16)
        gf_hi, gf_lo = _split_bf16(gf_scr[pl.ds(t0, HG_TILE), :])
        gb_hi, gb_lo = _split_bf16(gb_scr[pl.ds(t0, HG_TILE), :])
        pf = _dot(tri_ref[pl.ds(0, HG_TILE), :], jnp.concatenate([gf_hi, gf_lo], axis=1))
        pb = _dot(tri_ref[pl.ds(HG_TILE, HG_TILE), :], jnp.concatenate([gb_hi, gb_lo], axis=1))
        bf = pf[:, :LANES] + pf[:, LANES:]
        cb = pb[:, :LANES] + pb[:, LANES:]

        last_f = bf[HG_TILE - 1:HG_TILE, :]
        first_b = cb[0:1, :]
        wq_f = jnp.exp(bf)
        wq_b = jnp.exp(cb)
        ke = jnp.concatenate([kkf * jnp.exp(last_f - bf), kkb * jnp.exp(first_b - cb)], axis=1)
        u_scr[i] = _dot_tn(v_b, ke.astype(BF16))
        qe_scr[i] = jnp.concatenate([q * wq_f, q * wq_b], axis=1).astype(BF16)
        dec_scr[pl.ds(i, 1), :] = jnp.concatenate([wq_f[HG_TILE - 1:HG_TILE, :], wq_b[0:1, :]], axis=1)

        lvl = lvl_ref[...]
        a = jnp.zeros((HG_TILE, HG_TILE), F32)
        for li, m in enumerate(HG_LEVELS):
            qs, ks = _level_stacks(m, q, kkf, kkb, bf, cb)
            a = jnp.where(lvl == float(li + 1), _dot_nt(qs, ks), a)
        o = _dot(a.astype(BF16), v_b)

        ones2 = ones_ref[...]
        pf_run = None
        pb_run = None
        for d in range(HG_BAND):
            if d == 1:
                pf_run = ff_scr[pl.ds(pad + t0, HG_TILE), :]
                pb_run = fb_scr[pl.ds(pad + t0, HG_TILE), :]
            elif d > 1:
                pf_run = pf_run * ff_scr[pl.ds(pad + t0 - (d - 1), HG_TILE), :]
                pb_run = pb_run * fb_scr[pl.ds(pad + t0 + (d - 1), HG_TILE), :]
            p_f = q * kf_scr[pl.ds(pad + t0 - d, HG_TILE), :]
            p_b = q * kb_scr[pl.ds(pad + t0 + d, HG_TILE), :]
            if d > 0:
                p_f = p_f * pf_run
                p_b = p_b * pb_run
            rs = _dot(jnp.concatenate([p_f, p_b], axis=1).astype(BF16), ones2)
            o = (o + rs[:, :LANES] * v_scr[pl.ds(pad + t0 - d, HG_TILE), :]
                 + rs[:, LANES:] * v_scr[pl.ds(pad + t0 + d, HG_TILE), :])
        o_scr[pl.ds(t0, HG_TILE), :] = o
        return carry

    lax.fori_loop(0, n_tiles, tile_body, 0, unroll=2)

    def scan_f(i, state):
        st_scr[i] = state.astype(BF16)
        return state * dec_scr[pl.ds(i, 1), :][:, :LANES] + u_scr[i][:, :LANES]

    lax.fori_loop(0, n_tiles, scan_f, jnp.zeros((V_DIM, REC_DIM), F32))

    def scan_b(n, state):
        i = n_tiles - 1 - n
        t0 = pl.multiple_of(i * HG_TILE, HG_TILE)
        both = jnp.concatenate([st_scr[i], state.astype(BF16)], axis=1)
        o_scr[pl.ds(t0, HG_TILE), :] = o_scr[pl.ds(t0, HG_TILE), :] + _dot_nt(qe_scr[i], both)
        return state * dec_scr[pl.ds(i, 1), :][:, LANES:] + u_scr[i][:, LANES:]

    lax.fori_loop(0, n_tiles, scan_b, jnp.zeros((V_DIM, REC_DIM), F32), unroll=4)

    o = o_scr[...]
    o = o * lax.rsqrt(jnp.mean(o * o, -1, keepdims=True) + EPS) * nw_ref[...]
    gate = gate_ref[...]
    o_ref[...] = o * (gate * jax.nn.sigmoid(gate))


def _hgrn(rec, lbf, lbb, rec_norm_w, bsz, seq):
    tri, lvl, ones2 = _hgrn_constants()
    n_tiles = seq // HG_TILE
    col = lambda c: pl.BlockSpec((seq, REC_DIM), lambda b, h: (b, c * N_HEADS + h))
    vec = pl.BlockSpec((1, REC_DIM), lambda b, h: (0, h))
    const = lambda r, c: pl.BlockSpec((r, c), lambda b, h: (0, 0))
    padded = pltpu.VMEM((seq + 2 * HG_BAND, LANES), F32)
    plain = pltpu.VMEM((seq, LANES), F32)
    return pl.pallas_call(
        _hgrn_kernel,
        grid=(bsz, N_HEADS),
        in_specs=[col(0), col(1), col(2), col(3), col(4), vec, vec, const(1, REC_DIM),
                  const(2 * HG_TILE, HG_TILE), const(HG_TILE, HG_TILE), const(2 * LANES, 2 * LANES)],
        out_specs=pl.BlockSpec((seq, REC_DIM), lambda b, h: (b, h)),
        out_shape=jax.ShapeDtypeStruct((bsz * seq, N_HEADS * REC_DIM), F32),
        scratch_shapes=[padded, padded, padded, padded, padded, plain, plain, plain,
                        pltpu.VMEM((n_tiles, HG_TILE, 2 * LANES), BF16),
                        pltpu.VMEM((n_tiles, V_DIM, 2 * LANES), F32),
                        pltpu.VMEM((n_tiles, 2 * LANES), F32),
                        pltpu.VMEM((n_tiles, V_DIM, LANES), BF16)],
        compiler_params=_cparams(("parallel", "parallel")),
        name="hgrn",
    )(rec, rec, rec, rec, rec, lbf, lbb, rec_norm_w.reshape(1, REC_DIM),
      jnp.asarray(tri, BF16), jnp.asarray(lvl, F32), jnp.asarray(ones2, BF16))


def _outproj_kernel(x_ref, att_ref, rec_ref, mod_ref, wa_ref, wr_ref, o_ref):
    y = _dot(att_ref[...].astype(BF16), wa_ref[...]) + _dot(rec_ref[...].astype(BF16), wr_ref[...])
    o_ref[...] = x_ref[...] + mod_ref[0, 2:3, :] * y


def _outproj(x2, att, rec, mod3, w_att, w_rec, seq):
    t, d = x2.shape
    tm = 512
    per_b = seq // tm
    half = att.shape[1]
    return pl.pallas_call(
        _outproj_kernel,
        grid=(t // tm,),
        in_specs=[
            pl.BlockSpec((tm, d), lambda i: (i, 0)),
            pl.BlockSpec((tm, half), lambda i: (i, 0)),
            pl.BlockSpec((tm, half), lambda i: (i, 0)),
            pl.BlockSpec((1, 6, d), lambda i: (i // per_b, 0, 0)),
            pl.BlockSpec((half, d), lambda i: (0, 0)),
            pl.BlockSpec((half, d), lambda i: (0, 0)),
        ],
        out_specs=pl.BlockSpec((tm, d), lambda i: (i, 0)),
        out_shape=jax.ShapeDtypeStruct((t, d), F32),
        compiler_params=_cparams(("parallel",)),
        name="outproj",
    )(x2, att, rec, mod3, w_att, w_rec)


def _router_kernel(x_ref, mod_ref, nw_ref, wrh_ref, wrl_ref, bias_ref,
                   h2b_ref, h2p_ref, idx_ref, wgt_ref, cnt_ref, base_scr):
    i = pl.program_id(0)
    tm = x_ref.shape[0]
    half = x_ref.shape[1] // 2

    @pl.when(i == 0)
    def _():
        base_scr[...] = jnp.zeros_like(base_scr)

    h2 = _modulated_norm(x_ref[...], mod_ref, nw_ref[...], 3, 4)
    h_hi, h_lo = _split_bf16(h2)
    h2b_ref[...] = h_hi
    _store_slabs(h2p_ref, _pack_pair(h2[:, :half], h2[:, half:]))
    logits = _dot_nt(wrh_ref[...], h_hi) + _dot_nt(wrh_ref[...], h_lo) + _dot_nt(wrl_ref[...], h_hi)
    scores = jax.nn.sigmoid(logits)
    sel = scores + bias_ref[...]

    neg = -jnp.inf
    per_grp = N_EXPERTS // N_GROUPS
    sel3 = sel.reshape(N_GROUPS, per_grp, tm)
    within = lax.broadcasted_iota(I32, sel3.shape, 1)
    m1 = jnp.max(sel3, axis=1, keepdims=True)
    first = jnp.min(jnp.where(sel3 == m1, within, per_grp), axis=1, keepdims=True)
    m2 = jnp.max(jnp.where(within == first, neg, sel3), axis=1, keepdims=True)
    gs = (m1 + m2).reshape(N_GROUPS, tm)
    gidx = lax.broadcasted_iota(I32, gs.shape, 0)
    beaten = jnp.zeros(gs.shape, I32)
    for g2 in range(N_GROUPS):
        other = gs[g2:g2 + 1, :]
        beaten = beaten + ((other > gs) | ((other == gs) & (g2 < gidx))).astype(I32)
    keep = (beaten < TOPK_GROUPS).reshape(N_GROUPS, 1, tm)
    cur = jnp.where(keep, sel3, neg).reshape(N_EXPERTS, tm)

    eidx = lax.broadcasted_iota(I32, (N_EXPERTS, tm), 0)
    chosen = jnp.zeros((N_EXPERTS, tm), F32)
    picks, pick_scores = [], []
    for _ in range(TOP_K):
        m = jnp.max(cur, axis=0, keepdims=True)
        pick = jnp.min(jnp.where(cur == m, eidx, N_EXPERTS), axis=0, keepdims=True)
        hit = eidx == pick
        picks.append(pick)
        pick_scores.append(jnp.sum(jnp.where(hit, scores, 0.0), axis=0, keepdims=True))
        chosen = jnp.where(hit, 1.0, chosen)
        cur = jnp.where(hit, neg, cur)

    denom = pick_scores[0]
    for s in pick_scores[1:]:
        denom = denom + s
    for j in range(TOP_K):
        idx_ref[j:j + 1, :] = picks[j]
        wgt_ref[j:j + 1, :] = pick_scores[j] / denom * ROUTED_SCALE
    base_scr[...] = base_scr[...] + jnp.sum(chosen, axis=1, keepdims=True)
    cnt_ref[...] = base_scr[...]


def _router(x1, mod3, norm_w, wr_hi, wr_lo, bias, seq):
    t, d = x1.shape
    tm = 512
    per_b = seq // tm
    row_out = lambda dt: jax.ShapeDtypeStruct((TOP_K, t), dt)
    row_spec = pl.BlockSpec((TOP_K, tm), lambda i: (0, i))
    return pl.pallas_call(
        _router_kernel,
        grid=(t // tm,),
        in_specs=[
            pl.BlockSpec((tm, d), lambda i: (i, 0)),
            pl.BlockSpec((1, 6, d), lambda i: (i // per_b, 0, 0)),
            pl.BlockSpec((1, d), lambda i: (0, 0)),
            pl.BlockSpec((N_EXPERTS, d), lambda i: (0, 0)),
            pl.BlockSpec((N_EXPERTS, d), lambda i: (0, 0)),
            pl.BlockSpec((N_EXPERTS, 1), lambda i: (0, 0)),
        ],
        out_specs=[
            pl.BlockSpec((tm, d), lambda i: (i, 0)),
            pl.BlockSpec((tm * ROW_SLAB, LANES), lambda i: (i, 0)),
            row_spec, row_spec,
            pl.BlockSpec((N_EXPERTS, LANES), lambda i: (0, 0)),
        ],
        out_shape=[
            jax.ShapeDtypeStruct((t, d), BF16),
            jax.ShapeDtypeStruct((t * ROW_SLAB, LANES), U32),
            row_out(I32), row_out(F32),
            jax.ShapeDtypeStruct((N_EXPERTS, LANES), F32),
        ],
        scratch_shapes=[pltpu.VMEM((N_EXPERTS, LANES), F32)],
        compiler_params=_cparams(("arbitrary",)),
        name="router",
    )(x1, mod3, norm_w.reshape(1, d), wr_hi, wr_lo, bias.reshape(N_EXPERTS, 1))


BLOCK_SLABS = MOE_BLOCK * ROW_SLAB


def _gather_block(src_ref, h2p_ref, dst, sem):
    for i in range(MOE_BLOCK):
        src = pl.multiple_of(src_ref[0, 0, i], ROW_SLAB)
        pltpu.make_async_copy(h2p_ref.at[pl.ds(src, ROW_SLAB)], dst.at[pl.ds(i * ROW_SLAB, ROW_SLAB)],
                              sem).start(priority=i % 2)


def _scatter_block(dst_ref, src, out_ref, sem):
    for i in range(MOE_BLOCK):
        dst = pl.multiple_of(dst_ref[0, 0, i], ROW_SLAB)
        pltpu.make_async_copy(src.at[pl.ds(i * ROW_SLAB, ROW_SLAB)], out_ref.at[pl.ds(dst, ROW_SLAB)],
                              sem).start(priority=i % 2)


def _expert_kernel(be_ref, nused_ref, src_ref, dst_ref, h2p_ref, wg_ref, wu_ref, wd_ref, out_ref,
                   wg_scr, wu_scr, wd_scr, x_scr, xbuf, ybuf, xsem, ysem):
    s = pl.program_id(0)
    n_used = nused_ref[0]
    half = ROW_SLAB * LANES
    n_real = out_ref.shape[0] - 2 * BLOCK_SLABS
    slot = lax.rem(s, 2)
    other = 1 - slot
    gathered = lambda sem: pltpu.make_async_copy(h2p_ref.at[pl.ds(0, BLOCK_SLABS)], xbuf.at[0], sem)
    scattered = lambda sem: pltpu.make_async_copy(ybuf.at[0], out_ref.at[pl.ds(0, BLOCK_SLABS)], sem)

    @pl.when(s == 0)
    def _():
        ybuf[...] = jnp.zeros_like(ybuf)
        for q in range(2):
            cp = pltpu.make_async_copy(ybuf.at[q], out_ref.at[pl.ds(n_real + q * BLOCK_SLABS, BLOCK_SLABS)],
                                       ysem.at[q])
            cp.start()
            cp.wait()
        pltpu.make_async_copy(ybuf.at[0], out_ref.at[pl.ds(n_real, BLOCK_SLABS)], ysem.at[0]).start()
        _gather_block(src_ref, h2p_ref, xbuf.at[0], xsem.at[0])

    @pl.when((s == 0) | (be_ref[s] != be_ref[jnp.maximum(s - 1, 0)]))
    def _():
        wg_scr[...] = wg_ref[...].astype(BF16)
        wu_scr[...] = wu_ref[...].astype(BF16)
        wd_scr[...] = wd_ref[...].astype(BF16)

    @pl.when((s >= 1) & (s <= n_used + 1))
    def _():
        gathered(xsem.at[other]).wait()
        xcur = xbuf.at[other]
        for j in range(ROW_SLAB):
            xa, xb = _unpack_pair(_load_slab_chunk(xcur, j, MOE_BLOCK))
            x_scr[:, j * LANES:(j + 1) * LANES] = xa
            x_scr[:, half + j * LANES:half + (j + 1) * LANES] = xb
        _gather_block(src_ref, h2p_ref, xbuf.at[slot], xsem.at[slot])
        _scatter_block(dst_ref, ybuf.at[slot], out_ref, ysem.at[slot])
        xfull = x_scr[...]
        gate = _dot(xfull, wg_scr[...])
        up = _dot(xfull, wu_scr[...])
        mid = ((gate * jax.nn.sigmoid(gate)) * up).astype(BF16)
        y = _dot(mid, wd_scr[...])
        scattered(ysem.at[other]).wait()
        _store_slabs(ybuf.at[other], _pack_pair(y[:, :half], y[:, half:]))

    @pl.when(s == n_used + 2)
    def _():
        gathered(xsem.at[other]).wait()
        scattered(ysem.at[other]).wait()


def _experts(step_expert, n_used, src_rows, dst_rows, h2p, w_gate, w_up, w_down, n_out_slabs):
    n_steps = step_expert.shape[0]
    n_blocks = n_steps - 3
    d = 2 * ROW_SLAB * LANES
    wspec = lambda r, c: pl.BlockSpec((None, r, c), lambda s, be, nu: (be[s], 0, 0))
    scattered = lambda s, be, nu: (jnp.where(s >= 2, jnp.minimum(s - 2, nu[0] - 1), n_blocks), 0, 0)
    return pl.pallas_call(
        _expert_kernel,
        grid_spec=pltpu.PrefetchScalarGridSpec(
            num_scalar_prefetch=2,
            grid=(n_steps,),
            in_specs=[
                pl.BlockSpec((1, 1, MOE_BLOCK), lambda s, be, nu: (jnp.minimum(s, nu[0] - 1), 0, 0),
                             memory_space=pltpu.SMEM),
                pl.BlockSpec((1, 1, MOE_BLOCK), scattered, memory_space=pltpu.SMEM),
                pl.BlockSpec(memory_space=pl.ANY),
                wspec(d, EXPERT_DIM), wspec(d, EXPERT_DIM), wspec(EXPERT_DIM, d),
            ],
            out_specs=pl.BlockSpec(memory_space=pl.ANY),
            scratch_shapes=[pltpu.VMEM((d, EXPERT_DIM), BF16), pltpu.VMEM((d, EXPERT_DIM), BF16),
                            pltpu.VMEM((EXPERT_DIM, d), BF16), pltpu.VMEM((MOE_BLOCK, d), BF16),
                            pltpu.VMEM((2, BLOCK_SLABS, LANES), U32),
                            pltpu.VMEM((2, BLOCK_SLABS, LANES), U32),
                            pltpu.SemaphoreType.DMA((2,)), pltpu.SemaphoreType.DMA((2,))],
        ),
        out_shape=jax.ShapeDtypeStruct((n_out_slabs + 2 * BLOCK_SLABS, LANES), U32),
        compiler_params=_cparams(("arbitrary",)),
        name="experts",
    )(step_expert, n_used, src_rows, dst_rows, h2p, w_gate, w_up, w_down)


def _combine_kernel(x1_ref, h2b_ref, wgt_ref, mod_ref, wsg_ref, wsu_ref, wsd_ref, *rest):
    slot_refs, o_ref = rest[:TOP_K], rest[TOP_K]
    tm = x1_ref.shape[0]
    h2 = h2b_ref[...]
    gate = _dot(h2, wsg_ref[...])
    up = _dot(h2, wsu_ref[...])
    shared = _dot(((gate * jax.nn.sigmoid(gate)) * up).astype(BF16), wsd_ref[...])

    half = ROW_SLAB * LANES
    wgt = wgt_ref[...]
    g2 = mod_ref[0, 5:6, :]
    for c in range(ROW_SLAB):
        ra = jnp.zeros((tm, LANES), F32)
        rb = jnp.zeros((tm, LANES), F32)
        for j in range(TOP_K):
            ya, yb = _unpack_pair(_load_slab_chunk(slot_refs[j], c, tm))
            wj = wgt[:, j:j + 1]
            ra = ra + wj * ya.astype(F32)
            rb = rb + wj * yb.astype(F32)
        lo = slice(c * LANES, (c + 1) * LANES)
        hi = slice(half + c * LANES, half + (c + 1) * LANES)
        o_ref[:, lo] = x1_ref[:, lo] + g2[:, lo] * (ra + shared[:, lo])
        o_ref[:, hi] = x1_ref[:, hi] + g2[:, hi] * (rb + shared[:, hi])


def _combine(x1, h2b, wgt_t, mod3, ws_gate, ws_up, ws_down, ys, seq):
    t, d = x1.shape
    tm = 256
    per_b = seq // tm
    n_tiles = t // tm
    slot_spec = lambda j: pl.BlockSpec((tm * ROW_SLAB, LANES), lambda i: (j * n_tiles + i, 0))
    return pl.pallas_call(
        _combine_kernel,
        grid=(n_tiles,),
        in_specs=[
            pl.BlockSpec((tm, d), lambda i: (i, 0)),
            pl.BlockSpec((tm, d), lambda i: (i, 0)),
            pl.BlockSpec((tm, TOP_K), lambda i: (i, 0)),
            pl.BlockSpec((1, 6, d), lambda i: (i // per_b, 0, 0)),
            pl.BlockSpec((d, EXPERT_DIM), lambda i: (0, 0)),
            pl.BlockSpec((d, EXPERT_DIM), lambda i: (0, 0)),
            pl.BlockSpec((EXPERT_DIM, d), lambda i: (0, 0)),
        ] + [slot_spec(j) for j in range(TOP_K)],
        out_specs=pl.BlockSpec((tm, d), lambda i: (i, 0)),
        out_shape=jax.ShapeDtypeStruct((t, d), F32),
        compiler_params=_cparams(("parallel",)),
        name="combine",
    )(x1, h2b, wgt_t, mod3, ws_gate, ws_up, ws_down, *([ys] * TOP_K))


def kernel(x, c, positions, w_ada, b_ada, norm1_w, w_in, q_lat_norm_w, w_uq, kv_lat_norm_w, w_ukv, q_norm_w, k_norm_w, lb_fwd, lb_bwd, rec_norm_w, w_out, norm2_w, w_router, router_bias, w_exp_gate, w_exp_up, w_exp_down, w_sh_gate, w_sh_up, w_sh_down):
    bsz, seq, d = x.shape
    t = bsz * seq
    x2 = x.reshape(t, d)

    w_in0 = w_in[0]
    n_lat = Q_RANK + KV_RANK + QK_ROPE
    w_lat = jnp.pad(w_in0[:, :n_lat], ((0, 0), (0, LAT_PAD - n_lat))).astype(BF16)
    w_rec = w_in0[:, n_lat:].astype(BF16)
    wuq = jnp.pad(w_uq[0].reshape(Q_RANK, N_HEADS, QK_DIM), ((0, 0), (0, 0), (0, QK_PAD - QK_DIM)))
    wuq = wuq.transpose(1, 0, 2).astype(BF16)
    wukv = w_ukv[0].reshape(KV_RANK, N_HEADS, QK_NOPE + V_DIM).transpose(1, 0, 2).astype(BF16)
    qnw = jnp.pad(q_norm_w[0], (0, QK_PAD - QK_DIM)).reshape(1, QK_PAD)
    knw = jnp.pad(k_norm_w[0], (0, QK_PAD - QK_DIM)).reshape(1, QK_PAD)
    half = N_HEADS * V_DIM
    w_out_att = w_out[0, :half].astype(BF16)
    w_out_rec = w_out[0, half:].astype(BF16)
    wr_hi, wr_lo = _split_bf16(w_router[0].T)
    lbf = jnp.cumsum(jax.nn.softmax(lb_fwd.astype(F32), axis=0), axis=0)[0].reshape(1, -1)
    lbb = jnp.cumsum(jax.nn.softmax(lb_bwd.astype(F32), axis=0), axis=0)[0].reshape(1, -1)

    inv_freq = ROPE_THETA ** (-jnp.arange(0, QK_ROPE, 2, dtype=F32) / QK_ROPE)
    ang = positions.astype(F32).reshape(t, 1) * inv_freq[None, :]
    cos_h, sin_h = jnp.cos(ang), jnp.sin(ang)
    zh = jnp.zeros_like(cos_h)
    cos_t = jnp.concatenate([cos_h, cos_h, zh, zh], axis=-1)
    sin_a = jnp.concatenate([-sin_h, zh, zh, zh], axis=-1)
    sin_b = jnp.concatenate([zh, sin_h, zh, zh], axis=-1)

    mod3 = _ada(c, w_ada[0], b_ada[0]).reshape(bsz, 6, d)
    lat, rec_in = _inproj(x2, mod3, norm1_w[0], w_lat, w_rec, seq)
    q, k, v = _mla_prep(lat, cos_t, sin_a, sin_b, q_lat_norm_w[0].reshape(1, -1), wuq,
                        kv_lat_norm_w[0].reshape(1, -1), wukv, qnw, knw, bsz, seq)
    att = _attention(q, k, v)
    rec = _hgrn(rec_in, lbf, lbb, rec_norm_w[0], bsz, seq)
    x1 = _outproj(x2, att, rec, mod3, w_out_att, w_out_rec, seq)

    h2b, h2p, top_idx, top_w, counts = _router(x1, mod3, norm2_w[0], wr_hi, wr_lo, router_bias[0], seq)
    counts = counts[:, 0].astype(I32)
    padded = (counts + MOE_BLOCK - 1) // MOE_BLOCK * MOE_BLOCK
    pad_end = jnp.cumsum(padded)
    experts = jnp.arange(N_EXPERTS, dtype=I32)
    n_blocks = -(-(t * TOP_K) // MOE_BLOCK) + N_EXPERTS
    n_used = pad_end[-1] // MOE_BLOCK
    blk_row0 = jnp.minimum(jnp.arange(n_blocks, dtype=I32), n_used - 1) * MOE_BLOCK
    blk_expert = jnp.sum((pad_end[None, :] <= blk_row0[:, None]).astype(I32), axis=-1)
    step_blk = jnp.clip(jnp.arange(n_blocks + 3, dtype=I32) - 1, 0, n_used - 1)
    step_expert = jnp.sum(jnp.where(step_blk[:, None] == jnp.arange(n_blocks, dtype=I32)[None, :],
                                    blk_expert[None, :], 0), axis=-1)
    n_assign = t * TOP_K
    filler = jnp.arange(MOE_BLOCK - 1, dtype=I32)[None, :] < (padded - counts)[:, None]
    filler_keys = jnp.where(filler, experts[:, None], N_EXPERTS).reshape(-1)
    order = jnp.argsort(jnp.concatenate([top_idx.T.reshape(-1), filler_keys]), stable=True).astype(I32)
    order = jnp.pad(order, (0, n_blocks * MOE_BLOCK - order.shape[0]), constant_values=n_assign)
    real = order < n_assign
    pos = jnp.arange(n_blocks * MOE_BLOCK, dtype=I32)
    zone_slab = n_assign + ((pos // MOE_BLOCK) % 2) * MOE_BLOCK + pos % MOE_BLOCK
    src_rows = jnp.where(real, (order // TOP_K) * ROW_SLAB, 0).reshape(n_blocks, 1, MOE_BLOCK)
    dst_rows = jnp.where(real, (order % TOP_K) * t + order // TOP_K, zone_slab) * ROW_SLAB
    all_filler = (n_assign + MOE_BLOCK + jnp.arange(MOE_BLOCK, dtype=I32)) * ROW_SLAB
    dst_rows = jnp.concatenate([dst_rows, all_filler]).reshape(n_blocks + 1, 1, MOE_BLOCK)

    ys = _experts(step_expert.astype(I32), n_used.reshape(1).astype(I32), src_rows, dst_rows, h2p,
                  w_exp_gate[0], w_exp_up[0], w_exp_down[0], n_assign * ROW_SLAB)
    out = _combine(x1, h2b, top_w.T, mod3, w_sh_gate[0].astype(BF16), w_sh_up[0].astype(BF16),
                   w_sh_down[0].astype(BF16), ys, seq)
    return out.reshape(bsz, seq, d)
```

```python
import functools

import numpy as np
import jax
import jax.numpy as jnp
from jax import lax
from jax.experimental import pallas as pl
from jax.experimental.pallas import tpu as pltpu

F32 = jnp.float32
BF16 = jnp.bfloat16
U32 = jnp.uint32
I32 = jnp.int32

N_HEADS = 8
QK_NOPE = 128
QK_ROPE = 64
QK_DIM = QK_NOPE + QK_ROPE
QK_PAD = 256
V_DIM = 128
Q_RANK = 512
KV_RANK = 256
LAT_PAD = 896
REC_DIM = 128
REC_COLS = 5 * N_HEADS * REC_DIM
ROPE_THETA = 10000.0
N_EXPERTS = 64
TOP_K = 8
N_GROUPS = 8
TOPK_GROUPS = 4
EXPERT_DIM = 512
ROUTED_SCALE = 2.5
MOE_BLOCK = 256
EPS = 1e-6

LANES = 128
SUBLANES = 8
ROW_SLAB = 8
VMEM_LIMIT = 56 * 1024 * 1024

ATT_CHUNK = 256
HG_TILE = 128
HG_BAND = 8
HG_LEVELS = (8, 16, 32, 64)


def _cparams(sem, vmem=VMEM_LIMIT):
    return pltpu.CompilerParams(dimension_semantics=sem, vmem_limit_bytes=vmem)


def _split_bf16(a):
    hi = a.astype(BF16)
    lo = (a - hi.astype(F32)).astype(BF16)
    return hi, lo


def _pack_pair(a, b):
    ua = lax.bitcast_convert_type(a.astype(BF16).astype(F32), U32)
    ub = lax.bitcast_convert_type(b.astype(BF16).astype(F32), U32)
    return ua | (ub >> 16)


def _unpack_pair(p):
    a = lax.bitcast_convert_type(p & jnp.uint32(0xFFFF0000), F32).astype(BF16)
    b = lax.bitcast_convert_type(p << 16, F32).astype(BF16)
    return a, b


def _store_slabs(ref, packed):
    n = packed.shape[0]
    for j in range(ROW_SLAB):
        ref[pl.ds(j, n, stride=ROW_SLAB), :] = packed[:, j * LANES:(j + 1) * LANES]


def _load_slab_chunk(ref, j, n):
    return ref[pl.ds(j, n, stride=ROW_SLAB), :]


def _dot(a, b):
    return jnp.dot(a, b, preferred_element_type=F32)


def _dot_nt(a, b):
    return lax.dot_general(a, b, (((1,), (1,)), ((), ())), preferred_element_type=F32)


def _dot_tn(a, b):
    return lax.dot_general(a, b, (((0,), (0,)), ((), ())), preferred_element_type=F32)


def _ada_kernel(c_ref, w_ref, b_ref, o_ref):
    c = c_ref[...]
    cond = c * jax.nn.sigmoid(c)
    o_ref[...] = _dot(cond.astype(BF16), w_ref[...].astype(BF16)) + b_ref[...]


def _ada(c, w_ada, b_ada):
    bsz, d = c.shape
    n = w_ada.shape[1]
    tn = 1024
    return pl.pallas_call(
        _ada_kernel,
        grid=(n // tn,),
        in_specs=[
            pl.BlockSpec((bsz, d), lambda j: (0, 0)),
            pl.BlockSpec((d, tn), lambda j: (0, j)),
            pl.BlockSpec((1, tn), lambda j: (0, j)),
        ],
        out_specs=pl.BlockSpec((bsz, tn), lambda j: (0, j)),
        out_shape=jax.ShapeDtypeStruct((bsz, n), F32),
        compiler_params=_cparams(("arbitrary",)),
        name="ada",
    )(c, w_ada, b_ada.reshape(1, n))


def _modulated_norm(x, mod_ref, nw, shift_row, scale_row):
    r = lax.rsqrt(jnp.mean(x * x, axis=-1, keepdims=True) + EPS)
    scale = mod_ref[0, scale_row:scale_row + 1, :]
    shift = mod_ref[0, shift_row:shift_row + 1, :]
    return (x * r * nw) * (1.0 + scale) + shift


def _inproj_kernel(x_ref, mod_ref, nw_ref, wlat_ref, wrec_ref, lat_ref, rec_ref, h_scr):
    @pl.when(pl.program_id(1) == 0)
    def _():
        h = _modulated_norm(x_ref[...], mod_ref, nw_ref[...], 0, 1).astype(BF16)
        h_scr[...] = h
        lat_ref[...] = _dot(h, wlat_ref[...])

    rec_ref[...] = _dot(h_scr[...], wrec_ref[...])


def _inproj(x2, mod3, norm_w, w_lat, w_rec, seq):
    t, d = x2.shape
    tm, tn = 1024, 512
    per_b = seq // tm
    return pl.pallas_call(
        _inproj_kernel,
        grid=(t // tm, REC_COLS // tn),
        in_specs=[
            pl.BlockSpec((tm, d), lambda i, j: (i, 0)),
            pl.BlockSpec((1, 6, d), lambda i, j: (i // per_b, 0, 0)),
            pl.BlockSpec((1, d), lambda i, j: (0, 0)),
            pl.BlockSpec((d, LAT_PAD), lambda i, j: (0, 0)),
            pl.BlockSpec((d, tn), lambda i, j: (0, j)),
        ],
        out_specs=[
            pl.BlockSpec((tm, LAT_PAD), lambda i, j: (i, 0)),
            pl.BlockSpec((tm, tn), lambda i, j: (i, j)),
        ],
        out_shape=[
            jax.ShapeDtypeStruct((t, LAT_PAD), F32),
            jax.ShapeDtypeStruct((t, REC_COLS), F32),
        ],
        scratch_shapes=[pltpu.VMEM((tm, d), BF16)],
        compiler_params=_cparams(("parallel", "arbitrary")),
        name="inproj",
    )(x2, mod3, norm_w.reshape(1, d), w_lat, w_rec)


def _rope(x2, cos, sin_a, sin_b):
    return (x2 * cos + pltpu.roll(x2, LANES - QK_ROPE // 2, 1) * sin_a
            + pltpu.roll(x2, QK_ROPE // 2, 1) * sin_b)


def _mla_prep_kernel(lat_ref, cos_ref, sa_ref, sb_ref, qlw_ref, wuq_ref, kvlw_ref, wukv_ref,
                     qnw_ref, knw_ref, q_ref, k_ref, v_ref):
    lat = lat_ref[...]
    ql = lat[:, :Q_RANK]
    kvl = lat[:, Q_RANK:Q_RANK + KV_RANK]
    kr = lat[:, Q_RANK + KV_RANK:]
    qn = (ql * lax.rsqrt(jnp.mean(ql * ql, -1, keepdims=True) + EPS) * qlw_ref[...]).astype(BF16)
    kvn = (kvl * lax.rsqrt(jnp.mean(kvl * kvl, -1, keepdims=True) + EPS) * kvlw_ref[...]).astype(BF16)
    cos, sa, sb = cos_ref[...], sa_ref[...], sb_ref[...]
    qnw = qnw_ref[...]
    knw = knw_ref[...]
    kr_ss = jnp.sum(kr * kr, -1, keepdims=True)
    kr_roped = _rope(kr * knw[:, QK_NOPE:], cos, sa, sb)
    scale = QK_DIM ** -0.5
    for h in range(N_HEADS):
        qh = _dot(qn, wuq_ref[h])
        r = lax.rsqrt(jnp.sum(qh * qh, -1, keepdims=True) * (1.0 / QK_DIM) + EPS) * scale
        qh = qh * r * qnw
        q_ref[0, h, :, :QK_NOPE] = qh[:, :QK_NOPE].astype(BF16)
        q_ref[0, h, :, QK_NOPE:] = _rope(qh[:, QK_NOPE:], cos, sa, sb).astype(BF16)
        kvh = _dot(kvn, wukv_ref[h])
        kn = kvh[:, :QK_NOPE]
        rk = lax.rsqrt((jnp.sum(kn * kn, -1, keepdims=True) + kr_ss) * (1.0 / QK_DIM) + EPS)
        k_ref[0, h, :, :QK_NOPE] = (kn * rk * knw[:, :QK_NOPE]).astype(BF16)
        k_ref[0, h, :, QK_NOPE:] = (kr_roped * rk).astype(BF16)
        v_ref[0, h] = kvh[:, QK_NOPE:].astype(BF16)


def _mla_prep(lat, cos, sa, sb, qlw, wuq, kvlw, wukv, qnw, knw, bsz, seq):
    tm = 512
    per_b = seq // tm
    const2 = lambda i: (0, 0)
    const3 = lambda i: (0, 0, 0)
    head_out = lambda w: pl.BlockSpec((1, N_HEADS, tm, w), lambda i: (i // per_b, 0, i % per_b, 0))
    return pl.pallas_call(
        _mla_prep_kernel,
        grid=(bsz * seq // tm,),
        in_specs=[
            pl.BlockSpec((tm, LAT_PAD), lambda i: (i, 0)),
            pl.BlockSpec((tm, LANES), lambda i: (i, 0)),
            pl.BlockSpec((tm, LANES), lambda i: (i, 0)),
            pl.BlockSpec((tm, LANES), lambda i: (i, 0)),
            pl.BlockSpec((1, Q_RANK), const2),
            pl.BlockSpec((N_HEADS, Q_RANK, QK_PAD), const3),
            pl.BlockSpec((1, KV_RANK), const2),
            pl.BlockSpec((N_HEADS, KV_RANK, QK_PAD), const3),
            pl.BlockSpec((1, QK_PAD), const2),
            pl.BlockSpec((1, QK_PAD), const2),
        ],
        out_specs=[head_out(QK_PAD), head_out(QK_PAD), head_out(V_DIM)],
        out_shape=[
            jax.ShapeDtypeStruct((bsz, N_HEADS, seq, QK_PAD), BF16),
            jax.ShapeDtypeStruct((bsz, N_HEADS, seq, QK_PAD), BF16),
            jax.ShapeDtypeStruct((bsz, N_HEADS, seq, V_DIM), BF16),
        ],
        compiler_params=_cparams(("parallel",)),
        name="mla_prep",
    )(lat, cos, sa, sb, qlw, wuq, kvlw, wukv, qnw, knw)


def _attn_kernel(q_ref, k_ref, v_ref, o_ref):
    k = k_ref[0, 0]
    v = v_ref[0, 0]
    for c in range(q_ref.shape[2] // ATT_CHUNK):
        rows = pl.ds(c * ATT_CHUNK, ATT_CHUNK)
        s = _dot_nt(q_ref[0, 0, rows, :], k)
        m = jnp.max(s, axis=-1, keepdims=True)
        p = jnp.exp(s - m)
        l = jnp.sum(p, axis=-1, keepdims=True)
        o_ref[rows, :] = _dot(p.astype(BF16), v) / l


def _attention(q, k, v):
    bsz, nh, seq, _ = q.shape
    tq = 1024
    nq = seq // tq
    return pl.pallas_call(
        _attn_kernel,
        grid=(bsz, nh, nq),
        in_specs=[
            pl.BlockSpec((1, 1, tq, QK_PAD), lambda b, h, i: (b, h, i, 0)),
            pl.BlockSpec((1, 1, seq, QK_PAD), lambda b, h, i: (b, h, 0, 0)),
            pl.BlockSpec((1, 1, seq, V_DIM), lambda b, h, i: (b, h, 0, 0)),
        ],
        out_specs=pl.BlockSpec((tq, V_DIM), lambda b, h, i: (b * nq + i, h)),
        out_shape=jax.ShapeDtypeStruct((bsz * seq, nh * V_DIM), F32),
        compiler_params=_cparams(("parallel", "parallel", "arbitrary")),
        name="attention",
    )(q, k, v)


def _hgrn_constants():
    n = HG_TILE
    t = np.arange(n)[:, None]
    u = np.arange(n)[None, :]
    tri = np.concatenate([(u <= t), (u >= t)], axis=0).astype(np.float32)
    lvl = np.zeros((n, n), np.float32)
    for li, m in enumerate(HG_LEVELS):
        lvl[((t // (2 * m)) == (u // (2 * m))) & ((t // m) != (u // m))] = li + 1
    ones2 = np.zeros((2 * LANES, 2 * LANES), np.float32)
    ones2[:LANES, :LANES] = 1.0
    ones2[LANES:, LANES:] = 1.0
    return tri, lvl, ones2


def _level_stacks(m, q, kkf, kkb, bf, cb):
    zero = jnp.zeros((m, LANES), F32)
    q_rows, k_rows = [], []
    for a in range(0, HG_TILE, 2 * m):
        e, o = slice(a, a + m), slice(a + m, a + 2 * m)
        ref_f = bf[a + m - 1:a + m, :]
        ref_b = cb[a + m:a + m + 1, :]
        wf_e = jnp.exp(ref_f - bf[e])
        wf_o = jnp.exp(bf[o] - ref_f)
        wb_e = jnp.exp(cb[e] - ref_b)
        wb_o = jnp.exp(ref_b - cb[o])
        q_rows.append(jnp.concatenate([zero, q[e] * wb_e], axis=1))
        q_rows.append(jnp.concatenate([q[o] * wf_o, zero], axis=1))
        k_rows.append(jnp.concatenate([kkf[e] * wf_e, zero], axis=1))
        k_rows.append(jnp.concatenate([zero, kkb[o] * wb_o], axis=1))
    return (jnp.concatenate(q_rows, axis=0).astype(BF16), jnp.concatenate(k_rows, axis=0).astype(BF16))


def _hgrn_kernel(q_ref, zf_ref, zb_ref, v_ref, gate_ref, lbf_ref, lbb_ref, nw_ref,
                 tri_ref, lvl_ref, ones_ref, o_ref,
                 ff_scr, fb_scr, kf_scr, kb_scr, v_scr, gf_scr, gb_scr, o_scr,
                 qe_scr, u_scr, dec_scr, st_scr):
    seq = q_ref.shape[0]
    n_tiles = seq // HG_TILE
    pad = HG_BAND
    zeros_pad = jnp.zeros((pad, LANES), F32)
    for scr in (ff_scr, fb_scr, kf_scr, kb_scr, v_scr):
        scr[pl.ds(0, pad), :] = zeros_pad
        scr[pl.ds(pad + seq, pad), :] = zeros_pad
    v_scr[pl.ds(pad, seq), :] = v_ref[...]
    pos = lax.broadcasted_iota(I32, (seq, 1), 0) % HG_BAND

    lbf = lbf_ref[...]
    f = lbf + (1.0 - lbf) * jax.nn.sigmoid(zf_ref[...])
    gf_scr[...] = jnp.log(f)
    kf_scr[pl.ds(pad, seq), :] = 1.0 - f
    ff_scr[pl.ds(pad, seq), :] = jnp.where(pos == 0, 0.0, f)
    lbb = lbb_ref[...]
    f = lbb + (1.0 - lbb) * jax.nn.sigmoid(zb_ref[...])
    gb_scr[...] = jnp.log(f)
    kb_scr[pl.ds(pad, seq), :] = 1.0 - f
    fb_scr[pl.ds(pad, seq), :] = jnp.where(pos == HG_BAND - 1, 0.0, f)

    def tile_body(i, carry):
        t0 = pl.multiple_of(i * HG_TILE, HG_TILE)
        q = q_ref[pl.ds(t0, HG_TILE), :]
        v = v_scr[pl.ds(pad + t0, HG_TILE), :]
        kkf = kf_scr[pl.ds(pad + t0, HG_TILE), :]
        kkb = kb_scr[pl.ds(pad + t0, HG_TILE), :]
        v_b = v.astype(BF16)
        gf_hi, gf_lo = _split_bf16(gf_scr[pl.ds(t0, HG_TILE), :])
        gb_hi, gb_lo = _split_bf16(gb_scr[pl.ds(t0, HG_TILE), :])
        pf = _dot(tri_ref[pl.ds(0, HG_TILE), :], jnp.concatenate([gf_hi, gf_lo], axis=1))
        pb = _dot(tri_ref[pl.ds(HG_TILE, HG_TILE), :], jnp.concatenate([gb_hi, gb_lo], axis=1))
        bf = pf[:, :LANES] + pf[:, LANES:]
        cb = pb[:, :LANES] + pb[:, LANES:]

        last_f = bf[HG_TILE - 1:HG_TILE, :]
        first_b = cb[0:1, :]
        wq_f = jnp.exp(bf)
        wq_b = jnp.exp(cb)
        ke = jnp.concatenate([kkf * jnp.exp(last_f - bf), kkb * jnp.exp(first_b - cb)], axis=1)
        u_scr[i] = _dot_tn(v_b, ke.astype(BF16))
        qe_scr[i] = jnp.concatenate([q * wq_f, q * wq_b], axis=1).astype(BF16)
        dec_scr[pl.ds(i, 1), :] = jnp.concatenate([wq_f[HG_TILE - 1:HG_TILE, :], wq_b[0:1, :]], axis=1)

        lvl = lvl_ref[...]
        a = jnp.zeros((HG_TILE, HG_TILE), F32)
        for li, m in enumerate(HG_LEVELS):
            qs, ks = _level_stacks(m, q, kkf, kkb, bf, cb)
            a = jnp.where(lvl == float(li + 1), _dot_nt(qs, ks), a)
        o = _dot(a.astype(BF16), v_b)

        ones2 = ones_ref[...]
        pf_run = None
        pb_run = None
        for d in range(HG_BAND):
            if d == 1:
                pf_run = ff_scr[pl.ds(pad + t0, HG_TILE), :]
                pb_run = fb_scr[pl.ds(pad + t0, HG_TILE), :]
            elif d > 1:
                pf_run = pf_run * ff_scr[pl.ds(pad + t0 - (d - 1), HG_TILE), :]
                pb_run = pb_run * fb_scr[pl.ds(pad + t0 + (d - 1), HG_TILE), :]
            p_f = q * kf_scr[pl.ds(pad + t0 - d, HG_TILE), :]
            p_b = q * kb_scr[pl.ds(pad + t0 + d, HG_TILE), :]
            if d > 0:
                p_f = p_f * pf_run
                p_b = p_b * pb_run
            rs = _dot(jnp.concatenate([p_f, p_b], axis=1).astype(BF16), ones2)
            o = (o + rs[:, :LANES] * v_scr[pl.ds(pad + t0 - d, HG_TILE), :]
                 + rs[:, LANES:] * v_scr[pl.ds(pad + t0 + d, HG_TILE), :])
        o_scr[pl.ds(t0, HG_TILE), :] = o
        return carry

    lax.fori_loop(0, n_tiles, tile_body, 0, unroll=2)

    def scan_f(i, state):
        st_scr[i] = state.astype(BF16)
        return state * dec_scr[pl.ds(i, 1), :][:, :LANES] + u_scr[i][:, :LANES]

    lax.fori_loop(0, n_tiles, scan_f, jnp.zeros((V_DIM, REC_DIM), F32))

    def scan_b(n, state):
        i = n_tiles - 1 - n
        t0 = pl.multiple_of(i * HG_TILE, HG_TILE)
        both = jnp.concatenate([st_scr[i], state.astype(BF16)], axis=1)
        o_scr[pl.ds(t0, HG_TILE), :] = o_scr[pl.ds(t0, HG_TILE), :] + _dot_nt(qe_scr[i], both)
        return state * dec_scr[pl.ds(i, 1), :][:, LANES:] + u_scr[i][:, LANES:]

    lax.fori_loop(0, n_tiles, scan_b, jnp.zeros((V_DIM, REC_DIM), F32), unroll=4)

    o = o_scr[...]
    o = o * lax.rsqrt(jnp.mean(o * o, -1, keepdims=True) + EPS) * nw_ref[...]
    gate = gate_ref[...]
    o_ref[...] = o * (gate * jax.nn.sigmoid(gate))


def _hgrn(rec, lbf, lbb, rec_norm_w, bsz, seq):
    tri, lvl, ones2 = _hgrn_constants()
    n_tiles = seq // HG_TILE
    col = lambda c: pl.BlockSpec((seq, REC_DIM), lambda b, h: (b, c * N_HEADS + h))
    vec = pl.BlockSpec((1, REC_DIM), lambda b, h: (0, h))
    const = lambda r, c: pl.BlockSpec((r, c), lambda b, h: (0, 0))
    padded = pltpu.VMEM((seq + 2 * HG_BAND, LANES), F32)
    plain = pltpu.VMEM((seq, LANES), F32)
    return pl.pallas_call(
        _hgrn_kernel,
        grid=(bsz, N_HEADS),
        in_specs=[col(0), col(1), col(2), col(3), col(4), vec, vec, const(1, REC_DIM),
                  const(2 * HG_TILE, HG_TILE), const(HG_TILE, HG_TILE), const(2 * LANES, 2 * LANES)],
        out_specs=pl.BlockSpec((seq, REC_DIM), lambda b, h: (b, h)),
        out_shape=jax.ShapeDtypeStruct((bsz * seq, N_HEADS * REC_DIM), F32),
        scratch_shapes=[padded, padded, padded, padded, padded, plain, plain, plain,
                        pltpu.VMEM((n_tiles, HG_TILE, 2 * LANES), BF16),
                        pltpu.VMEM((n_tiles, V_DIM, 2 * LANES), F32),
                        pltpu.VMEM((n_tiles, 2 * LANES), F32),
                        pltpu.VMEM((n_tiles, V_DIM, LANES), BF16)],
        compiler_params=_cparams(("parallel", "parallel")),
        name="hgrn",
    )(rec, rec, rec, rec, rec, lbf, lbb, rec_norm_w.reshape(1, REC_DIM),
      jnp.asarray(tri, BF16), jnp.asarray(lvl, F32), jnp.asarray(ones2, BF16))


def _outproj_kernel(x_ref, att_ref, rec_ref, mod_ref, wa_ref, wr_ref, o_ref):
    y = _dot(att_ref[...].astype(BF16), wa_ref[...]) + _dot(rec_ref[...].astype(BF16), wr_ref[...])
    o_ref[...] = x_ref[...] + mod_ref[0, 2:3, :] * y


def _outproj(x2, att, rec, mod3, w_att, w_rec, seq):
    t, d = x2.shape
    tm = 512
    per_b = seq // tm
    half = att.shape[1]
    return pl.pallas_call(
        _outproj_kernel,
        grid=(t // tm,),
        in_specs=[
            pl.BlockSpec((tm, d), lambda i: (i, 0)),
            pl.BlockSpec((tm, half), lambda i: (i, 0)),
            pl.BlockSpec((tm, half), lambda i: (i, 0)),
            pl.BlockSpec((1, 6, d), lambda i: (i // per_b, 0, 0)),
            pl.BlockSpec((half, d), lambda i: (0, 0)),
            pl.BlockSpec((half, d), lambda i: (0, 0)),
        ],
        out_specs=pl.BlockSpec((tm, d), lambda i: (i, 0)),
        out_shape=jax.ShapeDtypeStruct((t, d), F32),
        compiler_params=_cparams(("parallel",)),
        name="outproj",
    )(x2, att, rec, mod3, w_att, w_rec)


def _router_kernel(x_ref, mod_ref, nw_ref, wrh_ref, wrl_ref, bias_ref,
                   h2b_ref, h2p_ref, idx_ref, wgt_ref, cnt_ref, base_scr):
    i = pl.program_id(0)
    tm = x_ref.shape[0]
    half = x_ref.shape[1] // 2

    @pl.when(i == 0)
    def _():
        base_scr[...] = jnp.zeros_like(base_scr)

    h2 = _modulated_norm(x_ref[...], mod_ref, nw_ref[...], 3, 4)
    h_hi, h_lo = _split_bf16(h2)
    h2b_ref[...] = h_hi
    _store_slabs(h2p_ref, _pack_pair(h2[:, :half], h2[:, half:]))
    logits = _dot_nt(wrh_ref[...], h_hi) + _dot_nt(wrh_ref[...], h_lo) + _dot_nt(wrl_ref[...], h_hi)
    scores = jax.nn.sigmoid(logits)
    sel = scores + bias_ref[...]

    neg = -jnp.inf
    per_grp = N_EXPERTS // N_GROUPS
    sel3 = sel.reshape(N_GROUPS, per_grp, tm)
    within = lax.broadcasted_iota(I32, sel3.shape, 1)
    m1 = jnp.max(sel3, axis=1, keepdims=True)
    first = jnp.min(jnp.where(sel3 == m1, within, per_grp), axis=1, keepdims=True)
    m2 = jnp.max(jnp.where(within == first, neg, sel3), axis=1, keepdims=True)
    gs = (m1 + m2).reshape(N_GROUPS, tm)
    gidx = lax.broadcasted_iota(I32, gs.shape, 0)
    beaten = jnp.zeros(gs.shape, I32)
    for g2 in range(N_GROUPS):
        other = gs[g2:g2 + 1, :]
        beaten = beaten + ((other > gs) | ((other == gs) & (g2 < gidx))).astype(I32)
    keep = (beaten < TOPK_GROUPS).reshape(N_GROUPS, 1, tm)
    cur = jnp.where(keep, sel3, neg).reshape(N_EXPERTS, tm)

    eidx = lax.broadcasted_iota(I32, (N_EXPERTS, tm), 0)
    chosen = jnp.zeros((N_EXPERTS, tm), F32)
    picks, pick_scores = [], []
    for _ in range(TOP_K):
        m = jnp.max(cur, axis=0, keepdims=True)
        pick = jnp.min(jnp.where(cur == m, eidx, N_EXPERTS), axis=0, keepdims=True)
        hit = eidx == pick
        picks.append(pick)
        pick_scores.append(jnp.sum(jnp.where(hit, scores, 0.0), axis=0, keepdims=True))
        chosen = jnp.where(hit, 1.0, chosen)
        cur = jnp.where(hit, neg, cur)

    denom = pick_scores[0]
    for s in pick_scores[1:]:
        denom = denom + s
    for j in range(TOP_K):
        idx_ref[j:j + 1, :] = picks[j]
        wgt_ref[j:j + 1, :] = pick_scores[j] / denom * ROUTED_SCALE
    base_scr[...] = base_scr[...] + jnp.sum(chosen, axis=1, keepdims=True)
    cnt_ref[...] = base_scr[...]


def _router(x1, mod3, norm_w, wr_hi, wr_lo, bias, seq):
    t, d = x1.shape
    tm = 512
    per_b = seq // tm
    row_out = lambda dt: jax.ShapeDtypeStruct((TOP_K, t), dt)
    row_spec = pl.BlockSpec((TOP_K, tm), lambda i: (0, i))
    return pl.pallas_call(
        _router_kernel,
        grid=(t // tm,),
        in_specs=[
            pl.BlockSpec((tm, d), lambda i: (i, 0)),
            pl.BlockSpec((1, 6, d), lambda i: (i // per_b, 0, 0)),
            pl.BlockSpec((1, d), lambda i: (0, 0)),
            pl.BlockSpec((N_EXPERTS, d), lambda i: (0, 0)),
            pl.BlockSpec((N_EXPERTS, d), lambda i: (0, 0)),
            pl.BlockSpec((N_EXPERTS, 1), lambda i: (0, 0)),
        ],
        out_specs=[
            pl.BlockSpec((tm, d), lambda i: (i, 0)),
            pl.BlockSpec((tm * ROW_SLAB, LANES), lambda i: (i, 0)),
            row_spec, row_spec,
            pl.BlockSpec((N_EXPERTS, LANES), lambda i: (0, 0)),
        ],
        out_shape=[
            jax.ShapeDtypeStruct((t, d), BF16),
            jax.ShapeDtypeStruct((t * ROW_SLAB, LANES), U32),
            row_out(I32), row_out(F32),
            jax.ShapeDtypeStruct((N_EXPERTS, LANES), F32),
        ],
        scratch_shapes=[pltpu.VMEM((N_EXPERTS, LANES), F32)],
        compiler_params=_cparams(("arbitrary",)),
        name="router",
    )(x1, mod3, norm_w.reshape(1, d), wr_hi, wr_lo, bias.reshape(N_EXPERTS, 1))


BLOCK_SLABS = MOE_BLOCK * ROW_SLAB


def _gather_block(src_ref, h2p_ref, dst, sem):
    for i in range(MOE_BLOCK):
        src = pl.multiple_of(src_ref[0, 0, i], ROW_SLAB)
        pltpu.make_async_copy(h2p_ref.at[pl.ds(src, ROW_SLAB)], dst.at[pl.ds(i * ROW_SLAB, ROW_SLAB)],
                              sem).start(priority=i % 2)


def _scatter_block(dst_ref, src, out_ref, sem):
    for i in range(MOE_BLOCK):
        dst = pl.multiple_of(dst_ref[0, 0, i], ROW_SLAB)
        pltpu.make_async_copy(src.at[pl.ds(i * ROW_SLAB, ROW_SLAB)], out_ref.at[pl.ds(dst, ROW_SLAB)],
                              sem).start(priority=i % 2)


def _expert_kernel(be_ref, nused_ref, src0_ref, src_ref, dst_ref, h2p_ref, wg_ref, wu_ref, wd_ref, out_ref,
                   wg_scr, wu_scr, wd_scr, x_scr, xbuf, ybuf, xsem, ysem):
    s = pl.program_id(0)
    n_used = nused_ref[0]
    half = ROW_SLAB * LANES
    n_real = out_ref.shape[0] - 2 * BLOCK_SLABS
    slot = lax.rem(s, 2)
    other = 1 - slot
    x_cur = lax.rem(s + 2, 3)
    x_next = lax.rem(s + 1, 3)
    gathered = lambda sem: pltpu.make_async_copy(h2p_ref.at[pl.ds(0, BLOCK_SLABS)], xbuf.at[0], sem)
    scattered = lambda sem: pltpu.make_async_copy(ybuf.at[0], out_ref.at[pl.ds(0, BLOCK_SLABS)], sem)

    @pl.when(s == 0)
    def _():
        ybuf[...] = jnp.zeros_like(ybuf)
        for q in range(2):
            cp = pltpu.make_async_copy(ybuf.at[q], out_ref.at[pl.ds(n_real + q * BLOCK_SLABS, BLOCK_SLABS)],
                                       ysem.at[q])
            cp.start()
            cp.wait()
        pltpu.make_async_copy(ybuf.at[0], out_ref.at[pl.ds(n_real, BLOCK_SLABS)], ysem.at[0]).start()
        _gather_block(src0_ref, h2p_ref, xbuf.at[0], xsem.at[0])
        _gather_block(src_ref, h2p_ref, xbuf.at[1], xsem.at[1])

    @pl.when((s == 0) | (be_ref[s] != be_ref[jnp.maximum(s - 1, 0)]))
    def _():
        wg_scr[...] = wg_ref[...].astype(BF16)
        wu_scr[...] = wu_ref[...].astype(BF16)
        wd_scr[...] = wd_ref[...].astype(BF16)

    @pl.when((s >= 1) & (s <= n_used + 1))
    def _():
        gathered(xsem.at[x_cur]).wait()
        xcur = xbuf.at[x_cur]
        for j in range(ROW_SLAB):
            xa, xb = _unpack_pair(_load_slab_chunk(xcur, j, MOE_BLOCK))
            x_scr[:, j * LANES:(j + 1) * LANES] = xa
            x_scr[:, half + j * LANES:half + (j + 1) * LANES] = xb
        _gather_block(src_ref, h2p_ref, xbuf.at[x_next], xsem.at[x_next])
        _scatter_block(dst_ref, ybuf.at[slot], out_ref, ysem.at[slot])
        xfull = x_scr[...]
        gate = _dot(xfull, wg_scr[...])
        up = _dot(xfull, wu_scr[...])
        mid = ((gate * jax.nn.sigmoid(gate)) * up).astype(BF16)
        y = _dot(mid, wd_scr[...])
        scattered(ysem.at[other]).wait()
        _store_slabs(ybuf.at[other], _pack_pair(y[:, :half], y[:, half:]))

    @pl.when(s == n_used + 2)
    def _():
        gathered(xsem.at[x_cur]).wait()
        gathered(xsem.at[lax.rem(s, 3)]).wait()
        scattered(ysem.at[other]).wait()


def _experts(step_expert, n_used, src_rows, dst_rows, h2p, w_gate, w_up, w_down, n_out_slabs):
    n_steps = step_expert.shape[0]
    n_blocks = n_steps - 3
    d = 2 * ROW_SLAB * LANES
    wspec = lambda r, c: pl.BlockSpec((None, r, c), lambda s, be, nu: (be[s], 0, 0))
    scattered = lambda s, be, nu: (jnp.where(s >= 2, jnp.minimum(s - 2, nu[0] - 1), n_blocks), 0, 0)
    return pl.pallas_call(
        _expert_kernel,
        grid_spec=pltpu.PrefetchScalarGridSpec(
            num_scalar_prefetch=2,
            grid=(n_steps,),
            in_specs=[
                pl.BlockSpec((1, 1, MOE_BLOCK), lambda s, be, nu: (0, 0, 0), memory_space=pltpu.SMEM),
                pl.BlockSpec((1, 1, MOE_BLOCK), lambda s, be, nu: (jnp.minimum(s + 1, nu[0] - 1), 0, 0),
                             memory_space=pltpu.SMEM),
                pl.BlockSpec((1, 1, MOE_BLOCK), scattered, memory_space=pltpu.SMEM),
                pl.BlockSpec(memory_space=pl.ANY),
                wspec(d, EXPERT_DIM), wspec(d, EXPERT_DIM), wspec(EXPERT_DIM, d),
            ],
            out_specs=pl.BlockSpec(memory_space=pl.ANY),
            scratch_shapes=[pltpu.VMEM((d, EXPERT_DIM), BF16), pltpu.VMEM((d, EXPERT_DIM), BF16),
                            pltpu.VMEM((EXPERT_DIM, d), BF16), pltpu.VMEM((MOE_BLOCK, d), BF16),
                            pltpu.VMEM((3, BLOCK_SLABS, LANES), U32),
                            pltpu.VMEM((2, BLOCK_SLABS, LANES), U32),
                            pltpu.SemaphoreType.DMA((3,)), pltpu.SemaphoreType.DMA((2,))],
        ),
        out_shape=jax.ShapeDtypeStruct((n_out_slabs + 2 * BLOCK_SLABS, LANES), U32),
        compiler_params=_cparams(("arbitrary",)),
        name="experts",
    )(step_expert, n_used, src_rows, src_rows, dst_rows, h2p, w_gate, w_up, w_down)


def _combine_kernel(x1_ref, h2b_ref, wgt_ref, mod_ref, wsg_ref, wsu_ref, wsd_ref, *rest):
    slot_refs, o_ref = rest[:TOP_K], rest[TOP_K]
    tm = x1_ref.shape[0]
    h2 = h2b_ref[...]
    gate = _dot(h2, wsg_ref[...])
    up = _dot(h2, wsu_ref[...])
    shared = _dot(((gate * jax.nn.sigmoid(gate)) * up).astype(BF16), wsd_ref[...])

    half = ROW_SLAB * LANES
    wgt = wgt_ref[...]
    g2 = mod_ref[0, 5:6, :]
    for c in range(ROW_SLAB):
        ra = jnp.zeros((tm, LANES), F32)
        rb = jnp.zeros((tm, LANES), F32)
        for j in range(TOP_K):
            ya, yb = _unpack_pair(_load_slab_chunk(slot_refs[j], c, tm))
            wj = wgt[:, j:j + 1]
            ra = ra + wj * ya.astype(F32)
            rb = rb + wj * yb.astype(F32)
        lo = slice(c * LANES, (c + 1) * LANES)
        hi = slice(half + c * LANES, half + (c + 1) * LANES)
        o_ref[:, lo] = x1_ref[:, lo] + g2[:, lo] * (ra + shared[:, lo])
        o_ref[:, hi] = x1_ref[:, hi] + g2[:, hi] * (rb + shared[:, hi])


def _combine(x1, h2b, wgt_t, mod3, ws_gate, ws_up, ws_down, ys, seq):
    t, d = x1.shape
    tm = 256
    per_b = seq // tm
    n_tiles = t // tm
    slot_spec = lambda j: pl.BlockSpec((tm * ROW_SLAB, LANES), lambda i: (j * n_tiles + i, 0))
    return pl.pallas_call(
        _combine_kernel,
        grid=(n_tiles,),
        in_specs=[
            pl.BlockSpec((tm, d), lambda i: (i, 0)),
            pl.BlockSpec((tm, d), lambda i: (i, 0)),
            pl.BlockSpec((tm, TOP_K), lambda i: (i, 0)),
            pl.BlockSpec((1, 6, d), lambda i: (i // per_b, 0, 0)),
            pl.BlockSpec((d, EXPERT_DIM), lambda i: (0, 0)),
            pl.BlockSpec((d, EXPERT_DIM), lambda i: (0, 0)),
            pl.BlockSpec((EXPERT_DIM, d), lambda i: (0, 0)),
        ] + [slot_spec(j) for j in range(TOP_K)],
        out_specs=pl.BlockSpec((tm, d), lambda i: (i, 0)),
        out_shape=jax.ShapeDtypeStruct((t, d), F32),
        compiler_params=_cparams(("parallel",)),
        name="combine",
    )(x1, h2b, wgt_t, mod3, ws_gate, ws_up, ws_down, *([ys] * TOP_K))


def kernel(x, c, positions, w_ada, b_ada, norm1_w, w_in, q_lat_norm_w, w_uq, kv_lat_norm_w, w_ukv, q_norm_w, k_norm_w, lb_fwd, lb_bwd, rec_norm_w, w_out, norm2_w, w_router, router_bias, w_exp_gate, w_exp_up, w_exp_down, w_sh_gate, w_sh_up, w_sh_down):
    bsz, seq, d = x.shape
    t = bsz * seq
    x2 = x.reshape(t, d)

    w_in0 = w_in[0]
    n_lat = Q_RANK + KV_RANK + QK_ROPE
    w_lat = jnp.pad(w_in0[:, :n_lat], ((0, 0), (0, LAT_PAD - n_lat))).astype(BF16)
    w_rec = w_in0[:, n_lat:].astype(BF16)
    wuq = jnp.pad(w_uq[0].reshape(Q_RANK, N_HEADS, QK_DIM), ((0, 0), (0, 0), (0, QK_PAD - QK_DIM)))
    wuq = wuq.transpose(1, 0, 2).astype(BF16)
    wukv = w_ukv[0].reshape(KV_RANK, N_HEADS, QK_NOPE + V_DIM).transpose(1, 0, 2).astype(BF16)
    qnw = jnp.pad(q_norm_w[0], (0, QK_PAD - QK_DIM)).reshape(1, QK_PAD)
    knw = jnp.pad(k_norm_w[0], (0, QK_PAD - QK_DIM)).reshape(1, QK_PAD)
    half = N_HEADS * V_DIM
    w_out_att = w_out[0, :half].astype(BF16)
    w_out_rec = w_out[0, half:].astype(BF16)
    wr_hi, wr_lo = _split_bf16(w_router[0].T)
    lbf = jnp.cumsum(jax.nn.softmax(lb_fwd.astype(F32), axis=0), axis=0)[0].reshape(1, -1)
    lbb = jnp.cumsum(jax.nn.softmax(lb_bwd.astype(F32), axis=0), axis=0)[0].reshape(1, -1)

    inv_freq = ROPE_THETA ** (-jnp.arange(0, QK_ROPE, 2, dtype=F32) / QK_ROPE)
    ang = positions.astype(F32).reshape(t, 1) * inv_freq[None, :]
    cos_h, sin_h = jnp.cos(ang), jnp.sin(ang)
    zh = jnp.zeros_like(cos_h)
    cos_t = jnp.concatenate([cos_h, cos_h, zh, zh], axis=-1)
    sin_a = jnp.concatenate([-sin_h, zh, zh, zh], axis=-1)
    sin_b = jnp.concatenate([zh, sin_h, zh, zh], axis=-1)

    mod3 = _ada(c, w_ada[0], b_ada[0]).reshape(bsz, 6, d)
    lat, rec_in = _inproj(x2, mod3, norm1_w[0], w_lat, w_rec, seq)
    q, k, v = _mla_prep(lat, cos_t, sin_a, sin_b, q_lat_norm_w[0].reshape(1, -1), wuq,
                        kv_lat_norm_w[0].reshape(1, -1), wukv, qnw, knw, bsz, seq)
    att = _attention(q, k, v)
    rec = _hgrn(rec_in, lbf, lbb, rec_norm_w[0], bsz, seq)
    x1 = _outproj(x2, att, rec, mod3, w_out_att, w_out_rec, seq)

    h2b, h2p, top_idx, top_w, counts = _router(x1, mod3, norm2_w[0], wr_hi, wr_lo, router_bias[0], seq)
    counts = counts[:, 0].astype(I32)
    padded = (counts + MOE_BLOCK - 1) // MOE_BLOCK * MOE_BLOCK
    pad_end = jnp.cumsum(padded)
    experts = jnp.arange(N_EXPERTS, dtype=I32)
    n_blocks = -(-(t * TOP_K) // MOE_BLOCK) + N_EXPERTS
    n_used = pad_end[-1] // MOE_BLOCK
    blk_row0 = jnp.minimum(jnp.arange(n_blocks, dtype=I32), n_used - 1) * MOE_BLOCK
    blk_expert = jnp.sum((pad_end[None, :] <= blk_row0[:, None]).astype(I32), axis=-1)
    step_blk = jnp.clip(jnp.arange(n_blocks + 3, dtype=I32) - 1, 0, n_used - 1)
    step_expert = jnp.sum(jnp.where(step_blk[:, None] == jnp.arange(n_blocks, dtype=I32)[None, :],
                                    blk_expert[None, :], 0), axis=-1)
    n_assign = t * TOP_K
    filler = jnp.arange(MOE_BLOCK - 1, dtype=I32)[None, :] < (padded - counts)[:, None]
    filler_keys = jnp.where(filler, experts[:, None], N_EXPERTS).reshape(-1)
    order = jnp.argsort(jnp.concatenate([top_idx.T.reshape(-1), filler_keys]), stable=True).astype(I32)
    order = jnp.pad(order, (0, n_blocks * MOE_BLOCK - order.shape[0]), constant_values=n_assign)
    real = order < n_assign
    pos = jnp.arange(n_blocks * MOE_BLOCK, dtype=I32)
    zone_slab = n_assign + ((pos // MOE_BLOCK) % 2) * MOE_BLOCK + pos % MOE_BLOCK
    src_rows = jnp.where(real, (order // TOP_K) * ROW_SLAB, 0).reshape(n_blocks, 1, MOE_BLOCK)
    dst_rows = jnp.where(real, (order % TOP_K) * t + order // TOP_K, zone_slab) * ROW_SLAB
    all_filler = (n_assign + MOE_BLOCK + jnp.arange(MOE_BLOCK, dtype=I32)) * ROW_SLAB
    dst_rows = jnp.concatenate([dst_rows, all_filler]).reshape(n_blocks + 1, 1, MOE_BLOCK)

    ys = _experts(step_expert.astype(I32), n_used.reshape(1).astype(I32), src_rows, dst_rows, h2p,
                  w_exp_gate[0], w_exp_up[0], w_exp_down[0], n_assign * ROW_SLAB)
    out = _combine(x1, h2b, top_w.T, mod3, w_sh_gate[0].astype(BF16), w_sh_up[0].astype(BF16),
                   w_sh_down[0].astype(BF16), ys, seq)
    return out.reshape(bsz, seq, d)
```

```python
import functools

import numpy as np
import jax
import jax.numpy as jnp
from jax import lax
from jax.experimental import pallas as pl
from jax.experimental.pallas import tpu as pltpu

F32 = jnp.float32
BF16 = jnp.bfloat16
U32 = jnp.uint32
I32 = jnp.int32

N_HEADS = 8
QK_NOPE = 128
QK_ROPE = 64
QK_DIM = QK_NOPE + QK_ROPE
QK_PAD = 256
V_DIM = 128
Q_RANK = 512
KV_RANK = 256
LAT_PAD = 896
REC_DIM = 128
REC_COLS = 5 * N_HEADS * REC_DIM
ROPE_THETA = 10000.0
N_EXPERTS = 64
TOP_K = 8
N_GROUPS = 8
TOPK_GROUPS = 4
EXPERT_DIM = 512
ROUTED_SCALE = 2.5
MOE_BLOCK = 256
EPS = 1e-6

LANES = 128
SUBLANES = 8
ROW_SLAB = 8
VMEM_LIMIT = 56 * 1024 * 1024

ATT_CHUNK = 256
HG_TILE = 128
HG_BAND = 8
HG_LEVELS = (8, 16, 32, 64)


def _cparams(sem, vmem=VMEM_LIMIT):
    return pltpu.CompilerParams(dimension_semantics=sem, vmem_limit_bytes=vmem)


def _split_bf16(a):
    hi = a.astype(BF16)
    lo = (a - hi.astype(F32)).astype(BF16)
    return hi, lo


def _pack_pair(a, b):
    ua = lax.bitcast_convert_type(a.astype(BF16).astype(F32), U32)
    ub = lax.bitcast_convert_type(b.astype(BF16).astype(F32), U32)
    return ua | (ub >> 16)


def _unpack_pair(p):
    a = lax.bitcast_convert_type(p & jnp.uint32(0xFFFF0000), F32).astype(BF16)
    b = lax.bitcast_convert_type(p << 16, F32).astype(BF16)
    return a, b


def _store_slabs(ref, packed):
    n = packed.shape[0]
    for j in range(ROW_SLAB):
        ref[pl.ds(j, n, stride=ROW_SLAB), :] = packed[:, j * LANES:(j + 1) * LANES]


def _load_slab_chunk(ref, j, n):
    return ref[pl.ds(j, n, stride=ROW_SLAB), :]


def _dot(a, b):
    return jnp.dot(a, b, preferred_element_type=F32)


def _dot_nt(a, b):
    return lax.dot_general(a, b, (((1,), (1,)), ((), ())), preferred_element_type=F32)


def _dot_tn(a, b):
    return lax.dot_general(a, b, (((0,), (0,)), ((), ())), preferred_element_type=F32)


def _ada_kernel(c_ref, w_ref, b_ref, o_ref):
    c = c_ref[...]
    cond = c * jax.nn.sigmoid(c)
    o_ref[...] = _dot(cond.astype(BF16), w_ref[...].astype(BF16)) + b_ref[...]


def _ada(c, w_ada, b_ada):
    bsz, d = c.shape
    n = w_ada.shape[1]
    tn = 1024
    return pl.pallas_call(
        _ada_kernel,
        grid=(n // tn,),
        in_specs=[
            pl.BlockSpec((bsz, d), lambda j: (0, 0)),
            pl.BlockSpec((d, tn), lambda j: (0, j)),
            pl.BlockSpec((1, tn), lambda j: (0, j)),
        ],
        out_specs=pl.BlockSpec((bsz, tn), lambda j: (0, j)),
        out_shape=jax.ShapeDtypeStruct((bsz, n), F32),
        compiler_params=_cparams(("arbitrary",)),
        name="ada",
    )(c, w_ada, b_ada.reshape(1, n))


def _modulated_norm(x, mod_ref, nw, shift_row, scale_row):
    r = lax.rsqrt(jnp.mean(x * x, axis=-1, keepdims=True) + EPS)
    scale = mod_ref[0, scale_row:scale_row + 1, :]
    shift = mod_ref[0, shift_row:shift_row + 1, :]
    return (x * r * nw) * (1.0 + scale) + shift


def _inproj_kernel(x_ref, mod_ref, nw_ref, wlat_ref, wrec_ref, lat_ref, rec_ref, h_scr):
    @pl.when(pl.program_id(1) == 0)
    def _():
        h = _modulated_norm(x_ref[...], mod_ref, nw_ref[...], 0, 1).astype(BF16)
        h_scr[...] = h
        lat_ref[...] = _dot(h, wlat_ref[...])

    rec_ref[...] = _dot(h_scr[...], wrec_ref[...])


def _inproj(x2, mod3, norm_w, w_lat, w_rec, seq):
    t, d = x2.shape
    tm, tn = 1024, 512
    per_b = seq // tm
    return pl.pallas_call(
        _inproj_kernel,
        grid=(t // tm, REC_COLS // tn),
        in_specs=[
            pl.BlockSpec((tm, d), lambda i, j: (i, 0)),
            pl.BlockSpec((1, 6, d), lambda i, j: (i // per_b, 0, 0)),
            pl.BlockSpec((1, d), lambda i, j: (0, 0)),
            pl.BlockSpec((d, LAT_PAD), lambda i, j: (0, 0)),
            pl.BlockSpec((d, tn), lambda i, j: (0, j)),
        ],
        out_specs=[
            pl.BlockSpec((tm, LAT_PAD), lambda i, j: (i, 0)),
            pl.BlockSpec((tm, tn), lambda i, j: (i, j)),
        ],
        out_shape=[
            jax.ShapeDtypeStruct((t, LAT_PAD), F32),
            jax.ShapeDtypeStruct((t, REC_COLS), F32),
        ],
        scratch_shapes=[pltpu.VMEM((tm, d), BF16)],
        compiler_params=_cparams(("parallel", "arbitrary")),
        name="inproj",
    )(x2, mod3, norm_w.reshape(1, d), w_lat, w_rec)


def _rope(x2, cos, sin_a, sin_b):
    return (x2 * cos + pltpu.roll(x2, LANES - QK_ROPE // 2, 1) * sin_a
            + pltpu.roll(x2, QK_ROPE // 2, 1) * sin_b)


def _mla_prep_kernel(lat_ref, cos_ref, sa_ref, sb_ref, qlw_ref, wuq_ref, kvlw_ref, wukv_ref,
                     qnw_ref, knw_ref, q_ref, k_ref, v_ref):
    lat = lat_ref[...]
    ql = lat[:, :Q_RANK]
    kvl = lat[:, Q_RANK:Q_RANK + KV_RANK]
    kr = lat[:, Q_RANK + KV_RANK:]
    qn = (ql * lax.rsqrt(jnp.mean(ql * ql, -1, keepdims=True) + EPS) * qlw_ref[...]).astype(BF16)
    kvn = (kvl * lax.rsqrt(jnp.mean(kvl * kvl, -1, keepdims=True) + EPS) * kvlw_ref[...]).astype(BF16)
    cos, sa, sb = cos_ref[...], sa_ref[...], sb_ref[...]
    qnw = qnw_ref[...]
    knw = knw_ref[...]
    kr_ss = jnp.sum(kr * kr, -1, keepdims=True)
    kr_roped = _rope(kr * knw[:, QK_NOPE:], cos, sa, sb)
    scale = QK_DIM ** -0.5
    for h in range(N_HEADS):
        qh = _dot(qn, wuq_ref[h])
        r = lax.rsqrt(jnp.sum(qh * qh, -1, keepdims=True) * (1.0 / QK_DIM) + EPS) * scale
        qh = qh * r * qnw
        q_ref[0, h, :, :QK_NOPE] = qh[:, :QK_NOPE].astype(BF16)
        q_ref[0, h, :, QK_NOPE:] = _rope(qh[:, QK_NOPE:], cos, sa, sb).astype(BF16)
        kvh = _dot(kvn, wukv_ref[h])
        kn = kvh[:, :QK_NOPE]
        rk = lax.rsqrt((jnp.sum(kn * kn, -1, keepdims=True) + kr_ss) * (1.0 / QK_DIM) + EPS)
        k_ref[0, h, :, :QK_NOPE] = (kn * rk * knw[:, :QK_NOPE]).astype(BF16)
        k_ref[0, h, :, QK_NOPE:] = (kr_roped * rk).astype(BF16)
        v_ref[0, h] = kvh[:, QK_NOPE:].astype(BF16)


def _mla_prep(lat, cos, sa, sb, qlw, wuq, kvlw, wukv, qnw, knw, bsz, seq):
    tm = 512
    per_b = seq // tm
    const2 = lambda i: (0, 0)
    const3 = lambda i: (0, 0, 0)
    head_out = lambda w: pl.BlockSpec((1, N_HEADS, tm, w), lambda i: (i // per_b, 0, i % per_b, 0))
    return pl.pallas_call(
        _mla_prep_kernel,
        grid=(bsz * seq // tm,),
        in_specs=[
            pl.BlockSpec((tm, LAT_PAD), lambda i: (i, 0)),
            pl.BlockSpec((tm, LANES), lambda i: (i, 0)),
            pl.BlockSpec((tm, LANES), lambda i: (i, 0)),
            pl.BlockSpec((tm, LANES), lambda i: (i, 0)),
            pl.BlockSpec((1, Q_RANK), const2),
            pl.BlockSpec((N_HEADS, Q_RANK, QK_PAD), const3),
            pl.BlockSpec((1, KV_RANK), const2),
            pl.BlockSpec((N_HEADS, KV_RANK, QK_PAD), const3),
            pl.BlockSpec((1, QK_PAD), const2),
            pl.BlockSpec((1, QK_PAD), const2),
        ],
        out_specs=[head_out(QK_PAD), head_out(QK_PAD), head_out(V_DIM)],
        out_shape=[
            jax.ShapeDtypeStruct((bsz, N_HEADS, seq, QK_PAD), BF16),
            jax.ShapeDtypeStruct((bsz, N_HEADS, seq, QK_PAD), BF16),
            jax.ShapeDtypeStruct((bsz, N_HEADS, seq, V_DIM), BF16),
        ],
        compiler_params=_cparams(("parallel",)),
        name="mla_prep",
    )(lat, cos, sa, sb, qlw, wuq, kvlw, wukv, qnw, knw)


def _attn_kernel(q_ref, k_ref, v_ref, o_ref):
    k = k_ref[0, 0]
    v = v_ref[0, 0]
    for c in range(q_ref.shape[2] // ATT_CHUNK):
        rows = pl.ds(c * ATT_CHUNK, ATT_CHUNK)
        s = _dot_nt(q_ref[0, 0, rows, :], k)
        m = jnp.max(s, axis=-1, keepdims=True)
        p = jnp.exp(s - m)
        l = jnp.sum(p, axis=-1, keepdims=True)
        o_ref[rows, :] = _dot(p.astype(BF16), v) / l


def _attention(q, k, v):
    bsz, nh, seq, _ = q.shape
    tq = 1024
    nq = seq // tq
    return pl.pallas_call(
        _attn_kernel,
        grid=(bsz, nh, nq),
        in_specs=[
            pl.BlockSpec((1, 1, tq, QK_PAD), lambda b, h, i: (b, h, i, 0)),
            pl.BlockSpec((1, 1, seq, QK_PAD), lambda b, h, i: (b, h, 0, 0)),
            pl.BlockSpec((1, 1, seq, V_DIM), lambda b, h, i: (b, h, 0, 0)),
        ],
        out_specs=pl.BlockSpec((tq, V_DIM), lambda b, h, i: (b * nq + i, h)),
        out_shape=jax.ShapeDtypeStruct((bsz * seq, nh * V_DIM), F32),
        compiler_params=_cparams(("parallel", "parallel", "arbitrary")),
        name="attention",
    )(q, k, v)


def _hgrn_constants():
    n = HG_TILE
    t = np.arange(n)[:, None]
    u = np.arange(n)[None, :]
    tri = np.concatenate([(u <= t), (u >= t)], axis=0).astype(np.float32)
    lvl = np.zeros((n, n), np.float32)
    for li, m in enumerate(HG_LEVELS):
        lvl[((t // (2 * m)) == (u // (2 * m))) & ((t // m) != (u // m))] = li + 1
    ones2 = np.zeros((2 * LANES, 2 * LANES), np.float32)
    ones2[:LANES, :LANES] = 1.0
    ones2[LANES:, LANES:] = 1.0
    return tri, lvl, ones2


def _level_stacks(m, q, kkf, kkb, bf, cb):
    zero = jnp.zeros((m, LANES), F32)
    q_rows, k_rows = [], []
    for a in range(0, HG_TILE, 2 * m):
        e, o = slice(a, a + m), slice(a + m, a + 2 * m)
        ref_f = bf[a + m - 1:a + m, :]
        ref_b = cb[a + m:a + m + 1, :]
        wf_e = jnp.exp(ref_f - bf[e])
        wf_o = jnp.exp(bf[o] - ref_f)
        wb_e = jnp.exp(cb[e] - ref_b)
        wb_o = jnp.exp(ref_b - cb[o])
        q_rows.append(jnp.concatenate([zero, q[e] * wb_e], axis=1))
        q_rows.append(jnp.concatenate([q[o] * wf_o, zero], axis=1))
        k_rows.append(jnp.concatenate([kkf[e] * wf_e, zero], axis=1))
        k_rows.append(jnp.concatenate([zero, kkb[o] * wb_o], axis=1))
    return (jnp.concatenate(q_rows, axis=0).astype(BF16), jnp.concatenate(k_rows, axis=0).astype(BF16))


def _hgrn_kernel(q_ref, zf_ref, zb_ref, v_ref, gate_ref, lbf_ref, lbb_ref, nw_ref,
                 tri_ref, lvl_ref, ones_ref, o_ref,
                 ff_scr, fb_scr, kf_scr, kb_scr, v_scr, o_scr,
                 qe_scr, u_scr, dec_scr, st_scr):
    seq = q_ref.shape[0]
    n_tiles = seq // HG_TILE
    pad = HG_BAND
    zeros_pad = jnp.zeros((pad, LANES), F32)
    for scr in (ff_scr, fb_scr, kf_scr, kb_scr, v_scr):
        scr[pl.ds(0, pad), :] = zeros_pad
        scr[pl.ds(pad + seq, pad), :] = zeros_pad
    v_scr[pl.ds(pad, seq), :] = v_ref[...]
    pos = lax.broadcasted_iota(I32, (HG_TILE, 1), 0) % HG_BAND
    lbf = lbf_ref[...]
    lbb = lbb_ref[...]

    def tile_body(i, carry):
        t0 = pl.multiple_of(i * HG_TILE, HG_TILE)
        q = q_ref[pl.ds(t0, HG_TILE), :]
        v = v_scr[pl.ds(pad + t0, HG_TILE), :]
        v_b = v.astype(BF16)
        f_f = lbf + (1.0 - lbf) * jax.nn.sigmoid(zf_ref[pl.ds(t0, HG_TILE), :])
        f_b = lbb + (1.0 - lbb) * jax.nn.sigmoid(zb_ref[pl.ds(t0, HG_TILE), :])
        kkf = 1.0 - f_f
        kkb = 1.0 - f_b
        kf_scr[pl.ds(pad + t0, HG_TILE), :] = kkf
        kb_scr[pl.ds(pad + t0, HG_TILE), :] = kkb
        ff_scr[pl.ds(pad + t0, HG_TILE), :] = jnp.where(pos == 0, 0.0, f_f)
        fb_scr[pl.ds(pad + t0, HG_TILE), :] = jnp.where(pos == HG_BAND - 1, 0.0, f_b)
        kb_scr[pl.ds(pad + t0 + HG_TILE, pad), :] = zeros_pad
        fb_scr[pl.ds(pad + t0 + HG_TILE, pad), :] = zeros_pad
        gf_hi, gf_lo = _split_bf16(jnp.log(f_f))
        gb_hi, gb_lo = _split_bf16(jnp.log(f_b))
        pf = _dot(tri_ref[pl.ds(0, HG_TILE), :], jnp.concatenate([gf_hi, gf_lo], axis=1))
        pb = _dot(tri_ref[pl.ds(HG_TILE, HG_TILE), :], jnp.concatenate([gb_hi, gb_lo], axis=1))
        bf = pf[:, :LANES] + pf[:, LANES:]
        cb = pb[:, :LANES] + pb[:, LANES:]

        last_f = bf[HG_TILE - 1:HG_TILE, :]
        first_b = cb[0:1, :]
        wq_f = jnp.exp(bf)
        wq_b = jnp.exp(cb)
        ke = jnp.concatenate([kkf * jnp.exp(last_f - bf), kkb * jnp.exp(first_b - cb)], axis=1)
        u_scr[i] = _dot_tn(v_b, ke.astype(BF16))
        qe_scr[i] = jnp.concatenate([q * wq_f, q * wq_b], axis=1).astype(BF16)
        dec_scr[pl.ds(i, 1), :] = jnp.concatenate([wq_f[HG_TILE - 1:HG_TILE, :], wq_b[0:1, :]], axis=1)

        lvl = lvl_ref[...]
        a = jnp.zeros((HG_TILE, HG_TILE), F32)
        for li, m in enumerate(HG_LEVELS):
            qs, ks = _level_stacks(m, q, kkf, kkb, bf, cb)
            a = jnp.where(lvl == float(li + 1), _dot_nt(qs, ks), a)
        o = _dot(a.astype(BF16), v_b)

        ones2 = ones_ref[...]
        pf_run = None
        pb_run = None
        for d in range(HG_BAND):
            if d == 1:
                pf_run = ff_scr[pl.ds(pad + t0, HG_TILE), :]
                pb_run = fb_scr[pl.ds(pad + t0, HG_TILE), :]
            elif d > 1:
                pf_run = pf_run * ff_scr[pl.ds(pad + t0 - (d - 1), HG_TILE), :]
                pb_run = pb_run * fb_scr[pl.ds(pad + t0 + (d - 1), HG_TILE), :]
            p_f = q * kf_scr[pl.ds(pad + t0 - d, HG_TILE), :]
            p_b = q * kb_scr[pl.ds(pad + t0 + d, HG_TILE), :]
            if d > 0:
                p_f = p_f * pf_run
                p_b = p_b * pb_run
            rs = _dot(jnp.concatenate([p_f, p_b], axis=1).astype(BF16), ones2)
            o = (o + rs[:, :LANES] * v_scr[pl.ds(pad + t0 - d, HG_TILE), :]
                 + rs[:, LANES:] * v_scr[pl.ds(pad + t0 + d, HG_TILE), :])
        o_scr[pl.ds(t0, HG_TILE), :] = o
        return carry

    lax.fori_loop(0, n_tiles, tile_body, 0, unroll=2)

    def scan_f(i, state):
        st_scr[i] = state.astype(BF16)
        return state * dec_scr[pl.ds(i, 1), :][:, :LANES] + u_scr[i][:, :LANES]

    lax.fori_loop(0, n_tiles, scan_f, jnp.zeros((V_DIM, REC_DIM), F32))

    def scan_b(n, state):
        i = n_tiles - 1 - n
        t0 = pl.multiple_of(i * HG_TILE, HG_TILE)
        both = jnp.concatenate([st_scr[i], state.astype(BF16)], axis=1)
        o_scr[pl.ds(t0, HG_TILE), :] = o_scr[pl.ds(t0, HG_TILE), :] + _dot_nt(qe_scr[i], both)
        return state * dec_scr[pl.ds(i, 1), :][:, LANES:] + u_scr[i][:, LANES:]

    lax.fori_loop(0, n_tiles, scan_b, jnp.zeros((V_DIM, REC_DIM), F32), unroll=4)

    o = o_scr[...]
    o = o * lax.rsqrt(jnp.mean(o * o, -1, keepdims=True) + EPS) * nw_ref[...]
    gate = gate_ref[...]
    o_ref[...] = o * (gate * jax.nn.sigmoid(gate))


def _hgrn(rec, lbf, lbb, rec_norm_w, bsz, seq):
    tri, lvl, ones2 = _hgrn_constants()
    n_tiles = seq // HG_TILE
    col = lambda c: pl.BlockSpec((seq, REC_DIM), lambda b, h: (b, c * N_HEADS + h))
    vec = pl.BlockSpec((1, REC_DIM), lambda b, h: (0, h))
    const = lambda r, c: pl.BlockSpec((r, c), lambda b, h: (0, 0))
    padded = pltpu.VMEM((seq + 2 * HG_BAND, LANES), F32)
    plain = pltpu.VMEM((seq, LANES), F32)
    return pl.pallas_call(
        _hgrn_kernel,
        grid=(bsz, N_HEADS),
        in_specs=[col(0), col(1), col(2), col(3), col(4), vec, vec, const(1, REC_DIM),
                  const(2 * HG_TILE, HG_TILE), const(HG_TILE, HG_TILE), const(2 * LANES, 2 * LANES)],
        out_specs=pl.BlockSpec((seq, REC_DIM), lambda b, h: (b, h)),
        out_shape=jax.ShapeDtypeStruct((bsz * seq, N_HEADS * REC_DIM), F32),
        scratch_shapes=[padded, padded, padded, padded, padded, plain,
                        pltpu.VMEM((n_tiles, HG_TILE, 2 * LANES), BF16),
                        pltpu.VMEM((n_tiles, V_DIM, 2 * LANES), F32),
                        pltpu.VMEM((n_tiles, 2 * LANES), F32),
                        pltpu.VMEM((n_tiles, V_DIM, LANES), BF16)],
        compiler_params=_cparams(("parallel", "parallel")),
        name="hgrn",
    )(rec, rec, rec, rec, rec, lbf, lbb, rec_norm_w.reshape(1, REC_DIM),
      jnp.asarray(tri, BF16), jnp.asarray(lvl, F32), jnp.asarray(ones2, BF16))


def _outproj_kernel(x_ref, att_ref, rec_ref, mod_ref, wa_ref, wr_ref, o_ref):
    y = _dot(att_ref[...].astype(BF16), wa_ref[...]) + _dot(rec_ref[...].astype(BF16), wr_ref[...])
    o_ref[...] = x_ref[...] + mod_ref[0, 2:3, :] * y


def _outproj(x2, att, rec, mod3, w_att, w_rec, seq):
    t, d = x2.shape
    tm = 512
    per_b = seq // tm
    half = att.shape[1]
    return pl.pallas_call(
        _outproj_kernel,
        grid=(t // tm,),
        in_specs=[
            pl.BlockSpec((tm, d), lambda i: (i, 0)),
            pl.BlockSpec((tm, half), lambda i: (i, 0)),
            pl.BlockSpec((tm, half), lambda i: (i, 0)),
            pl.BlockSpec((1, 6, d), lambda i: (i // per_b, 0, 0)),
            pl.BlockSpec((half, d), lambda i: (0, 0)),
            pl.BlockSpec((half, d), lambda i: (0, 0)),
        ],
        out_specs=pl.BlockSpec((tm, d), lambda i: (i, 0)),
        out_shape=jax.ShapeDtypeStruct((t, d), F32),
        compiler_params=_cparams(("parallel",)),
        name="outproj",
    )(x2, att, rec, mod3, w_att, w_rec)


def _router_kernel(x_ref, mod_ref, nw_ref, wrh_ref, wrl_ref, bias_ref,
                   h2b_ref, h2p_ref, idx_ref, wgt_ref, cnt_ref, base_scr):
    i = pl.program_id(0)
    tm = x_ref.shape[0]
    half = x_ref.shape[1] // 2

    @pl.when(i == 0)
    def _():
        base_scr[...] = jnp.zeros_like(base_scr)

    h2 = _modulated_norm(x_ref[...], mod_ref, nw_ref[...], 3, 4)
    h_hi, h_lo = _split_bf16(h2)
    h2b_ref[...] = h_hi
    _store_slabs(h2p_ref, _pack_pair(h2[:, :half], h2[:, half:]))
    logits = _dot_nt(wrh_ref[...], h_hi) + _dot_nt(wrh_ref[...], h_lo) + _dot_nt(wrl_ref[...], h_hi)
    scores = jax.nn.sigmoid(logits)
    sel = scores + bias_ref[...]

    neg = -jnp.inf
    per_grp = N_EXPERTS // N_GROUPS
    sel3 = sel.reshape(N_GROUPS, per_grp, tm)
    within = lax.broadcasted_iota(I32, sel3.shape, 1)
    m1 = jnp.max(sel3, axis=1, keepdims=True)
    first = jnp.min(jnp.where(sel3 == m1, within, per_grp), axis=1, keepdims=True)
    m2 = jnp.max(jnp.where(within == first, neg, sel3), axis=1, keepdims=True)
    gs = (m1 + m2).reshape(N_GROUPS, tm)
    gidx = lax.broadcasted_iota(I32, gs.shape, 0)
    beaten = jnp.zeros(gs.shape, I32)
    for g2 in range(N_GROUPS):
        other = gs[g2:g2 + 1, :]
        beaten = beaten + ((other > gs) | ((other == gs) & (g2 < gidx))).astype(I32)
    keep = (beaten < TOPK_GROUPS).reshape(N_GROUPS, 1, tm)
    cur = jnp.where(keep, sel3, neg).reshape(N_EXPERTS, tm)

    eidx = lax.broadcasted_iota(I32, (N_EXPERTS, tm), 0)
    chosen = jnp.zeros((N_EXPERTS, tm), F32)
    picks, pick_scores = [], []
    for _ in range(TOP_K):
        m = jnp.max(cur, axis=0, keepdims=True)
        pick = jnp.min(jnp.where(cur == m, eidx, N_EXPERTS), axis=0, keepdims=True)
        hit = eidx == pick
        picks.append(pick)
        pick_scores.append(jnp.sum(jnp.where(hit, scores, 0.0), axis=0, keepdims=True))
        chosen = jnp.where(hit, 1.0, chosen)
        cur = jnp.where(hit, neg, cur)

    denom = pick_scores[0]
    for s in pick_scores[1:]:
        denom = denom + s
    for j in range(TOP_K):
        idx_ref[j:j + 1, :] = picks[j]
        wgt_ref[j:j + 1, :] = pick_scores[j] / denom * ROUTED_SCALE
    base_scr[...] = base_scr[...] + jnp.sum(chosen, axis=1, keepdims=True)
    cnt_ref[...] = base_scr[...]


def _router(x1, mod3, norm_w, wr_hi, wr_lo, bias, seq):
    t, d = x1.shape
    tm = 512
    per_b = seq // tm
    row_out = lambda dt: jax.ShapeDtypeStruct((TOP_K, t), dt)
    row_spec = pl.BlockSpec((TOP_K, tm), lambda i: (0, i))
    return pl.pallas_call(
        _router_kernel,
        grid=(t // tm,),
        in_specs=[
            pl.BlockSpec((tm, d), lambda i: (i, 0)),
            pl.BlockSpec((1, 6, d), lambda i: (i // per_b, 0, 0)),
            pl.BlockSpec((1, d), lambda i: (0, 0)),
            pl.BlockSpec((N_EXPERTS, d), lambda i: (0, 0)),
            pl.BlockSpec((N_EXPERTS, d), lambda i: (0, 0)),
            pl.BlockSpec((N_EXPERTS, 1), lambda i: (0, 0)),
        ],
        out_specs=[
            pl.BlockSpec((tm, d), lambda i: (i, 0)),
            pl.BlockSpec((tm * ROW_SLAB, LANES), lambda i: (i, 0)),
            row_spec, row_spec,
            pl.BlockSpec((N_EXPERTS, LANES), lambda i: (0, 0)),
        ],
        out_shape=[
            jax.ShapeDtypeStruct((t, d), BF16),
            jax.ShapeDtypeStruct((t * ROW_SLAB, LANES), U32),
            row_out(I32), row_out(F32),
            jax.ShapeDtypeStruct((N_EXPERTS, LANES), F32),
        ],
        scratch_shapes=[pltpu.VMEM((N_EXPERTS, LANES), F32)],
        compiler_params=_cparams(("arbitrary",)),
        name="router",
    )(x1, mod3, norm_w.reshape(1, d), wr_hi, wr_lo, bias.reshape(N_EXPERTS, 1))


BLOCK_SLABS = MOE_BLOCK * ROW_SLAB


def _gather_block(src_ref, h2p_ref, dst, sem):
    for i in range(MOE_BLOCK):
        src = pl.multiple_of(src_ref[0, 0, i], ROW_SLAB)
        pltpu.make_async_copy(h2p_ref.at[pl.ds(src, ROW_SLAB)], dst.at[pl.ds(i * ROW_SLAB, ROW_SLAB)],
                              sem).start(priority=i % 2)


def _scatter_block(dst_ref, src, out_ref, sem):
    for i in range(MOE_BLOCK):
        dst = pl.multiple_of(dst_ref[0, 0, i], ROW_SLAB)
        pltpu.make_async_copy(src.at[pl.ds(i * ROW_SLAB, ROW_SLAB)], out_ref.at[pl.ds(dst, ROW_SLAB)],
                              sem).start(priority=i % 2)


def _expert_kernel(be_ref, nused_ref, src0_ref, src_ref, dst_ref, h2p_ref, wg_ref, wu_ref, wd_ref, out_ref,
                   wg_scr, wu_scr, wd_scr, x_scr, xbuf, ybuf, xsem, ysem):
    s = pl.program_id(0)
    n_used = nused_ref[0]
    half = ROW_SLAB * LANES
    n_real = out_ref.shape[0] - 2 * BLOCK_SLABS
    slot = lax.rem(s, 2)
    other = 1 - slot
    x_cur = lax.rem(s + 2, 3)
    x_next = lax.rem(s + 1, 3)
    gathered = lambda sem: pltpu.make_async_copy(h2p_ref.at[pl.ds(0, BLOCK_SLABS)], xbuf.at[0], sem)
    scattered = lambda sem: pltpu.make_async_copy(ybuf.at[0], out_ref.at[pl.ds(0, BLOCK_SLABS)], sem)

    @pl.when(s == 0)
    def _():
        ybuf[...] = jnp.zeros_like(ybuf)
        for q in range(2):
            cp = pltpu.make_async_copy(ybuf.at[q], out_ref.at[pl.ds(n_real + q * BLOCK_SLABS, BLOCK_SLABS)],
                                       ysem.at[q])
            cp.start()
            cp.wait()
        pltpu.make_async_copy(ybuf.at[0], out_ref.at[pl.ds(n_real, BLOCK_SLABS)], ysem.at[0]).start()
        _gather_block(src0_ref, h2p_ref, xbuf.at[0], xsem.at[0])
        _gather_block(src_ref, h2p_ref, xbuf.at[1], xsem.at[1])

    @pl.when((s == 0) | (be_ref[s] != be_ref[jnp.maximum(s - 1, 0)]))
    def _():
        wg_scr[...] = wg_ref[...].astype(BF16)
        wu_scr[...] = wu_ref[...].astype(BF16)
        wd_scr[...] = wd_ref[...].astype(BF16)

    @pl.when((s >= 1) & (s <= n_used + 1))
    def _():
        gathered(xsem.at[x_cur]).wait()
        xcur = xbuf.at[x_cur]
        for j in range(ROW_SLAB):
            xa, xb = _unpack_pair(_load_slab_chunk(xcur, j, MOE_BLOCK))
            x_scr[:, j * LANES:(j + 1) * LANES] = xa
            x_scr[:, half + j * LANES:half + (j + 1) * LANES] = xb
        _gather_block(src_ref, h2p_ref, xbuf.at[x_next], xsem.at[x_next])
        _scatter_block(dst_ref, ybuf.at[slot], out_ref, ysem.at[slot])
        xfull = x_scr[...]
        gate = _dot(xfull, wg_scr[...])
        up = _dot(xfull, wu_scr[...])
        mid = ((gate * jax.nn.sigmoid(gate)) * up).astype(BF16)
        y = _dot(mid, wd_scr[...])
        scattered(ysem.at[other]).wait()
        _store_slabs(ybuf.at[other], _pack_pair(y[:, :half], y[:, half:]))

    @pl.when(s == n_used + 2)
    def _():
        gathered(xsem.at[x_cur]).wait()
        gathered(xsem.at[lax.rem(s, 3)]).wait()
        scattered(ysem.at[other]).wait()


def _experts(step_expert, n_used, src_rows, dst_rows, h2p, w_gate, w_up, w_down, n_out_slabs):
    n_steps = step_expert.shape[0]
    n_blocks = n_steps - 3
    d = 2 * ROW_SLAB * LANES
    wspec = lambda r, c: pl.BlockSpec((None, r, c), lambda s, be, nu: (be[s], 0, 0))
    scattered = lambda s, be, nu: (jnp.where(s >= 2, jnp.minimum(s - 2, nu[0] - 1), n_blocks), 0, 0)
    return pl.pallas_call(
        _expert_kernel,
        grid_spec=pltpu.PrefetchScalarGridSpec(
            num_scalar_prefetch=2,
            grid=(n_steps,),
            in_specs=[
                pl.BlockSpec((1, 1, MOE_BLOCK), lambda s, be, nu: (0, 0, 0), memory_space=pltpu.SMEM),
                pl.BlockSpec((1, 1, MOE_BLOCK), lambda s, be, nu: (jnp.minimum(s + 1, nu[0] - 1), 0, 0),
                             memory_space=pltpu.SMEM),
                pl.BlockSpec((1, 1, MOE_BLOCK), scattered, memory_space=pltpu.SMEM),
                pl.BlockSpec(memory_space=pl.ANY),
                wspec(d, EXPERT_DIM), wspec(d, EXPERT_DIM), wspec(EXPERT_DIM, d),
            ],
            out_specs=pl.BlockSpec(memory_space=pl.ANY),
            scratch_shapes=[pltpu.VMEM((d, EXPERT_DIM), BF16), pltpu.VMEM((d, EXPERT_DIM), BF16),
                            pltpu.VMEM((EXPERT_DIM, d), BF16), pltpu.VMEM((MOE_BLOCK, d), BF16),
                            pltpu.VMEM((3, BLOCK_SLABS, LANES), U32),
                            pltpu.VMEM((2, BLOCK_SLABS, LANES), U32),
                            pltpu.SemaphoreType.DMA((3,)), pltpu.SemaphoreType.DMA((2,))],
        ),
        out_shape=jax.ShapeDtypeStruct((n_out_slabs + 2 * BLOCK_SLABS, LANES), U32),
        compiler_params=_cparams(("arbitrary",)),
        name="experts",
    )(step_expert, n_used, src_rows, src_rows, dst_rows, h2p, w_gate, w_up, w_down)


def _combine_kernel(x1_ref, h2b_ref, wgt_ref, mod_ref, wsg_ref, wsu_ref, wsd_ref, *rest):
    slot_refs, o_ref = rest[:TOP_K], rest[TOP_K]
    tm = x1_ref.shape[0]
    h2 = h2b_ref[...]
    gate = _dot(h2, wsg_ref[...])
    up = _dot(h2, wsu_ref[...])
    shared = _dot(((gate * jax.nn.sigmoid(gate)) * up).astype(BF16), wsd_ref[...])

    half = ROW_SLAB * LANES
    wgt = wgt_ref[...]
    g2 = mod_ref[0, 5:6, :]
    for c in range(ROW_SLAB):
        ra = jnp.zeros((tm, LANES), F32)
        rb = jnp.zeros((tm, LANES), F32)
        for j in range(TOP_K):
            ya, yb = _unpack_pair(_load_slab_chunk(slot_refs[j], c, tm))
            wj = wgt[:, j:j + 1]
            ra = ra + wj * ya.astype(F32)
            rb = rb + wj * yb.astype(F32)
        lo = slice(c * LANES, (c + 1) * LANES)
        hi = slice(half + c * LANES, half + (c + 1) * LANES)
        o_ref[:, lo] = x1_ref[:, lo] + g2[:, lo] * (ra + shared[:, lo])
        o_ref[:, hi] = x1_ref[:, hi] + g2[:, hi] * (rb + shared[:, hi])


def _combine(x1, h2b, wgt_t, mod3, ws_gate, ws_up, ws_down, ys, seq):
    t, d = x1.shape
    tm = 256
    per_b = seq // tm
    n_tiles = t // tm
    slot_spec = lambda j: pl.BlockSpec((tm * ROW_SLAB, LANES), lambda i: (j * n_tiles + i, 0))
    return pl.pallas_call(
        _combine_kernel,
        grid=(n_tiles,),
        in_specs=[
            pl.BlockSpec((tm, d), lambda i: (i, 0)),
            pl.BlockSpec((tm, d), lambda i: (i, 0)),
            pl.BlockSpec((tm, TOP_K), lambda i: (i, 0)),
            pl.BlockSpec((1, 6, d), lambda i: (i // per_b, 0, 0)),
            pl.BlockSpec((d, EXPERT_DIM), lambda i: (0, 0)),
            pl.BlockSpec((d, EXPERT_DIM), lambda i: (0, 0)),
            pl.BlockSpec((EXPERT_DIM, d), lambda i: (0, 0)),
        ] + [slot_spec(j) for j in range(TOP_K)],
        out_specs=pl.BlockSpec((tm, d), lambda i: (i, 0)),
        out_shape=jax.ShapeDtypeStruct((t, d), F32),
        compiler_params=_cparams(("parallel",)),
        name="combine",
    )(x1, h2b, wgt_t, mod3, ws_gate, ws_up, ws_down, *([ys] * TOP_K))


def kernel(x, c, positions, w_ada, b_ada, norm1_w, w_in, q_lat_norm_w, w_uq, kv_lat_norm_w, w_ukv, q_norm_w, k_norm_w, lb_fwd, lb_bwd, rec_norm_w, w_out, norm2_w, w_router, router_bias, w_exp_gate, w_exp_up, w_exp_down, w_sh_gate, w_sh_up, w_sh_down):
    bsz, seq, d = x.shape
    t = bsz * seq
    x2 = x.reshape(t, d)

    w_in0 = w_in[0]
    n_lat = Q_RANK + KV_RANK + QK_ROPE
    w_lat = jnp.pad(w_in0[:, :n_lat], ((0, 0), (0, LAT_PAD - n_lat))).astype(BF16)
    w_rec = w_in0[:, n_lat:].astype(BF16)
    wuq = jnp.pad(w_uq[0].reshape(Q_RANK, N_HEADS, QK_DIM), ((0, 0), (0, 0), (0, QK_PAD - QK_DIM)))
    wuq = wuq.transpose(1, 0, 2).astype(BF16)
    wukv = w_ukv[0].reshape(KV_RANK, N_HEADS, QK_NOPE + V_DIM).transpose(1, 0, 2).astype(BF16)
    qnw = jnp.pad(q_norm_w[0], (0, QK_PAD - QK_DIM)).reshape(1, QK_PAD)
    knw = jnp.pad(k_norm_w[0], (0, QK_PAD - QK_DIM)).reshape(1, QK_PAD)
    half = N_HEADS * V_DIM
    w_out_att = w_out[0, :half].astype(BF16)
    w_out_rec = w_out[0, half:].astype(BF16)
    wr_hi, wr_lo = _split_bf16(w_router[0].T)
    lbf = jnp.cumsum(jax.nn.softmax(lb_fwd.astype(F32), axis=0), axis=0)[0].reshape(1, -1)
    lbb = jnp.cumsum(jax.nn.softmax(lb_bwd.astype(F32), axis=0), axis=0)[0].reshape(1, -1)

    inv_freq = ROPE_THETA ** (-jnp.arange(0, QK_ROPE, 2, dtype=F32) / QK_ROPE)
    ang = positions.astype(F32).reshape(t, 1) * inv_freq[None, :]
    cos_h, sin_h = jnp.cos(ang), jnp.sin(ang)
    zh = jnp.zeros_like(cos_h)
    cos_t = jnp.concatenate([cos_h, cos_h, zh, zh], axis=-1)
    sin_a = jnp.concatenate([-sin_h, zh, zh, zh], axis=-1)
    sin_b = jnp.concatenate([zh, sin_h, zh, zh], axis=-1)

    mod3 = _ada(c, w_ada[0], b_ada[0]).reshape(bsz, 6, d)
    lat, rec_in = _inproj(x2, mod3, norm1_w[0], w_lat, w_rec, seq)
    q, k, v = _mla_prep(lat, cos_t, sin_a, sin_b, q_lat_norm_w[0].reshape(1, -1), wuq,
                        kv_lat_norm_w[0].reshape(1, -1), wukv, qnw, knw, bsz, seq)
    att = _attention(q, k, v)
    rec = _hgrn(rec_in, lbf, lbb, rec_norm_w[0], bsz, seq)
    x1 = _outproj(x2, att, rec, mod3, w_out_att, w_out_rec, seq)

    h2b, h2p, top_idx, top_w, counts = _router(x1, mod3, norm2_w[0], wr_hi, wr_lo, router_bias[0], seq)
    counts = counts[:, 0].astype(I32)
    padded = (counts + MOE_BLOCK - 1) // MOE_BLOCK * MOE_BLOCK
    pad_end = jnp.cumsum(padded)
    experts = jnp.arange(N_EXPERTS, dtype=I32)
    n_blocks = -(-(t * TOP_K) // MOE_BLOCK) + N_EXPERTS
    n_used = pad_end[-1] // MOE_BLOCK
    blk_row0 = jnp.minimum(jnp.arange(n_blocks, dtype=I32), n_used - 1) * MOE_BLOCK
    blk_expert = jnp.sum((pad_end[None, :] <= blk_row0[:, None]).astype(I32), axis=-1)
    step_blk = jnp.clip(jnp.arange(n_blocks + 3, dtype=I32) - 1, 0, n_used - 1)
    step_expert = jnp.sum(jnp.where(step_blk[:, None] == jnp.arange(n_blocks, dtype=I32)[None, :],
                                    blk_expert[None, :], 0), axis=-1)
    n_assign = t * TOP_K
    filler = jnp.arange(MOE_BLOCK - 1, dtype=I32)[None, :] < (padded - counts)[:, None]
    filler_keys = jnp.where(filler, experts[:, None], N_EXPERTS).reshape(-1)
    keys = jnp.concatenate([top_idx.T.reshape(-1), filler_keys]).astype(I32)
    idx_bits = int(keys.shape[0] - 1).bit_length()
    packed = jnp.sort(keys * (1 << idx_bits) + jnp.arange(keys.shape[0], dtype=I32))
    order = packed & ((1 << idx_bits) - 1)
    order = jnp.pad(order, (0, n_blocks * MOE_BLOCK - order.shape[0]), constant_values=n_assign)
    real = order < n_assign
    pos = jnp.arange(n_blocks * MOE_BLOCK, dtype=I32)
    zone_slab = n_assign + ((pos // MOE_BLOCK) % 2) * MOE_BLOCK + pos % MOE_BLOCK
    src_rows = jnp.where(real, (order // TOP_K) * ROW_SLAB, 0).reshape(n_blocks, 1, MOE_BLOCK)
    dst_rows = jnp.where(real, (order % TOP_K) * t + order // TOP_K, zone_slab) * ROW_SLAB
    all_filler = (n_assign + MOE_BLOCK + jnp.arange(MOE_BLOCK, dtype=I32)) * ROW_SLAB
    dst_rows = jnp.concatenate([dst_rows, all_filler]).reshape(n_blocks + 1, 1, MOE_BLOCK)

    ys = _experts(step_expert.astype(I32), n_used.reshape(1).astype(I32), src_rows, dst_rows, h2p,
                  w_exp_gate[0], w_exp_up[0], w_exp_down[0], n_assign * ROW_SLAB)
    out = _combine(x1, h2b, top_w.T, mod3, w_sh_gate[0].astype(BF16), w_sh_up[0].astype(BF16),
                   w_sh_down[0].astype(BF16), ys, seq)
    return out.reshape(bsz, seq, d)
```

```python
import functools

import numpy as np
import jax
import jax.numpy as jnp
from jax import lax
from jax.experimental import pallas as pl
from jax.experimental.pallas import tpu as pltpu

F32 = jnp.float32
BF16 = jnp.bfloat16
U32 = jnp.uint32
I32 = jnp.int32

N_HEADS = 8
QK_NOPE = 128
QK_ROPE = 64
QK_DIM = QK_NOPE + QK_ROPE
QK_PAD = 256
V_DIM = 128
Q_RANK = 512
KV_RANK = 256
LAT_PAD = 896
REC_DIM = 128
REC_COLS = 5 * N_HEADS * REC_DIM
ROPE_THETA = 10000.0
N_EXPERTS = 64
TOP_K = 8
N_GROUPS = 8
TOPK_GROUPS = 4
EXPERT_DIM = 512
ROUTED_SCALE = 2.5
MOE_BLOCK = 256
EPS = 1e-6

LANES = 128
SUBLANES = 8
ROW_SLAB = 8
VMEM_LIMIT = 56 * 1024 * 1024

ATT_CHUNK = 256
HG_TILE = 128
HG_BAND = 8
HG_LEVELS = (8, 16, 32, 64)


def _cparams(sem, vmem=VMEM_LIMIT):
    return pltpu.CompilerParams(dimension_semantics=sem, vmem_limit_bytes=vmem)


def _split_bf16(a):
    hi = a.astype(BF16)
    lo = (a - hi.astype(F32)).astype(BF16)
    return hi, lo


def _pack_pair(a, b):
    ua = lax.bitcast_convert_type(a.astype(BF16).astype(F32), U32)
    ub = lax.bitcast_convert_type(b.astype(BF16).astype(F32), U32)
    return ua | (ub >> 16)


def _unpack_pair(p):
    a = lax.bitcast_convert_type(p & jnp.uint32(0xFFFF0000), F32).astype(BF16)
    b = lax.bitcast_convert_type(p << 16, F32).astype(BF16)
    return a, b


def _store_slabs(ref, packed):
    n = packed.shape[0]
    for j in range(ROW_SLAB):
        ref[pl.ds(j, n, stride=ROW_SLAB), :] = packed[:, j * LANES:(j + 1) * LANES]


def _load_slab_chunk(ref, j, n):
    return ref[pl.ds(j, n, stride=ROW_SLAB), :]


def _dot(a, b):
    return jnp.dot(a, b, preferred_element_type=F32)


def _dot_nt(a, b):
    return lax.dot_general(a, b, (((1,), (1,)), ((), ())), preferred_element_type=F32)


def _dot_tn(a, b):
    return lax.dot_general(a, b, (((0,), (0,)), ((), ())), preferred_element_type=F32)


def _ada_kernel(c_ref, w_ref, b_ref, o_ref):
    c = c_ref[...]
    cond = c * jax.nn.sigmoid(c)
    o_ref[...] = _dot(cond.astype(BF16), w_ref[...].astype(BF16)) + b_ref[...]


def _ada(c, w_ada, b_ada):
    bsz, d = c.shape
    n = w_ada.shape[1]
    tn = 1024
    return pl.pallas_call(
        _ada_kernel,
        grid=(n // tn,),
        in_specs=[
            pl.BlockSpec((bsz, d), lambda j: (0, 0)),
            pl.BlockSpec((d, tn), lambda j: (0, j)),
            pl.BlockSpec((1, tn), lambda j: (0, j)),
        ],
        out_specs=pl.BlockSpec((bsz, tn), lambda j: (0, j)),
        out_shape=jax.ShapeDtypeStruct((bsz, n), F32),
        compiler_params=_cparams(("arbitrary",)),
        name="ada",
    )(c, w_ada, b_ada.reshape(1, n))


def _modulated_norm(x, mod_ref, nw, shift_row, scale_row):
    r = lax.rsqrt(jnp.mean(x * x, axis=-1, keepdims=True) + EPS)
    scale = mod_ref[0, scale_row:scale_row + 1, :]
    shift = mod_ref[0, shift_row:shift_row + 1, :]
    return (x * r * nw) * (1.0 + scale) + shift


def _inproj_kernel(x_ref, mod_ref, nw_ref, wlat_ref, wrec_ref, lat_ref, rec_ref, h_scr):
    @pl.when(pl.program_id(1) == 0)
    def _():
        h = _modulated_norm(x_ref[...], mod_ref, nw_ref[...], 0, 1).astype(BF16)
        h_scr[...] = h
        lat_ref[...] = _dot(h, wlat_ref[...])

    rec_ref[...] = _dot(h_scr[...], wrec_ref[...])


def _inproj(x2, mod3, norm_w, w_lat, w_rec, seq):
    t, d = x2.shape
    tm, tn = 1024, 512
    per_b = seq // tm
    return pl.pallas_call(
        _inproj_kernel,
        grid=(t // tm, REC_COLS // tn),
        in_specs=[
            pl.BlockSpec((tm, d), lambda i, j: (i, 0)),
            pl.BlockSpec((1, 6, d), lambda i, j: (i // per_b, 0, 0)),
            pl.BlockSpec((1, d), lambda i, j: (0, 0)),
            pl.BlockSpec((d, LAT_PAD), lambda i, j: (0, 0)),
            pl.BlockSpec((d, tn), lambda i, j: (0, j)),
        ],
        out_specs=[
            pl.BlockSpec((tm, LAT_PAD), lambda i, j: (i, 0)),
            pl.BlockSpec((tm, tn), lambda i, j: (i, j)),
        ],
        out_shape=[
            jax.ShapeDtypeStruct((t, LAT_PAD), F32),
            jax.ShapeDtypeStruct((t, REC_COLS), F32),
        ],
        scratch_shapes=[pltpu.VMEM((tm, d), BF16)],
        compiler_params=_cparams(("parallel", "arbitrary")),
        name="inproj",
    )(x2, mod3, norm_w.reshape(1, d), w_lat, w_rec)


def _rope(x2, cos, sin_a, sin_b):
    return (x2 * cos + pltpu.roll(x2, LANES - QK_ROPE // 2, 1) * sin_a
            + pltpu.roll(x2, QK_ROPE // 2, 1) * sin_b)


def _mla_prep_kernel(lat_ref, cos_ref, sa_ref, sb_ref, qlw_ref, wuq_ref, kvlw_ref, wukv_ref,
                     qnw_ref, knw_ref, q_ref, k_ref, v_ref):
    lat = lat_ref[...]
    ql = lat[:, :Q_RANK]
    kvl = lat[:, Q_RANK:Q_RANK + KV_RANK]
    kr = lat[:, Q_RANK + KV_RANK:]
    qn = (ql * lax.rsqrt(jnp.mean(ql * ql, -1, keepdims=True) + EPS) * qlw_ref[...]).astype(BF16)
    kvn = (kvl * lax.rsqrt(jnp.mean(kvl * kvl, -1, keepdims=True) + EPS) * kvlw_ref[...]).astype(BF16)
    cos, sa, sb = cos_ref[...], sa_ref[...], sb_ref[...]
    qnw = qnw_ref[...]
    knw = knw_ref[...]
    kr_ss = jnp.sum(kr * kr, -1, keepdims=True)
    kr_roped = _rope(kr * knw[:, QK_NOPE:], cos, sa, sb)
    scale = QK_DIM ** -0.5
    for h in range(N_HEADS):
        qh = _dot(qn, wuq_ref[h])
        r = lax.rsqrt(jnp.sum(qh * qh, -1, keepdims=True) * (1.0 / QK_DIM) + EPS) * scale
        qh = qh * r * qnw
        q_ref[0, h, :, :QK_NOPE] = qh[:, :QK_NOPE].astype(BF16)
        q_ref[0, h, :, QK_NOPE:] = _rope(qh[:, QK_NOPE:], cos, sa, sb).astype(BF16)
        kvh = _dot(kvn, wukv_ref[h])
        kn = kvh[:, :QK_NOPE]
        rk = lax.rsqrt((jnp.sum(kn * kn, -1, keepdims=True) + kr_ss) * (1.0 / QK_DIM) + EPS)
        k_ref[0, h, :, :QK_NOPE] = (kn * rk * knw[:, :QK_NOPE]).astype(BF16)
        k_ref[0, h, :, QK_NOPE:] = (kr_roped * rk).astype(BF16)
        v_ref[0, h] = kvh[:, QK_NOPE:].astype(BF16)


def _mla_prep(lat, cos, sa, sb, qlw, wuq, kvlw, wukv, qnw, knw, bsz, seq):
    tm = 512
    per_b = seq // tm
    const2 = lambda i: (0, 0)
    const3 = lambda i: (0, 0, 0)
    head_out = lambda w: pl.BlockSpec((1, N_HEADS, tm, w), lambda i: (i // per_b, 0, i % per_b, 0))
    return pl.pallas_call(
        _mla_prep_kernel,
        grid=(bsz * seq // tm,),
        in_specs=[
            pl.BlockSpec((tm, LAT_PAD), lambda i: (i, 0)),
            pl.BlockSpec((tm, LANES), lambda i: (i, 0)),
            pl.BlockSpec((tm, LANES), lambda i: (i, 0)),
            pl.BlockSpec((tm, LANES), lambda i: (i, 0)),
            pl.BlockSpec((1, Q_RANK), const2),
            pl.BlockSpec((N_HEADS, Q_RANK, QK_PAD), const3),
            pl.BlockSpec((1, KV_RANK), const2),
            pl.BlockSpec((N_HEADS, KV_RANK, QK_PAD), const3),
            pl.BlockSpec((1, QK_PAD), const2),
            pl.BlockSpec((1, QK_PAD), const2),
        ],
        out_specs=[head_out(QK_PAD), head_out(QK_PAD), head_out(V_DIM)],
        out_shape=[
            jax.ShapeDtypeStruct((bsz, N_HEADS, seq, QK_PAD), BF16),
            jax.ShapeDtypeStruct((bsz, N_HEADS, seq, QK_PAD), BF16),
            jax.ShapeDtypeStruct((bsz, N_HEADS, seq, V_DIM), BF16),
        ],
        compiler_params=_cparams(("parallel",)),
        name="mla_prep",
    )(lat, cos, sa, sb, qlw, wuq, kvlw, wukv, qnw, knw)


def _attn_kernel(q_ref, k_ref, v_ref, o_ref):
    k = k_ref[0, 0]
    v = v_ref[0, 0]
    for c in range(q_ref.shape[2] // ATT_CHUNK):
        rows = pl.ds(c * ATT_CHUNK, ATT_CHUNK)
        s = _dot_nt(q_ref[0, 0, rows, :], k)
        m = jnp.max(s, axis=-1, keepdims=True)
        p = jnp.exp(s - m)
        l = jnp.sum(p, axis=-1, keepdims=True)
        o_ref[rows, :] = _dot(p.astype(BF16), v) / l


def _attention(q, k, v):
    bsz, nh, seq, _ = q.shape
    tq = 1024
    nq = seq // tq
    return pl.pallas_call(
        _attn_kernel,
        grid=(bsz, nh, nq),
        in_specs=[
            pl.BlockSpec((1, 1, tq, QK_PAD), lambda b, h, i: (b, h, i, 0)),
            pl.BlockSpec((1, 1, seq, QK_PAD), lambda b, h, i: (b, h, 0, 0)),
            pl.BlockSpec((1, 1, seq, V_DIM), lambda b, h, i: (b, h, 0, 0)),
        ],
        out_specs=pl.BlockSpec((tq, V_DIM), lambda b, h, i: (b * nq + i, h)),
        out_shape=jax.ShapeDtypeStruct((bsz * seq, nh * V_DIM), F32),
        compiler_params=_cparams(("parallel", "parallel", "arbitrary")),
        name="attention",
    )(q, k, v)


def _hgrn_constants():
    n = HG_TILE
    t = np.arange(n)[:, None]
    u = np.arange(n)[None, :]
    tri = np.concatenate([(u <= t), (u >= t)], axis=0).astype(np.float32)
    lvl = np.zeros((n, n), np.float32)
    for li, m in enumerate(HG_LEVELS):
        lvl[((t // (2 * m)) == (u // (2 * m))) & ((t // m) != (u // m))] = li + 1
    ones2 = np.zeros((2 * LANES, 2 * LANES), np.float32)
    ones2[:LANES, :LANES] = 1.0
    ones2[LANES:, LANES:] = 1.0
    return tri, lvl, ones2


def _level_stacks(m, q, kkf, kkb, bf, cb):
    zero = jnp.zeros((m, LANES), F32)
    q_rows, k_rows = [], []
    for a in range(0, HG_TILE, 2 * m):
        e, o = slice(a, a + m), slice(a + m, a + 2 * m)
        ref_f = bf[a + m - 1:a + m, :]
        ref_b = cb[a + m:a + m + 1, :]
        wf_e = jnp.exp(ref_f - bf[e])
        wf_o = jnp.exp(bf[o] - ref_f)
        wb_e = jnp.exp(cb[e] - ref_b)
        wb_o = jnp.exp(ref_b - cb[o])
        q_rows.append(jnp.concatenate([zero, q[e] * wb_e], axis=1))
        q_rows.append(jnp.concatenate([q[o] * wf_o, zero], axis=1))
        k_rows.append(jnp.concatenate([kkf[e] * wf_e, zero], axis=1))
        k_rows.append(jnp.concatenate([zero, kkb[o] * wb_o], axis=1))
    return (jnp.concatenate(q_rows, axis=0).astype(BF16), jnp.concatenate(k_rows, axis=0).astype(BF16))


def _hgrn_kernel(q_ref, zf_ref, zb_ref, v_ref, gate_ref, lbf_ref, lbb_ref, nw_ref,
                 tri_ref, lvl_ref, ones_ref, o_ref,
                 ff_scr, fb_scr, kf_scr, kb_scr, v_scr, o_scr,
                 qe_scr, u_scr, dec_scr, st_scr):
    seq = q_ref.shape[0]
    n_tiles = seq // HG_TILE
    pad = HG_BAND
    zeros_pad = jnp.zeros((pad, LANES), F32)
    for scr in (ff_scr, fb_scr, kf_scr, kb_scr, v_scr):
        scr[pl.ds(0, pad), :] = zeros_pad
        scr[pl.ds(pad + seq, pad), :] = zeros_pad
    v_scr[pl.ds(pad, seq), :] = v_ref[...]
    pos = lax.broadcasted_iota(I32, (HG_TILE, 1), 0) % HG_BAND
    lbf = lbf_ref[...]
    lbb = lbb_ref[...]

    def tile_body(i, carry):
        t0 = pl.multiple_of(i * HG_TILE, HG_TILE)
        q = q_ref[pl.ds(t0, HG_TILE), :]
        v = v_scr[pl.ds(pad + t0, HG_TILE), :]
        v_b = v.astype(BF16)
        f_f = lbf + (1.0 - lbf) * jax.nn.sigmoid(zf_ref[pl.ds(t0, HG_TILE), :])
        f_b = lbb + (1.0 - lbb) * jax.nn.sigmoid(zb_ref[pl.ds(t0, HG_TILE), :])
        kkf = 1.0 - f_f
        kkb = 1.0 - f_b
        kf_scr[pl.ds(pad + t0, HG_TILE), :] = kkf
        kb_scr[pl.ds(pad + t0, HG_TILE), :] = kkb
        ff_scr[pl.ds(pad + t0, HG_TILE), :] = jnp.where(pos == 0, 0.0, f_f)
        fb_scr[pl.ds(pad + t0, HG_TILE), :] = jnp.where(pos == HG_BAND - 1, 0.0, f_b)
        kb_scr[pl.ds(pad + t0 + HG_TILE, pad), :] = zeros_pad
        fb_scr[pl.ds(pad + t0 + HG_TILE, pad), :] = zeros_pad
        gf_hi, gf_lo = _split_bf16(jnp.log(f_f))
        gb_hi, gb_lo = _split_bf16(jnp.log(f_b))
        pf = _dot(tri_ref[pl.ds(0, HG_TILE), :], jnp.concatenate([gf_hi, gf_lo], axis=1))
        pb = _dot(tri_ref[pl.ds(HG_TILE, HG_TILE), :], jnp.concatenate([gb_hi, gb_lo], axis=1))
        bf = pf[:, :LANES] + pf[:, LANES:]
        cb = pb[:, :LANES] + pb[:, LANES:]

        last_f = bf[HG_TILE - 1:HG_TILE, :]
        first_b = cb[0:1, :]
        wq_f = jnp.exp(bf)
        wq_b = jnp.exp(cb)
        ke = jnp.concatenate([kkf * jnp.exp(last_f - bf), kkb * jnp.exp(first_b - cb)], axis=1)
        u_scr[i] = _dot_tn(v_b, ke.astype(BF16))
        qe_scr[i] = jnp.concatenate([q * wq_f, q * wq_b], axis=1).astype(BF16)
        dec_scr[pl.ds(i, 1), :] = jnp.concatenate([wq_f[HG_TILE - 1:HG_TILE, :], wq_b[0:1, :]], axis=1)

        lvl = lvl_ref[...]
        a = jnp.zeros((HG_TILE, HG_TILE), F32)
        for li, m in enumerate(HG_LEVELS):
            qs, ks = _level_stacks(m, q, kkf, kkb, bf, cb)
            a = jnp.where(lvl == float(li + 1), _dot_nt(qs, ks), a)
        o = _dot(a.astype(BF16), v_b)

        ones2 = ones_ref[...]
        pf_run = None
        pb_run = None
        for d in range(HG_BAND):
            if d == 1:
                pf_run = ff_scr[pl.ds(pad + t0, HG_TILE), :]
                pb_run = fb_scr[pl.ds(pad + t0, HG_TILE), :]
            elif d > 1:
                pf_run = pf_run * ff_scr[pl.ds(pad + t0 - (d - 1), HG_TILE), :]
                pb_run = pb_run * fb_scr[pl.ds(pad + t0 + (d - 1), HG_TILE), :]
            p_f = q * kf_scr[pl.ds(pad + t0 - d, HG_TILE), :]
            p_b = q * kb_scr[pl.ds(pad + t0 + d, HG_TILE), :]
            if d > 0:
                p_f = p_f * pf_run
                p_b = p_b * pb_run
            rs = _dot(jnp.concatenate([p_f, p_b], axis=1).astype(BF16), ones2)
            o = (o + rs[:, :LANES] * v_scr[pl.ds(pad + t0 - d, HG_TILE), :]
                 + rs[:, LANES:] * v_scr[pl.ds(pad + t0 + d, HG_TILE), :])
        o_scr[pl.ds(t0, HG_TILE), :] = o
        return carry

    lax.fori_loop(0, n_tiles, tile_body, 0, unroll=2)

    def scan_f(i, state):
        st_scr[i] = state.astype(BF16)
        return state * dec_scr[pl.ds(i, 1), :][:, :LANES] + u_scr[i][:, :LANES]

    lax.fori_loop(0, n_tiles, scan_f, jnp.zeros((V_DIM, REC_DIM), F32))

    def scan_b(n, state):
        i = n_tiles - 1 - n
        t0 = pl.multiple_of(i * HG_TILE, HG_TILE)
        both = jnp.concatenate([st_scr[i], state.astype(BF16)], axis=1)
        o_scr[pl.ds(t0, HG_TILE), :] = o_scr[pl.ds(t0, HG_TILE), :] + _dot_nt(qe_scr[i], both)
        return state * dec_scr[pl.ds(i, 1), :][:, LANES:] + u_scr[i][:, LANES:]

    lax.fori_loop(0, n_tiles, scan_b, jnp.zeros((V_DIM, REC_DIM), F32), unroll=4)

    o = o_scr[...]
    o = o * lax.rsqrt(jnp.mean(o * o, -1, keepdims=True) + EPS) * nw_ref[...]
    gate = gate_ref[...]
    o_ref[...] = o * (gate * jax.nn.sigmoid(gate))


def _hgrn(rec, lbf, lbb, rec_norm_w, bsz, seq):
    tri, lvl, ones2 = _hgrn_constants()
    n_tiles = seq // HG_TILE
    col = lambda c: pl.BlockSpec((seq, REC_DIM), lambda b, h: (b, c * N_HEADS + h))
    vec = pl.BlockSpec((1, REC_DIM), lambda b, h: (0, h))
    const = lambda r, c: pl.BlockSpec((r, c), lambda b, h: (0, 0))
    padded = pltpu.VMEM((seq + 2 * HG_BAND, LANES), F32)
    plain = pltpu.VMEM((seq, LANES), F32)
    return pl.pallas_call(
        _hgrn_kernel,
        grid=(bsz, N_HEADS),
        in_specs=[col(0), col(1), col(2), col(3), col(4), vec, vec, const(1, REC_DIM),
                  const(2 * HG_TILE, HG_TILE), const(HG_TILE, HG_TILE), const(2 * LANES, 2 * LANES)],
        out_specs=pl.BlockSpec((seq, REC_DIM), lambda b, h: (b, h)),
        out_shape=jax.ShapeDtypeStruct((bsz * seq, N_HEADS * REC_DIM), F32),
        scratch_shapes=[padded, padded, padded, padded, padded, plain,
                        pltpu.VMEM((n_tiles, HG_TILE, 2 * LANES), BF16),
                        pltpu.VMEM((n_tiles, V_DIM, 2 * LANES), F32),
                        pltpu.VMEM((n_tiles, 2 * LANES), F32),
                        pltpu.VMEM((n_tiles, V_DIM, LANES), BF16)],
        compiler_params=_cparams(("parallel", "parallel")),
        name="hgrn",
    )(rec, rec, rec, rec, rec, lbf, lbb, rec_norm_w.reshape(1, REC_DIM),
      jnp.asarray(tri, BF16), jnp.asarray(lvl, F32), jnp.asarray(ones2, BF16))


def _outproj_kernel(x_ref, att_ref, rec_ref, mod_ref, wa_ref, wr_ref, o_ref):
    y = _dot(att_ref[...].astype(BF16), wa_ref[...]) + _dot(rec_ref[...].astype(BF16), wr_ref[...])
    o_ref[...] = x_ref[...] + mod_ref[0, 2:3, :] * y


def _outproj(x2, att, rec, mod3, w_att, w_rec, seq):
    t, d = x2.shape
    tm = 512
    per_b = seq // tm
    half = att.shape[1]
    return pl.pallas_call(
        _outproj_kernel,
        grid=(t // tm,),
        in_specs=[
            pl.BlockSpec((tm, d), lambda i: (i, 0)),
            pl.BlockSpec((tm, half), lambda i: (i, 0)),
            pl.BlockSpec((tm, half), lambda i: (i, 0)),
            pl.BlockSpec((1, 6, d), lambda i: (i // per_b, 0, 0)),
            pl.BlockSpec((half, d), lambda i: (0, 0)),
            pl.BlockSpec((half, d), lambda i: (0, 0)),
        ],
        out_specs=pl.BlockSpec((tm, d), lambda i: (i, 0)),
        out_shape=jax.ShapeDtypeStruct((t, d), F32),
        compiler_params=_cparams(("parallel",)),
        name="outproj",
    )(x2, att, rec, mod3, w_att, w_rec)


def _router_kernel(x_ref, mod_ref, nw_ref, wrh_ref, wrl_ref, bias_ref,
                   h2b_ref, h2p_ref, idx_ref, wgt_ref, cnt_ref, base_scr):
    i = pl.program_id(0)
    tm = x_ref.shape[0]
    half = x_ref.shape[1] // 2

    @pl.when(i == 0)
    def _():
        base_scr[...] = jnp.zeros_like(base_scr)

    h2 = _modulated_norm(x_ref[...], mod_ref, nw_ref[...], 3, 4)
    h_hi, h_lo = _split_bf16(h2)
    h2b_ref[...] = h_hi
    _store_slabs(h2p_ref, _pack_pair(h2[:, :half], h2[:, half:]))
    logits = _dot_nt(wrh_ref[...], h_hi) + _dot_nt(wrh_ref[...], h_lo) + _dot_nt(wrl_ref[...], h_hi)
    scores = jax.nn.sigmoid(logits)
    sel = scores + bias_ref[...]

    neg = -jnp.inf
    per_grp = N_EXPERTS // N_GROUPS
    sel3 = sel.reshape(N_GROUPS, per_grp, tm)
    within = lax.broadcasted_iota(I32, sel3.shape, 1)
    m1 = jnp.max(sel3, axis=1, keepdims=True)
    first = jnp.min(jnp.where(sel3 == m1, within, per_grp), axis=1, keepdims=True)
    m2 = jnp.max(jnp.where(within == first, neg, sel3), axis=1, keepdims=True)
    gs = (m1 + m2).reshape(N_GROUPS, tm)
    gidx = lax.broadcasted_iota(I32, gs.shape, 0)
    beaten = jnp.zeros(gs.shape, I32)
    for g2 in range(N_GROUPS):
        other = gs[g2:g2 + 1, :]
        beaten = beaten + ((other > gs) | ((other == gs) & (g2 < gidx))).astype(I32)
    keep = (beaten < TOPK_GROUPS).reshape(N_GROUPS, 1, tm)
    cur = jnp.where(keep, sel3, neg).reshape(N_EXPERTS, tm)

    eidx = lax.broadcasted_iota(I32, (N_EXPERTS, tm), 0)
    chosen = jnp.zeros((N_EXPERTS, tm), F32)
    picks, pick_scores = [], []
    for _ in range(TOP_K):
        m = jnp.max(cur, axis=0, keepdims=True)
        pick = jnp.min(jnp.where(cur == m, eidx, N_EXPERTS), axis=0, keepdims=True)
        hit = eidx == pick
        picks.append(pick)
        pick_scores.append(jnp.sum(jnp.where(hit, scores, 0.0), axis=0, keepdims=True))
        chosen = jnp.where(hit, 1.0, chosen)
        cur = jnp.where(hit, neg, cur)

    denom = pick_scores[0]
    for s in pick_scores[1:]:
        denom = denom + s
    for j in range(TOP_K):
        idx_ref[j:j + 1, :] = picks[j]
        wgt_ref[j:j + 1, :] = pick_scores[j] / denom * ROUTED_SCALE
    base_scr[...] = base_scr[...] + jnp.sum(chosen, axis=1, keepdims=True)
    cnt_ref[...] = base_scr[...]


def _router(x1, mod3, norm_w, wr_hi, wr_lo, bias, seq):
    t, d = x1.shape
    tm = 512
    per_b = seq // tm
    row_out = lambda dt: jax.ShapeDtypeStruct((TOP_K, t), dt)
    row_spec = pl.BlockSpec((TOP_K, tm), lambda i: (0, i))
    return pl.pallas_call(
        _router_kernel,
        grid=(t // tm,),
        in_specs=[
            pl.BlockSpec((tm, d), lambda i: (i, 0)),
            pl.BlockSpec((1, 6, d), lambda i: (i // per_b, 0, 0)),
            pl.BlockSpec((1, d), lambda i: (0, 0)),
            pl.BlockSpec((N_EXPERTS, d), lambda i: (0, 0)),
            pl.BlockSpec((N_EXPERTS, d), lambda i: (0, 0)),
            pl.BlockSpec((N_EXPERTS, 1), lambda i: (0, 0)),
        ],
        out_specs=[
            pl.BlockSpec((tm, d), lambda i: (i, 0)),
            pl.BlockSpec((tm * ROW_SLAB, LANES), lambda i: (i, 0)),
            row_spec, row_spec,
            pl.BlockSpec((N_EXPERTS, LANES), lambda i: (0, 0)),
        ],
        out_shape=[
            jax.ShapeDtypeStruct((t, d), BF16),
            jax.ShapeDtypeStruct((t * ROW_SLAB, LANES), U32),
            row_out(I32), row_out(F32),
            jax.ShapeDtypeStruct((N_EXPERTS, LANES), F32),
        ],
        scratch_shapes=[pltpu.VMEM((N_EXPERTS, LANES), F32)],
        compiler_params=_cparams(("arbitrary",)),
        name="router",
    )(x1, mod3, norm_w.reshape(1, d), wr_hi, wr_lo, bias.reshape(N_EXPERTS, 1))


BLOCK_SLABS = MOE_BLOCK * ROW_SLAB


def _gather_block(src_ref, h2p_ref, dst, sem):
    for i in range(MOE_BLOCK):
        src = pl.multiple_of(src_ref[0, 0, i], ROW_SLAB)
        pltpu.make_async_copy(h2p_ref.at[pl.ds(src, ROW_SLAB)], dst.at[pl.ds(i * ROW_SLAB, ROW_SLAB)],
                              sem).start(priority=0)


def _scatter_block(dst_ref, src, out_ref, sem):
    for i in range(MOE_BLOCK):
        dst = pl.multiple_of(dst_ref[0, 0, i], ROW_SLAB)
        pltpu.make_async_copy(src.at[pl.ds(i * ROW_SLAB, ROW_SLAB)], out_ref.at[pl.ds(dst, ROW_SLAB)],
                              sem).start(priority=1)


def _expert_kernel(be_ref, nused_ref, src0_ref, src_ref, dst_ref, h2p_ref, wg_ref, wu_ref, wd_ref, out_ref,
                   wg_scr, wu_scr, wd_scr, x_scr, xbuf, ybuf, xsem, ysem):
    s = pl.program_id(0)
    n_used = nused_ref[0]
    half = ROW_SLAB * LANES
    n_real = out_ref.shape[0] - 2 * BLOCK_SLABS
    slot = lax.rem(s, 2)
    other = 1 - slot
    x_cur = lax.rem(s + 2, 3)
    x_next = lax.rem(s + 1, 3)
    gathered = lambda sem: pltpu.make_async_copy(h2p_ref.at[pl.ds(0, BLOCK_SLABS)], xbuf.at[0], sem)
    scattered = lambda sem: pltpu.make_async_copy(ybuf.at[0], out_ref.at[pl.ds(0, BLOCK_SLABS)], sem)

    @pl.when(s == 0)
    def _():
        ybuf[...] = jnp.zeros_like(ybuf)
        for q in range(2):
            cp = pltpu.make_async_copy(ybuf.at[q], out_ref.at[pl.ds(n_real + q * BLOCK_SLABS, BLOCK_SLABS)],
                                       ysem.at[q])
            cp.start()
            cp.wait()
        pltpu.make_async_copy(ybuf.at[0], out_ref.at[pl.ds(n_real, BLOCK_SLABS)], ysem.at[0]).start()
        _gather_block(src0_ref, h2p_ref, xbuf.at[0], xsem.at[0])
        _gather_block(src_ref, h2p_ref, xbuf.at[1], xsem.at[1])

    @pl.when((s == 0) | (be_ref[s] != be_ref[jnp.maximum(s - 1, 0)]))
    def _():
        wg_scr[...] = wg_ref[...].astype(BF16)
        wu_scr[...] = wu_ref[...].astype(BF16)
        wd_scr[...] = wd_ref[...].astype(BF16)

    @pl.when((s >= 1) & (s <= n_used + 1))
    def _():
        gathered(xsem.at[x_cur]).wait()
        xcur = xbuf.at[x_cur]
        for j in range(ROW_SLAB):
            xa, xb = _unpack_pair(_load_slab_chunk(xcur, j, MOE_BLOCK))
            x_scr[:, j * LANES:(j + 1) * LANES] = xa
            x_scr[:, half + j * LANES:half + (j + 1) * LANES] = xb
        _gather_block(src_ref, h2p_ref, xbuf.at[x_next], xsem.at[x_next])
        _scatter_block(dst_ref, ybuf.at[slot], out_ref, ysem.at[slot])
        xfull = x_scr[...]
        gate = _dot(xfull, wg_scr[...])
        up = _dot(xfull, wu_scr[...])
        mid = ((gate * jax.nn.sigmoid(gate)) * up).astype(BF16)
        y = _dot(mid, wd_scr[...])
        scattered(ysem.at[other]).wait()
        _store_slabs(ybuf.at[other], _pack_pair(y[:, :half], y[:, half:]))

    @pl.when(s == n_used + 2)
    def _():
        gathered(xsem.at[x_cur]).wait()
        gathered(xsem.at[lax.rem(s, 3)]).wait()
        scattered(ysem.at[other]).wait()


def _experts(step_expert, n_used, src_rows, dst_rows, h2p, w_gate, w_up, w_down, n_out_slabs):
    n_steps = step_expert.shape[0]
    n_blocks = n_steps - 3
    d = 2 * ROW_SLAB * LANES
    wspec = lambda r, c: pl.BlockSpec((None, r, c), lambda s, be, nu: (be[s], 0, 0))
    scattered = lambda s, be, nu: (jnp.where(s >= 2, jnp.minimum(s - 2, nu[0] - 1), n_blocks), 0, 0)
    return pl.pallas_call(
        _expert_kernel,
        grid_spec=pltpu.PrefetchScalarGridSpec(
            num_scalar_prefetch=2,
            grid=(n_steps,),
            in_specs=[
                pl.BlockSpec((1, 1, MOE_BLOCK), lambda s, be, nu: (0, 0, 0), memory_space=pltpu.SMEM),
                pl.BlockSpec((1, 1, MOE_BLOCK), lambda s, be, nu: (jnp.minimum(s + 1, nu[0] - 1), 0, 0),
                             memory_space=pltpu.SMEM),
                pl.BlockSpec((1, 1, MOE_BLOCK), scattered, memory_space=pltpu.SMEM),
                pl.BlockSpec(memory_space=pl.ANY),
                wspec(d, EXPERT_DIM), wspec(d, EXPERT_DIM), wspec(EXPERT_DIM, d),
            ],
            out_specs=pl.BlockSpec(memory_space=pl.ANY),
            scratch_shapes=[pltpu.VMEM((d, EXPERT_DIM), BF16), pltpu.VMEM((d, EXPERT_DIM), BF16),
                            pltpu.VMEM((EXPERT_DIM, d), BF16), pltpu.VMEM((MOE_BLOCK, d), BF16),
                            pltpu.VMEM((3, BLOCK_SLABS, LANES), U32),
                            pltpu.VMEM((2, BLOCK_SLABS, LANES), U32),
                            pltpu.SemaphoreType.DMA((3,)), pltpu.SemaphoreType.DMA((2,))],
        ),
        out_shape=jax.ShapeDtypeStruct((n_out_slabs + 2 * BLOCK_SLABS, LANES), U32),
        compiler_params=_cparams(("arbitrary",)),
        name="experts",
    )(step_expert, n_used, src_rows, src_rows, dst_rows, h2p, w_gate, w_up, w_down)


def _combine_kernel(x1_ref, h2b_ref, wgt_ref, mod_ref, wsg_ref, wsu_ref, wsd_ref, *rest):
    slot_refs, o_ref = rest[:TOP_K], rest[TOP_K]
    tm = x1_ref.shape[0]
    h2 = h2b_ref[...]
    gate = _dot(h2, wsg_ref[...])
    up = _dot(h2, wsu_ref[...])
    shared = _dot(((gate * jax.nn.sigmoid(gate)) * up).astype(BF16), wsd_ref[...])

    half = ROW_SLAB * LANES
    wgt = wgt_ref[...]
    g2 = mod_ref[0, 5:6, :]
    for c in range(ROW_SLAB):
        ra = jnp.zeros((tm, LANES), F32)
        rb = jnp.zeros((tm, LANES), F32)
        for j in range(TOP_K):
            ya, yb = _unpack_pair(_load_slab_chunk(slot_refs[j], c, tm))
            wj = wgt[:, j:j + 1]
            ra = ra + wj * ya.astype(F32)
            rb = rb + wj * yb.astype(F32)
        lo = slice(c * LANES, (c + 1) * LANES)
        hi = slice(half + c * LANES, half + (c + 1) * LANES)
        o_ref[:, lo] = x1_ref[:, lo] + g2[:, lo] * (ra + shared[:, lo])
        o_ref[:, hi] = x1_ref[:, hi] + g2[:, hi] * (rb + shared[:, hi])


def _combine(x1, h2b, wgt_t, mod3, ws_gate, ws_up, ws_down, ys, seq):
    t, d = x1.shape
    tm = 256
    per_b = seq // tm
    n_tiles = t // tm
    slot_spec = lambda j: pl.BlockSpec((tm * ROW_SLAB, LANES), lambda i: (j * n_tiles + i, 0))
    return pl.pallas_call(
        _combine_kernel,
        grid=(n_tiles,),
        in_specs=[
            pl.BlockSpec((tm, d), lambda i: (i, 0)),
            pl.BlockSpec((tm, d), lambda i: (i, 0)),
            pl.BlockSpec((tm, TOP_K), lambda i: (i, 0)),
            pl.BlockSpec((1, 6, d), lambda i: (i // per_b, 0, 0)),
            pl.BlockSpec((d, EXPERT_DIM), lambda i: (0, 0)),
            pl.BlockSpec((d, EXPERT_DIM), lambda i: (0, 0)),
            pl.BlockSpec((EXPERT_DIM, d), lambda i: (0, 0)),
        ] + [slot_spec(j) for j in range(TOP_K)],
        out_specs=pl.BlockSpec((tm, d), lambda i: (i, 0)),
        out_shape=jax.ShapeDtypeStruct((t, d), F32),
        compiler_params=_cparams(("parallel",)),
        name="combine",
    )(x1, h2b, wgt_t, mod3, ws_gate, ws_up, ws_down, *([ys] * TOP_K))


def kernel(x, c, positions, w_ada, b_ada, norm1_w, w_in, q_lat_norm_w, w_uq, kv_lat_norm_w, w_ukv, q_norm_w, k_norm_w, lb_fwd, lb_bwd, rec_norm_w, w_out, norm2_w, w_router, router_bias, w_exp_gate, w_exp_up, w_exp_down, w_sh_gate, w_sh_up, w_sh_down):
    bsz, seq, d = x.shape
    t = bsz * seq
    x2 = x.reshape(t, d)

    w_in0 = w_in[0]
    n_lat = Q_RANK + KV_RANK + QK_ROPE
    w_lat = jnp.pad(w_in0[:, :n_lat], ((0, 0), (0, LAT_PAD - n_lat))).astype(BF16)
    w_rec = w_in0[:, n_lat:].astype(BF16)
    wuq = jnp.pad(w_uq[0].reshape(Q_RANK, N_HEADS, QK_DIM), ((0, 0), (0, 0), (0, QK_PAD - QK_DIM)))
    wuq = wuq.transpose(1, 0, 2).astype(BF16)
    wukv = w_ukv[0].reshape(KV_RANK, N_HEADS, QK_NOPE + V_DIM).transpose(1, 0, 2).astype(BF16)
    qnw = jnp.pad(q_norm_w[0], (0, QK_PAD - QK_DIM)).reshape(1, QK_PAD)
    knw = jnp.pad(k_norm_w[0], (0, QK_PAD - QK_DIM)).reshape(1, QK_PAD)
    half = N_HEADS * V_DIM
    w_out_att = w_out[0, :half].astype(BF16)
    w_out_rec = w_out[0, half:].astype(BF16)
    wr_hi, wr_lo = _split_bf16(w_router[0].T)
    lbf = jnp.cumsum(jax.nn.softmax(lb_fwd.astype(F32), axis=0), axis=0)[0].reshape(1, -1)
    lbb = jnp.cumsum(jax.nn.softmax(lb_bwd.astype(F32), axis=0), axis=0)[0].reshape(1, -1)

    inv_freq = ROPE_THETA ** (-jnp.arange(0, QK_ROPE, 2, dtype=F32) / QK_ROPE)
    ang = positions.astype(F32).reshape(t, 1) * inv_freq[None, :]
    cos_h, sin_h = jnp.cos(ang), jnp.sin(ang)
    zh = jnp.zeros_like(cos_h)
    cos_t = jnp.concatenate([cos_h, cos_h, zh, zh], axis=-1)
    sin_a = jnp.concatenate([-sin_h, zh, zh, zh], axis=-1)
    sin_b = jnp.concatenate([zh, sin_h, zh, zh], axis=-1)

    mod3 = _ada(c, w_ada[0], b_ada[0]).reshape(bsz, 6, d)
    lat, rec_in = _inproj(x2, mod3, norm1_w[0], w_lat, w_rec, seq)
    q, k, v = _mla_prep(lat, cos_t, sin_a, sin_b, q_lat_norm_w[0].reshape(1, -1), wuq,
                        kv_lat_norm_w[0].reshape(1, -1), wukv, qnw, knw, bsz, seq)
    att = _attention(q, k, v)
    rec = _hgrn(rec_in, lbf, lbb, rec_norm_w[0], bsz, seq)
    x1 = _outproj(x2, att, rec, mod3, w_out_att, w_out_rec, seq)

    h2b, h2p, top_idx, top_w, counts = _router(x1, mod3, norm2_w[0], wr_hi, wr_lo, router_bias[0], seq)
    counts = counts[:, 0].astype(I32)
    padded = (counts + MOE_BLOCK - 1) // MOE_BLOCK * MOE_BLOCK
    pad_end = jnp.cumsum(padded)
    experts = jnp.arange(N_EXPERTS, dtype=I32)
    n_blocks = -(-(t * TOP_K) // MOE_BLOCK) + N_EXPERTS
    n_used = pad_end[-1] // MOE_BLOCK
    blk_row0 = jnp.minimum(jnp.arange(n_blocks, dtype=I32), n_used - 1) * MOE_BLOCK
    blk_expert = jnp.sum((pad_end[None, :] <= blk_row0[:, None]).astype(I32), axis=-1)
    step_blk = jnp.clip(jnp.arange(n_blocks + 3, dtype=I32) - 1, 0, n_used - 1)
    step_expert = jnp.sum(jnp.where(step_blk[:, None] == jnp.arange(n_blocks, dtype=I32)[None, :],
                                    blk_expert[None, :], 0), axis=-1)
    n_assign = t * TOP_K
    filler = jnp.arange(MOE_BLOCK - 1, dtype=I32)[None, :] < (padded - counts)[:, None]
    filler_keys = jnp.where(filler, experts[:, None], N_EXPERTS).reshape(-1)
    keys = jnp.concatenate([top_idx.T.reshape(-1), filler_keys]).astype(I32)
    idx_bits = int(keys.shape[0] - 1).bit_length()
    packed = jnp.sort(keys * (1 << idx_bits) + jnp.arange(keys.shape[0], dtype=I32))
    order = packed & ((1 << idx_bits) - 1)
    order = jnp.pad(order, (0, n_blocks * MOE_BLOCK - order.shape[0]), constant_values=n_assign)
    real = order < n_assign
    pos = jnp.arange(n_blocks * MOE_BLOCK, dtype=I32)
    zone_slab = n_assign + ((pos // MOE_BLOCK) % 2) * MOE_BLOCK + pos % MOE_BLOCK
    src_rows = jnp.where(real, (order // TOP_K) * ROW_SLAB, 0).reshape(n_blocks, 1, MOE_BLOCK)
    dst_rows = jnp.where(real, (order % TOP_K) * t + order // TOP_K, zone_slab) * ROW_SLAB
    all_filler = (n_assign + MOE_BLOCK + jnp.arange(MOE_BLOCK, dtype=I32)) * ROW_SLAB
    dst_rows = jnp.concatenate([dst_rows, all_filler]).reshape(n_blocks + 1, 1, MOE_BLOCK)

    ys = _experts(step_expert.astype(I32), n_used.reshape(1).astype(I32), src_rows, dst_rows, h2p,
                  w_exp_gate[0], w_exp_up[0], w_exp_down[0], n_assign * ROW_SLAB)
    out = _combine(x1, h2b, top_w.T, mod3, w_sh_gate[0].astype(BF16), w_sh_up[0].astype(BF16),
                   w_sh_down[0].astype(BF16), ys, seq)
    return out.reshape(bsz, seq, d)
```

```python
import numpy as np
import jax
import jax.numpy as jnp
from jax import lax
from jax.experimental import pallas as pl
from jax.experimental.pallas import tpu as pltpu

F32 = jnp.float32
BF16 = jnp.bfloat16
U32 = jnp.uint32
I32 = jnp.int32

N_HEADS = 8
QK_NOPE = 128
QK_ROPE = 64
QK_DIM = QK_NOPE + QK_ROPE
QK_PAD = 256
V_DIM = 128
Q_RANK = 512
KV_RANK = 256
LAT_PAD = 896
REC_DIM = 128
REC_COLS = 5 * N_HEADS * REC_DIM
ROPE_THETA = 10000.0
N_EXPERTS = 64
TOP_K = 8
N_GROUPS = 8
TOPK_GROUPS = 4
EXPERT_DIM = 512
ROUTED_SCALE = 2.5
MOE_BLOCK = 256
EPS = 1e-6

LANES = 128
ROW_SLAB = 8
VMEM_LIMIT = 56 * 1024 * 1024

ATT_CHUNK = 256
HG_TILE = 128
HG_BAND = 8
HG_LEVELS = (8, 16, 32, 64)


def _cparams(sem, vmem=VMEM_LIMIT):
    return pltpu.CompilerParams(dimension_semantics=sem, vmem_limit_bytes=vmem)


def _split_bf16(a):
    hi = a.astype(BF16)
    lo = (a - hi.astype(F32)).astype(BF16)
    return hi, lo


def _pack_pair(a, b):
    ua = lax.bitcast_convert_type(a.astype(BF16).astype(F32), U32)
    ub = lax.bitcast_convert_type(b.astype(BF16).astype(F32), U32)
    return ua | (ub >> 16)


def _unpack_pair(p):
    a = lax.bitcast_convert_type(p & jnp.uint32(0xFFFF0000), F32).astype(BF16)
    b = lax.bitcast_convert_type(p << 16, F32).astype(BF16)
    return a, b


def _store_slabs(ref, packed):
    n = packed.shape[0]
    for j in range(ROW_SLAB):
        ref[pl.ds(j, n, stride=ROW_SLAB), :] = packed[:, j * LANES:(j + 1) * LANES]


def _load_slab_chunk(ref, j, n):
    return ref[pl.ds(j, n, stride=ROW_SLAB), :]


def _dot(a, b):
    return jnp.dot(a, b, preferred_element_type=F32)


def _dot_nt(a, b):
    return lax.dot_general(a, b, (((1,), (1,)), ((), ())), preferred_element_type=F32)


def _dot_tn(a, b):
    return lax.dot_general(a, b, (((0,), (0,)), ((), ())), preferred_element_type=F32)


def _ada_kernel(c_ref, w_ref, b_ref, o_ref):
    c = c_ref[...]
    cond = c * jax.nn.sigmoid(c)
    o_ref[...] = _dot(cond.astype(BF16), w_ref[...].astype(BF16)) + b_ref[...]


def _ada(c, w_ada, b_ada):
    bsz, d = c.shape
    n = w_ada.shape[1]
    tn = 1024
    return pl.pallas_call(
        _ada_kernel,
        grid=(n // tn,),
        in_specs=[
            pl.BlockSpec((bsz, d), lambda j: (0, 0)),
            pl.BlockSpec((d, tn), lambda j: (0, j)),
            pl.BlockSpec((1, tn), lambda j: (0, j)),
        ],
        out_specs=pl.BlockSpec((bsz, tn), lambda j: (0, j)),
        out_shape=jax.ShapeDtypeStruct((bsz, n), F32),
        compiler_params=_cparams(("arbitrary",)),
        name="ada",
    )(c, w_ada, b_ada.reshape(1, n))


def _modulated_norm(x, mod_ref, nw, shift_row, scale_row):
    r = lax.rsqrt(jnp.mean(x * x, axis=-1, keepdims=True) + EPS)
    scale = mod_ref[0, scale_row:scale_row + 1, :]
    shift = mod_ref[0, shift_row:shift_row + 1, :]
    return (x * r * nw) * (1.0 + scale) + shift


def _inproj_kernel(x_ref, mod_ref, nw_ref, wlat_ref, wrec_ref, lat_ref, rec_ref, h_scr):
    @pl.when(pl.program_id(1) == 0)
    def _():
        h = _modulated_norm(x_ref[...], mod_ref, nw_ref[...], 0, 1).astype(BF16)
        h_scr[...] = h
        lat_ref[...] = _dot(h, wlat_ref[...])

    rec_ref[...] = _dot(h_scr[...], wrec_ref[...])


def _inproj(x2, mod3, norm_w, w_lat, w_rec, seq):
    t, d = x2.shape
    tm, tn = 1024, 512
    per_b = seq // tm
    return pl.pallas_call(
        _inproj_kernel,
        grid=(t // tm, REC_COLS // tn),
        in_specs=[
            pl.BlockSpec((tm, d), lambda i, j: (i, 0)),
            pl.BlockSpec((1, 6, d), lambda i, j: (i // per_b, 0, 0)),
            pl.BlockSpec((1, d), lambda i, j: (0, 0)),
            pl.BlockSpec((d, LAT_PAD), lambda i, j: (0, 0)),
            pl.BlockSpec((d, tn), lambda i, j: (0, j)),
        ],
        out_specs=[
            pl.BlockSpec((tm, LAT_PAD), lambda i, j: (i, 0)),
            pl.BlockSpec((tm, tn), lambda i, j: (i, j)),
        ],
        out_shape=[
            jax.ShapeDtypeStruct((t, LAT_PAD), F32),
            jax.ShapeDtypeStruct((t, REC_COLS), F32),
        ],
        scratch_shapes=[pltpu.VMEM((tm, d), BF16)],
        compiler_params=_cparams(("parallel", "arbitrary")),
        name="inproj",
    )(x2, mod3, norm_w.reshape(1, d), w_lat, w_rec)


def _rope(x2, cos, sin_a, sin_b):
    return (x2 * cos + pltpu.roll(x2, LANES - QK_ROPE // 2, 1) * sin_a
            + pltpu.roll(x2, QK_ROPE // 2, 1) * sin_b)


def _mla_prep_kernel(lat_ref, cos_ref, sa_ref, sb_ref, qlw_ref, wuq_ref, kvlw_ref, wukv_ref,
                     qnw_ref, knw_ref, q_ref, k_ref, v_ref):
    lat = lat_ref[...]
    ql = lat[:, :Q_RANK]
    kvl = lat[:, Q_RANK:Q_RANK + KV_RANK]
    kr = lat[:, Q_RANK + KV_RANK:]
    qn = (ql * lax.rsqrt(jnp.mean(ql * ql, -1, keepdims=True) + EPS) * qlw_ref[...]).astype(BF16)
    kvn = (kvl * lax.rsqrt(jnp.mean(kvl * kvl, -1, keepdims=True) + EPS) * kvlw_ref[...]).astype(BF16)
    cos, sa, sb = cos_ref[...], sa_ref[...], sb_ref[...]
    qnw = qnw_ref[...]
    knw = knw_ref[...]
    kr_ss = jnp.sum(kr * kr, -1, keepdims=True)
    kr_roped = _rope(kr * knw[:, QK_NOPE:], cos, sa, sb)
    scale = QK_DIM ** -0.5
    for h in range(N_HEADS):
        qh = _dot(qn, wuq_ref[h])
        r = lax.rsqrt(jnp.sum(qh * qh, -1, keepdims=True) * (1.0 / QK_DIM) + EPS) * scale
        qh = qh * r * qnw
        q_ref[0, h, :, :QK_NOPE] = qh[:, :QK_NOPE].astype(BF16)
        q_ref[0, h, :, QK_NOPE:] = _rope(qh[:, QK_NOPE:], cos, sa, sb).astype(BF16)
        kvh = _dot(kvn, wukv_ref[h])
        kn = kvh[:, :QK_NOPE]
        rk = lax.rsqrt((jnp.sum(kn * kn, -1, keepdims=True) + kr_ss) * (1.0 / QK_DIM) + EPS)
        k_ref[0, h, :, :QK_NOPE] = (kn * rk * knw[:, :QK_NOPE]).astype(BF16)
        k_ref[0, h, :, QK_NOPE:] = (kr_roped * rk).astype(BF16)
        v_ref[0, h] = kvh[:, QK_NOPE:].astype(BF16)


def _mla_prep(lat, cos, sa, sb, qlw, wuq, kvlw, wukv, qnw, knw, bsz, seq):
    tm = 512
    per_b = seq // tm
    const2 = lambda i: (0, 0)
    const3 = lambda i: (0, 0, 0)
    head_out = lambda w: pl.BlockSpec((1, N_HEADS, tm, w), lambda i: (i // per_b, 0, i % per_b, 0))
    return pl.pallas_call(
        _mla_prep_kernel,
        grid=(bsz * seq // tm,),
        in_specs=[
            pl.BlockSpec((tm, LAT_PAD), lambda i: (i, 0)),
            pl.BlockSpec((tm, LANES), lambda i: (i, 0)),
            pl.BlockSpec((tm, LANES), lambda i: (i, 0)),
            pl.BlockSpec((tm, LANES), lambda i: (i, 0)),
            pl.BlockSpec((1, Q_RANK), const2),
            pl.BlockSpec((N_HEADS, Q_RANK, QK_PAD), const3),
            pl.BlockSpec((1, KV_RANK), const2),
            pl.BlockSpec((N_HEADS, KV_RANK, QK_PAD), const3),
            pl.BlockSpec((1, QK_PAD), const2),
            pl.BlockSpec((1, QK_PAD), const2),
        ],
        out_specs=[head_out(QK_PAD), head_out(QK_PAD), head_out(V_DIM)],
        out_shape=[
            jax.ShapeDtypeStruct((bsz, N_HEADS, seq, QK_PAD), BF16),
            jax.ShapeDtypeStruct((bsz, N_HEADS, seq, QK_PAD), BF16),
            jax.ShapeDtypeStruct((bsz, N_HEADS, seq, V_DIM), BF16),
        ],
        compiler_params=_cparams(("parallel",)),
        name="mla_prep",
    )(lat, cos, sa, sb, qlw, wuq, kvlw, wukv, qnw, knw)


def _attn_kernel(q_ref, k_ref, v_ref, o_ref):
    k = k_ref[0, 0]
    v = v_ref[0, 0]
    for c in range(q_ref.shape[2] // ATT_CHUNK):
        rows = pl.ds(c * ATT_CHUNK, ATT_CHUNK)
        s = _dot_nt(q_ref[0, 0, rows, :], k)
        m = jnp.max(s, axis=-1, keepdims=True)
        p = jnp.exp(s - m)
        l = jnp.sum(p, axis=-1, keepdims=True)
        o_ref[rows, :] = _dot(p.astype(BF16), v) / l


def _attention(q, k, v):
    bsz, nh, seq, _ = q.shape
    tq = 1024
    nq = seq // tq
    return pl.pallas_call(
        _attn_kernel,
        grid=(bsz, nh, nq),
        in_specs=[
            pl.BlockSpec((1, 1, tq, QK_PAD), lambda b, h, i: (b, h, i, 0)),
            pl.BlockSpec((1, 1, seq, QK_PAD), lambda b, h, i: (b, h, 0, 0)),
            pl.BlockSpec((1, 1, seq, V_DIM), lambda b, h, i: (b, h, 0, 0)),
        ],
        out_specs=pl.BlockSpec((tq, V_DIM), lambda b, h, i: (b * nq + i, h)),
        out_shape=jax.ShapeDtypeStruct((bsz * seq, nh * V_DIM), F32),
        compiler_params=_cparams(("parallel", "parallel", "arbitrary")),
        name="attention",
    )(q, k, v)


def _hgrn_constants():
    n = HG_TILE
    t = np.arange(n)[:, None]
    u = np.arange(n)[None, :]
    tri = np.concatenate([(u <= t), (u >= t)], axis=0).astype(np.float32)
    lvl = np.zeros((n, n), np.float32)
    for li, m in enumerate(HG_LEVELS):
        lvl[((t // (2 * m)) == (u // (2 * m))) & ((t // m) != (u // m))] = li + 1
    ones2 = np.zeros((2 * LANES, 2 * LANES), np.float32)
    ones2[:LANES, :LANES] = 1.0
    ones2[LANES:, LANES:] = 1.0
    return tri, lvl, ones2


def _level_stacks(m, q, kkf, kkb, bf, cb):
    zero = jnp.zeros((m, LANES), F32)
    q_rows, k_rows = [], []
    for a in range(0, HG_TILE, 2 * m):
        e, o = slice(a, a + m), slice(a + m, a + 2 * m)
        ref_f = bf[a + m - 1:a + m, :]
        ref_b = cb[a + m:a + m + 1, :]
        wf_e = jnp.exp(ref_f - bf[e])
        wf_o = jnp.exp(bf[o] - ref_f)
        wb_e = jnp.exp(cb[e] - ref_b)
        wb_o = jnp.exp(ref_b - cb[o])
        q_rows.append(jnp.concatenate([zero, q[e] * wb_e], axis=1))
        q_rows.append(jnp.concatenate([q[o] * wf_o, zero], axis=1))
        k_rows.append(jnp.concatenate([kkf[e] * wf_e, zero], axis=1))
        k_rows.append(jnp.concatenate([zero, kkb[o] * wb_o], axis=1))
    return (jnp.concatenate(q_rows, axis=0).astype(BF16), jnp.concatenate(k_rows, axis=0).astype(BF16))


def _hgrn_kernel(q_ref, zf_ref, zb_ref, v_ref, gate_ref, lbf_ref, lbb_ref, nw_ref,
                 tri_ref, lvl_ref, ones_ref, o_ref,
                 ff_scr, fb_scr, kf_scr, kb_scr, v_scr, o_scr,
                 qe_scr, u_scr, dec_scr, st_scr):
    seq = q_ref.shape[0]
    n_tiles = seq // HG_TILE
    pad = HG_BAND
    zeros_pad = jnp.zeros((pad, LANES), F32)
    for scr in (ff_scr, fb_scr, kf_scr, kb_scr, v_scr):
        scr[pl.ds(0, pad), :] = zeros_pad
        scr[pl.ds(pad + seq, pad), :] = zeros_pad
    v_scr[pl.ds(pad, seq), :] = v_ref[...]
    pos = lax.broadcasted_iota(I32, (HG_TILE, 1), 0) % HG_BAND
    lbf = lbf_ref[...]
    lbb = lbb_ref[...]

    def tile_body(i, carry):
        t0 = pl.multiple_of(i * HG_TILE, HG_TILE)
        q = q_ref[pl.ds(t0, HG_TILE), :]
        v = v_scr[pl.ds(pad + t0, HG_TILE), :]
        v_b = v.astype(BF16)
        f_f = lbf + (1.0 - lbf) * jax.nn.sigmoid(zf_ref[pl.ds(t0, HG_TILE), :])
        f_b = lbb + (1.0 - lbb) * jax.nn.sigmoid(zb_ref[pl.ds(t0, HG_TILE), :])
        kkf = 1.0 - f_f
        kkb = 1.0 - f_b
        kf_scr[pl.ds(pad + t0, HG_TILE), :] = kkf
        kb_scr[pl.ds(pad + t0, HG_TILE), :] = kkb
        ff_scr[pl.ds(pad + t0, HG_TILE), :] = jnp.where(pos == 0, 0.0, f_f)
        fb_scr[pl.ds(pad + t0, HG_TILE), :] = jnp.where(pos == HG_BAND - 1, 0.0, f_b)
        kb_scr[pl.ds(pad + t0 + HG_TILE, pad), :] = zeros_pad
        fb_scr[pl.ds(pad + t0 + HG_TILE, pad), :] = zeros_pad
        gf_hi, gf_lo = _split_bf16(jnp.log(f_f))
        gb_hi, gb_lo = _split_bf16(jnp.log(f_b))
        pf = _dot(tri_ref[pl.ds(0, HG_TILE), :], jnp.concatenate([gf_hi, gf_lo], axis=1))
        pb = _dot(tri_ref[pl.ds(HG_TILE, HG_TILE), :], jnp.concatenate([gb_hi, gb_lo], axis=1))
        bf = pf[:, :LANES] + pf[:, LANES:]
        cb = pb[:, :LANES] + pb[:, LANES:]

        last_f = bf[HG_TILE - 1:HG_TILE, :]
        first_b = cb[0:1, :]
        wq_f = jnp.exp(bf)
        wq_b = jnp.exp(cb)
        ke = jnp.concatenate([kkf * jnp.exp(last_f - bf), kkb * jnp.exp(first_b - cb)], axis=1)
        u_scr[i] = _dot_tn(v_b, ke.astype(BF16))
        qe_scr[i] = jnp.concatenate([q * wq_f, q * wq_b], axis=1).astype(BF16)
        dec_scr[pl.ds(i, 1), :] = jnp.concatenate([wq_f[HG_TILE - 1:HG_TILE, :], wq_b[0:1, :]], axis=1)

        lvl = lvl_ref[...]
        a = jnp.zeros((HG_TILE, HG_TILE), F32)
        for li, m in enumerate(HG_LEVELS):
            qs, ks = _level_stacks(m, q, kkf, kkb, bf, cb)
            a = jnp.where(lvl == float(li + 1), _dot_nt(qs, ks), a)
        o = _dot(a.astype(BF16), v_b)

        ones2 = ones_ref[...]
        pf_run = None
        pb_run = None
        for d in range(HG_BAND):
            if d == 1:
                pf_run = ff_scr[pl.ds(pad + t0, HG_TILE), :]
                pb_run = fb_scr[pl.ds(pad + t0, HG_TILE), :]
            elif d > 1:
                pf_run = pf_run * ff_scr[pl.ds(pad + t0 - (d - 1), HG_TILE), :]
                pb_run = pb_run * fb_scr[pl.ds(pad + t0 + (d - 1), HG_TILE), :]
            p_f = q * kf_scr[pl.ds(pad + t0 - d, HG_TILE), :]
            p_b = q * kb_scr[pl.ds(pad + t0 + d, HG_TILE), :]
            if d > 0:
                p_f = p_f * pf_run
                p_b = p_b * pb_run
            rs = _dot(jnp.concatenate([p_f, p_b], axis=1).astype(BF16), ones2)
            o = (o + rs[:, :LANES] * v_scr[pl.ds(pad + t0 - d, HG_TILE), :]
                 + rs[:, LANES:] * v_scr[pl.ds(pad + t0 + d, HG_TILE), :])
        o_scr[pl.ds(t0, HG_TILE), :] = o
        return carry

    lax.fori_loop(0, n_tiles, tile_body, 0, unroll=2)

    def scan_f(i, state):
        st_scr[i] = state.astype(BF16)
        return state * dec_scr[pl.ds(i, 1), :][:, :LANES] + u_scr[i][:, :LANES]

    lax.fori_loop(0, n_tiles, scan_f, jnp.zeros((V_DIM, REC_DIM), F32))

    def scan_b(n, state):
        i = n_tiles - 1 - n
        t0 = pl.multiple_of(i * HG_TILE, HG_TILE)
        both = jnp.concatenate([st_scr[i], state.astype(BF16)], axis=1)
        o_scr[pl.ds(t0, HG_TILE), :] = o_scr[pl.ds(t0, HG_TILE), :] + _dot_nt(qe_scr[i], both)
        return state * dec_scr[pl.ds(i, 1), :][:, LANES:] + u_scr[i][:, LANES:]

    lax.fori_loop(0, n_tiles, scan_b, jnp.zeros((V_DIM, REC_DIM), F32), unroll=4)

    o = o_scr[...]
    o = o * lax.rsqrt(jnp.mean(o * o, -1, keepdims=True) + EPS) * nw_ref[...]
    gate = gate_ref[...]
    o_ref[...] = o * (gate * jax.nn.sigmoid(gate))


def _hgrn(rec, lbf, lbb, rec_norm_w, bsz, seq):
    tri, lvl, ones2 = _hgrn_constants()
    n_tiles = seq // HG_TILE
    col = lambda c: pl.BlockSpec((seq, REC_DIM), lambda b, h: (b, c * N_HEADS + h))
    vec = pl.BlockSpec((1, REC_DIM), lambda b, h: (0, h))
    const = lambda r, c: pl.BlockSpec((r, c), lambda b, h: (0, 0))
    padded = pltpu.VMEM((seq + 2 * HG_BAND, LANES), F32)
    plain = pltpu.VMEM((seq, LANES), F32)
    return pl.pallas_call(
        _hgrn_kernel,
        grid=(bsz, N_HEADS),
        in_specs=[col(0), col(1), col(2), col(3), col(4), vec, vec, const(1, REC_DIM),
                  const(2 * HG_TILE, HG_TILE), const(HG_TILE, HG_TILE), const(2 * LANES, 2 * LANES)],
        out_specs=pl.BlockSpec((seq, REC_DIM), lambda b, h: (b, h)),
        out_shape=jax.ShapeDtypeStruct((bsz * seq, N_HEADS * REC_DIM), F32),
        scratch_shapes=[padded, padded, padded, padded, padded, plain,
                        pltpu.VMEM((n_tiles, HG_TILE, 2 * LANES), BF16),
                        pltpu.VMEM((n_tiles, V_DIM, 2 * LANES), F32),
                        pltpu.VMEM((n_tiles, 2 * LANES), F32),
                        pltpu.VMEM((n_tiles, V_DIM, LANES), BF16)],
        compiler_params=_cparams(("parallel", "parallel")),
        name="hgrn",
    )(rec, rec, rec, rec, rec, lbf, lbb, rec_norm_w.reshape(1, REC_DIM),
      jnp.asarray(tri, BF16), jnp.asarray(lvl, F32), jnp.asarray(ones2, BF16))


def _outproj_kernel(x_ref, att_ref, rec_ref, mod_ref, wa_ref, wr_ref, o_ref):
    y = _dot(att_ref[...].astype(BF16), wa_ref[...]) + _dot(rec_ref[...].astype(BF16), wr_ref[...])
    o_ref[...] = x_ref[...] + mod_ref[0, 2:3, :] * y


def _outproj(x2, att, rec, mod3, w_att, w_rec, seq):
    t, d = x2.shape
    tm = 512
    per_b = seq // tm
    half = att.shape[1]
    return pl.pallas_call(
        _outproj_kernel,
        grid=(t // tm,),
        in_specs=[
            pl.BlockSpec((tm, d), lambda i: (i, 0)),
            pl.BlockSpec((tm, half), lambda i: (i, 0)),
            pl.BlockSpec((tm, half), lambda i: (i, 0)),
            pl.BlockSpec((1, 6, d), lambda i: (i // per_b, 0, 0)),
            pl.BlockSpec((half, d), lambda i: (0, 0)),
            pl.BlockSpec((half, d), lambda i: (0, 0)),
        ],
        out_specs=pl.BlockSpec((tm, d), lambda i: (i, 0)),
        out_shape=jax.ShapeDtypeStruct((t, d), F32),
        compiler_params=_cparams(("parallel",)),
        name="outproj",
    )(x2, att, rec, mod3, w_att, w_rec)


def _router_kernel(x_ref, mod_ref, nw_ref, wrh_ref, wrl_ref, bias_ref,
                   h2b_ref, h2p_ref, idx_ref, wgt_ref, cnt_ref, base_scr):
    i = pl.program_id(0)
    tm = x_ref.shape[0]
    half = x_ref.shape[1] // 2

    @pl.when(i == 0)
    def _():
        base_scr[...] = jnp.zeros_like(base_scr)

    h2 = _modulated_norm(x_ref[...], mod_ref, nw_ref[...], 3, 4)
    h_hi, h_lo = _split_bf16(h2)
    h2b_ref[...] = h_hi
    _store_slabs(h2p_ref, _pack_pair(h2[:, :half], h2[:, half:]))
    logits = _dot_nt(wrh_ref[...], h_hi) + _dot_nt(wrh_ref[...], h_lo) + _dot_nt(wrl_ref[...], h_hi)
    scores = jax.nn.sigmoid(logits)
    sel = scores + bias_ref[...]

    neg = -jnp.inf
    per_grp = N_EXPERTS // N_GROUPS
    sel3 = sel.reshape(N_GROUPS, per_grp, tm)
    within = lax.broadcasted_iota(I32, sel3.shape, 1)
    m1 = jnp.max(sel3, axis=1, keepdims=True)
    first = jnp.min(jnp.where(sel3 == m1, within, per_grp), axis=1, keepdims=True)
    m2 = jnp.max(jnp.where(within == first, neg, sel3), axis=1, keepdims=True)
    gs = (m1 + m2).reshape(N_GROUPS, tm)
    gidx = lax.broadcasted_iota(I32, gs.shape, 0)
    beaten = jnp.zeros(gs.shape, I32)
    for g2 in range(N_GROUPS):
        other = gs[g2:g2 + 1, :]
        beaten = beaten + ((other > gs) | ((other == gs) & (g2 < gidx))).astype(I32)
    keep = (beaten < TOPK_GROUPS).reshape(N_GROUPS, 1, tm)
    cur = jnp.where(keep, sel3, neg).reshape(N_EXPERTS, tm)

    eidx = lax.broadcasted_iota(I32, (N_EXPERTS, tm), 0)
    chosen = jnp.zeros((N_EXPERTS, tm), F32)
    picks, pick_scores = [], []
    for _ in range(TOP_K):
        m = jnp.max(cur, axis=0, keepdims=True)
        pick = jnp.min(jnp.where(cur == m, eidx, N_EXPERTS), axis=0, keepdims=True)
        hit = eidx == pick
        picks.append(pick)
        pick_scores.append(jnp.sum(jnp.where(hit, scores, 0.0), axis=0, keepdims=True))
        chosen = jnp.where(hit, 1.0, chosen)
        cur = jnp.where(hit, neg, cur)

    denom = pick_scores[0]
    for s in pick_scores[1:]:
        denom = denom + s
    for j in range(TOP_K):
        idx_ref[j:j + 1, :] = picks[j]
        wgt_ref[j:j + 1, :] = pick_scores[j] / denom * ROUTED_SCALE
    base_scr[...] = base_scr[...] + jnp.sum(chosen, axis=1, keepdims=True)
    cnt_ref[...] = base_scr[...]


def _router(x1, mod3, norm_w, wr_hi, wr_lo, bias, seq):
    t, d = x1.shape
    tm = 512
    per_b = seq // tm
    row_out = lambda dt: jax.ShapeDtypeStruct((TOP_K, t), dt)
    row_spec = pl.BlockSpec((TOP_K, tm), lambda i: (0, i))
    return pl.pallas_call(
        _router_kernel,
        grid=(t // tm,),
        in_specs=[
            pl.BlockSpec((tm, d), lambda i: (i, 0)),
            pl.BlockSpec((1, 6, d), lambda i: (i // per_b, 0, 0)),
            pl.BlockSpec((1, d), lambda i: (0, 0)),
            pl.BlockSpec((N_EXPERTS, d), lambda i: (0, 0)),
            pl.BlockSpec((N_EXPERTS, d), lambda i: (0, 0)),
            pl.BlockSpec((N_EXPERTS, 1), lambda i: (0, 0)),
        ],
        out_specs=[
            pl.BlockSpec((tm, d), lambda i: (i, 0)),
            pl.BlockSpec((tm * ROW_SLAB, LANES), lambda i: (i, 0)),
            row_spec, row_spec,
            pl.BlockSpec((N_EXPERTS, LANES), lambda i: (0, 0)),
        ],
        out_shape=[
            jax.ShapeDtypeStruct((t, d), BF16),
            jax.ShapeDtypeStruct((t * ROW_SLAB, LANES), U32),
            row_out(I32), row_out(F32),
            jax.ShapeDtypeStruct((N_EXPERTS, LANES), F32),
        ],
        scratch_shapes=[pltpu.VMEM((N_EXPERTS, LANES), F32)],
        compiler_params=_cparams(("arbitrary",)),
        name="router",
    )(x1, mod3, norm_w.reshape(1, d), wr_hi, wr_lo, bias.reshape(N_EXPERTS, 1))


BLOCK_SLABS = MOE_BLOCK * ROW_SLAB


def _gather_block(src_ref, h2p_ref, dst, sem):
    for i in range(MOE_BLOCK):
        src = pl.multiple_of(src_ref[0, 0, i], ROW_SLAB)
        pltpu.make_async_copy(h2p_ref.at[pl.ds(src, ROW_SLAB)], dst.at[pl.ds(i * ROW_SLAB, ROW_SLAB)],
                              sem).start(priority=i % 2)


def _scatter_block(dst_ref, src, out_ref, sem):
    for i in range(MOE_BLOCK):
        dst = pl.multiple_of(dst_ref[0, 0, i], ROW_SLAB)
        pltpu.make_async_copy(src.at[pl.ds(i * ROW_SLAB, ROW_SLAB)], out_ref.at[pl.ds(dst, ROW_SLAB)],
                              sem).start(priority=i % 2)


def _expert_kernel(be_ref, nused_ref, src0_ref, src_ref, dst_ref, h2p_ref, wg_ref, wu_ref, wd_ref, out_ref,
                   wg_scr, wu_scr, wd_scr, x_scr, xbuf, ybuf, xsem, ysem):
    s = pl.program_id(0)
    n_used = nused_ref[0]
    half = ROW_SLAB * LANES
    n_real = out_ref.shape[0] - 2 * BLOCK_SLABS
    slot = lax.rem(s, 2)
    other = 1 - slot
    x_cur = lax.rem(s + 2, 3)
    x_next = lax.rem(s + 1, 3)
    gathered = lambda sem: pltpu.make_async_copy(h2p_ref.at[pl.ds(0, BLOCK_SLABS)], xbuf.at[0], sem)
    scattered = lambda sem: pltpu.make_async_copy(ybuf.at[0], out_ref.at[pl.ds(0, BLOCK_SLABS)], sem)

    @pl.when(s == 0)
    def _():
        ybuf[...] = jnp.zeros_like(ybuf)
        for q in range(2):
            cp = pltpu.make_async_copy(ybuf.at[q], out_ref.at[pl.ds(n_real + q * BLOCK_SLABS, BLOCK_SLABS)],
                                       ysem.at[q])
            cp.start()
            cp.wait()
        pltpu.make_async_copy(ybuf.at[0], out_ref.at[pl.ds(n_real, BLOCK_SLABS)], ysem.at[0]).start()
        _gather_block(src0_ref, h2p_ref, xbuf.at[0], xsem.at[0])
        _gather_block(src_ref, h2p_ref, xbuf.at[1], xsem.at[1])

    @pl.when((s == 0) | (be_ref[s] != be_ref[jnp.maximum(s - 1, 0)]))
    def _():
        wg_scr[...] = wg_ref[...].astype(BF16)
        wu_scr[...] = wu_ref[...].astype(BF16)
        wd_scr[...] = wd_ref[...].astype(BF16)

    @pl.when((s >= 1) & (s <= n_used + 1))
    def _():
        gathered(xsem.at[x_cur]).wait()
        xcur = xbuf.at[x_cur]
        for j in range(ROW_SLAB):
            xa, xb = _unpack_pair(_load_slab_chunk(xcur, j, MOE_BLOCK))
            x_scr[:, j * LANES:(j + 1) * LANES] = xa
            x_scr[:, half + j * LANES:half + (j + 1) * LANES] = xb
        _gather_block(src_ref, h2p_ref, xbuf.at[x_next], xsem.at[x_next])
        _scatter_block(dst_ref, ybuf.at[slot], out_ref, ysem.at[slot])
        xfull = x_scr[...]
        gate = _dot(xfull, wg_scr[...])
        up = _dot(xfull, wu_scr[...])
        mid = ((gate * jax.nn.sigmoid(gate)) * up).astype(BF16)
        y = _dot(mid, wd_scr[...])
        scattered(ysem.at[other]).wait()
        _store_slabs(ybuf.at[other], _pack_pair(y[:, :half], y[:, half:]))

    @pl.when(s == n_used + 2)
    def _():
        gathered(xsem.at[x_cur]).wait()
        gathered(xsem.at[lax.rem(s, 3)]).wait()
        scattered(ysem.at[other]).wait()


def _experts(step_expert, n_used, src_rows, dst_rows, h2p, w_gate, w_up, w_down, n_out_slabs):
    n_steps = step_expert.shape[0]
    n_blocks = n_steps - 3
    d = 2 * ROW_SLAB * LANES
    wspec = lambda r, c: pl.BlockSpec((None, r, c), lambda s, be, nu: (be[s], 0, 0))
    scattered = lambda s, be, nu: (jnp.where(s >= 2, jnp.minimum(s - 2, nu[0] - 1), n_blocks), 0, 0)
    return pl.pallas_call(
        _expert_kernel,
        grid_spec=pltpu.PrefetchScalarGridSpec(
            num_scalar_prefetch=2,
            grid=(n_steps,),
            in_specs=[
                pl.BlockSpec((1, 1, MOE_BLOCK), lambda s, be, nu: (0, 0, 0), memory_space=pltpu.SMEM),
                pl.BlockSpec((1, 1, MOE_BLOCK), lambda s, be, nu: (jnp.minimum(s + 1, nu[0] - 1), 0, 0),
                             memory_space=pltpu.SMEM),
                pl.BlockSpec((1, 1, MOE_BLOCK), scattered, memory_space=pltpu.SMEM),
                pl.BlockSpec(memory_space=pl.ANY),
                wspec(d, EXPERT_DIM), wspec(d, EXPERT_DIM), wspec(EXPERT_DIM, d),
            ],
            out_specs=pl.BlockSpec(memory_space=pl.ANY),
            scratch_shapes=[pltpu.VMEM((d, EXPERT_DIM), BF16), pltpu.VMEM((d, EXPERT_DIM), BF16),
                            pltpu.VMEM((EXPERT_DIM, d), BF16), pltpu.VMEM((MOE_BLOCK, d), BF16),
                            pltpu.VMEM((3, BLOCK_SLABS, LANES), U32),
                            pltpu.VMEM((2, BLOCK_SLABS, LANES), U32),
                            pltpu.SemaphoreType.DMA((3,)), pltpu.SemaphoreType.DMA((2,))],
        ),
        out_shape=jax.ShapeDtypeStruct((n_out_slabs + 2 * BLOCK_SLABS, LANES), U32),
        compiler_params=_cparams(("arbitrary",)),
        name="experts",
    )(step_expert, n_used, src_rows, src_rows, dst_rows, h2p, w_gate, w_up, w_down)


def _combine_kernel(x1_ref, h2b_ref, wgt_ref, mod_ref, wsg_ref, wsu_ref, wsd_ref, *rest):
    slot_refs, o_ref = rest[:TOP_K], rest[TOP_K]
    tm = x1_ref.shape[0]
    h2 = h2b_ref[...]
    gate = _dot(h2, wsg_ref[...])
    up = _dot(h2, wsu_ref[...])
    shared = _dot(((gate * jax.nn.sigmoid(gate)) * up).astype(BF16), wsd_ref[...])

    half = ROW_SLAB * LANES
    wgt = wgt_ref[...]
    g2 = mod_ref[0, 5:6, :]
    for c in range(ROW_SLAB):
        ra = jnp.zeros((tm, LANES), F32)
        rb = jnp.zeros((tm, LANES), F32)
        for j in range(TOP_K):
            ya, yb = _unpack_pair(_load_slab_chunk(slot_refs[j], c, tm))
            wj = wgt[:, j:j + 1]
            ra = ra + wj * ya.astype(F32)
            rb = rb + wj * yb.astype(F32)
        lo = slice(c * LANES, (c + 1) * LANES)
        hi = slice(half + c * LANES, half + (c + 1) * LANES)
        o_ref[:, lo] = x1_ref[:, lo] + g2[:, lo] * (ra + shared[:, lo])
        o_ref[:, hi] = x1_ref[:, hi] + g2[:, hi] * (rb + shared[:, hi])


def _combine(x1, h2b, wgt_t, mod3, ws_gate, ws_up, ws_down, ys, seq):
    t, d = x1.shape
    tm = 256
    per_b = seq // tm
    n_tiles = t // tm
    slot_spec = lambda j: pl.BlockSpec((tm * ROW_SLAB, LANES), lambda i: (j * n_tiles + i, 0))
    return pl.pallas_call(
        _combine_kernel,
        grid=(n_tiles,),
        in_specs=[
            pl.BlockSpec((tm, d), lambda i: (i, 0)),
            pl.BlockSpec((tm, d), lambda i: (i, 0)),
            pl.BlockSpec((tm, TOP_K), lambda i: (i, 0)),
            pl.BlockSpec((1, 6, d), lambda i: (i // per_b, 0, 0)),
            pl.BlockSpec((d, EXPERT_DIM), lambda i: (0, 0)),
            pl.BlockSpec((d, EXPERT_DIM), lambda i: (0, 0)),
            pl.BlockSpec((EXPERT_DIM, d), lambda i: (0, 0)),
        ] + [slot_spec(j) for j in range(TOP_K)],
        out_specs=pl.BlockSpec((tm, d), lambda i: (i, 0)),
        out_shape=jax.ShapeDtypeStruct((t, d), F32),
        compiler_params=_cparams(("parallel",)),
        name="combine",
    )(x1, h2b, wgt_t, mod3, ws_gate, ws_up, ws_down, *([ys] * TOP_K))


def kernel(x, c, positions, w_ada, b_ada, norm1_w, w_in, q_lat_norm_w, w_uq, kv_lat_norm_w, w_ukv, q_norm_w, k_norm_w, lb_fwd, lb_bwd, rec_norm_w, w_out, norm2_w, w_router, router_bias, w_exp_gate, w_exp_up, w_exp_down, w_sh_gate, w_sh_up, w_sh_down):
    bsz, seq, d = x.shape
    t = bsz * seq
    x2 = x.reshape(t, d)

    w_in0 = w_in[0]
    n_lat = Q_RANK + KV_RANK + QK_ROPE
    w_lat = jnp.pad(w_in0[:, :n_lat], ((0, 0), (0, LAT_PAD - n_lat))).astype(BF16)
    w_rec = w_in0[:, n_lat:].astype(BF16)
    wuq = jnp.pad(w_uq[0].reshape(Q_RANK, N_HEADS, QK_DIM), ((0, 0), (0, 0), (0, QK_PAD - QK_DIM)))
    wuq = wuq.transpose(1, 0, 2).astype(BF16)
    wukv = w_ukv[0].reshape(KV_RANK, N_HEADS, QK_NOPE + V_DIM).transpose(1, 0, 2).astype(BF16)
    qnw = jnp.pad(q_norm_w[0], (0, QK_PAD - QK_DIM)).reshape(1, QK_PAD)
    knw = jnp.pad(k_norm_w[0], (0, QK_PAD - QK_DIM)).reshape(1, QK_PAD)
    half = N_HEADS * V_DIM
    w_out_att = w_out[0, :half].astype(BF16)
    w_out_rec = w_out[0, half:].astype(BF16)
    wr_hi, wr_lo = _split_bf16(w_router[0].T)
    lbf = jnp.cumsum(jax.nn.softmax(lb_fwd.astype(F32), axis=0), axis=0)[0].reshape(1, -1)
    lbb = jnp.cumsum(jax.nn.softmax(lb_bwd.astype(F32), axis=0), axis=0)[0].reshape(1, -1)

    inv_freq = ROPE_THETA ** (-jnp.arange(0, QK_ROPE, 2, dtype=F32) / QK_ROPE)
    ang = positions.astype(F32).reshape(t, 1) * inv_freq[None, :]
    cos_h, sin_h = jnp.cos(ang), jnp.sin(ang)
    zh = jnp.zeros_like(cos_h)
    cos_t = jnp.concatenate([cos_h, cos_h, zh, zh], axis=-1)
    sin_a = jnp.concatenate([-sin_h, zh, zh, zh], axis=-1)
    sin_b = jnp.concatenate([zh, sin_h, zh, zh], axis=-1)

    mod3 = _ada(c, w_ada[0], b_ada[0]).reshape(bsz, 6, d)
    lat, rec_in = _inproj(x2, mod3, norm1_w[0], w_lat, w_rec, seq)
    q, k, v = _mla_prep(lat, cos_t, sin_a, sin_b, q_lat_norm_w[0].reshape(1, -1), wuq,
                        kv_lat_norm_w[0].reshape(1, -1), wukv, qnw, knw, bsz, seq)
    att = _attention(q, k, v)
    rec = _hgrn(rec_in, lbf, lbb, rec_norm_w[0], bsz, seq)
    x1 = _outproj(x2, att, rec, mod3, w_out_att, w_out_rec, seq)

    h2b, h2p, top_idx, top_w, counts = _router(x1, mod3, norm2_w[0], wr_hi, wr_lo, router_bias[0], seq)
    counts = counts[:, 0].astype(I32)
    padded = (counts + MOE_BLOCK - 1) // MOE_BLOCK * MOE_BLOCK
    pad_end = jnp.cumsum(padded)
    experts = jnp.arange(N_EXPERTS, dtype=I32)
    n_blocks = -(-(t * TOP_K) // MOE_BLOCK) + N_EXPERTS
    n_used = pad_end[-1] // MOE_BLOCK
    blk_row0 = jnp.minimum(jnp.arange(n_blocks, dtype=I32), n_used - 1) * MOE_BLOCK
    blk_expert = jnp.sum((pad_end[None, :] <= blk_row0[:, None]).astype(I32), axis=-1)
    step_blk = jnp.clip(jnp.arange(n_blocks + 3, dtype=I32) - 1, 0, n_used - 1)
    step_expert = jnp.sum(jnp.where(step_blk[:, None] == jnp.arange(n_blocks, dtype=I32)[None, :],
                                    blk_expert[None, :], 0), axis=-1)
    n_assign = t * TOP_K
    filler = jnp.arange(MOE_BLOCK - 1, dtype=I32)[None, :] < (padded - counts)[:, None]
    filler_keys = jnp.where(filler, experts[:, None], N_EXPERTS).reshape(-1)
    keys = jnp.concatenate([top_idx.T.reshape(-1), filler_keys]).astype(I32)
    idx_bits = int(keys.shape[0] - 1).bit_length()
    packed = jnp.sort(keys * (1 << idx_bits) + jnp.arange(keys.shape[0], dtype=I32))
    order = packed & ((1 << idx_bits) - 1)
    order = jnp.pad(order, (0, n_blocks * MOE_BLOCK - order.shape[0]), constant_values=n_assign)
    real = order < n_assign
    pos = jnp.arange(n_blocks * MOE_BLOCK, dtype=I32)
    zone_slab = n_assign + ((pos // MOE_BLOCK) % 2) * MOE_BLOCK + pos % MOE_BLOCK
    src_rows = jnp.where(real, (order // TOP_K) * ROW_SLAB, 0).reshape(n_blocks, 1, MOE_BLOCK)
    dst_rows = jnp.where(real, (order % TOP_K) * t + order // TOP_K, zone_slab) * ROW_SLAB
    all_filler = (n_assign + MOE_BLOCK + jnp.arange(MOE_BLOCK, dtype=I32)) * ROW_SLAB
    dst_rows = jnp.concatenate([dst_rows, all_filler]).reshape(n_blocks + 1, 1, MOE_BLOCK)

    ys = _experts(step_expert.astype(I32), n_used.reshape(1).astype(I32), src_rows, dst_rows, h2p,
                  w_exp_gate[0], w_exp_up[0], w_exp_down[0], n_assign * ROW_SLAB)
    out = _combine(x1, h2b, top_w.T, mod3, w_sh_gate[0].astype(BF16), w_sh_up[0].astype(BF16),
                   w_sh_down[0].astype(BF16), ys, seq)
    return out.reshape(bsz, seq, d)
```
